```python
import jax
import jax.numpy as jnp
from jax import lax
import numpy as np

D_MODEL = 2048
BATCH = 8
SEQ = 4096
DEPTH = 2

D_MIX = D_MODEL
W_POOL = D_MIX // 4
W_CONV = D_MIX // 4
W_DSA = D_MIX // 4
W_NSA = D_MIX - W_POOL - W_CONV - W_DSA
HEAD_DIM = 128
POOL_WINDOWS = (2, 4, 8, 16)
N_POOL_GROUPS = len(POOL_WINDOWS)
POOL_GROUP = W_POOL // N_POOL_GROUPS
CONV_WIDTH = 31
DSA_HEADS = W_DSA // HEAD_DIM
IDX_HEADS = 8
IDX_DIM = 64
DSA_TOPK_MAX = 256
NSA_HEADS = W_NSA // HEAD_DIM
NSA_CMP_BLOCK = 32
NSA_CMP_STRIDE = 16
NSA_SEL_BLOCK = 64
NSA_SEL_TOPK = 16
NSA_WINDOW = 512
Q_BLOCK = 128
D_FF = ((8 * D_MODEL + 3 * 256 - 1) // (3 * 256)) * 256
DN_ALPHA = (2 * DEPTH) ** 0.25
DN_BETA = (8 * DEPTH) ** -0.25
ATTN_SCALE = HEAD_DIM ** -0.5
IDX_SCALE = (IDX_HEADS * IDX_DIM) ** -0.5

IN_SIZES = (
    W_POOL,
    W_CONV, W_CONV,
    W_DSA, HEAD_DIM, HEAD_DIM,
    IDX_HEADS * IDX_DIM, IDX_DIM, IDX_HEADS,
    W_NSA,
    HEAD_DIM, HEAD_DIM,
    HEAD_DIM, HEAD_DIM,
    HEAD_DIM, HEAD_DIM,
    NSA_HEADS * 3,
)
IN_OFFSETS = tuple(int(v) for v in np.cumsum(IN_SIZES)[:-1])
D_IN = int(sum(IN_SIZES))

kernel_name = 'hybrid_pool_conv_dsa_nsa_deepnorm'


def _alibi_slopes():
    n = DSA_HEADS + NSA_HEADS
    s = np.power(2.0, -8.0 * np.arange(1, n + 1) / n).astype(np.float32)
    return jnp.asarray(s[0::2]), jnp.asarray(s[1::2])


def _layernorm(x, g, b, eps=1e-5):
    xf = x.astype(jnp.float32)
    mu = jnp.mean(xf, axis=-1, keepdims=True)
    var = jnp.mean(jnp.square(xf - mu), axis=-1, keepdims=True)
    y = (xf - mu) * lax.rsqrt(var + eps) * g.astype(jnp.float32) + b.astype(jnp.float32)
    return y.astype(x.dtype)


def _masked_softmax(s, mask):
    s = jnp.where(mask, s, -jnp.inf)
    m = jnp.max(s, axis=-1, keepdims=True)
    m = jnp.where(jnp.isfinite(m), m, 0.0)
    e = jnp.where(mask, jnp.exp(s - m), 0.0)
    return e / jnp.maximum(jnp.sum(e, axis=-1, keepdims=True), 1e-30)


def _sweep_blocks(fn, n_tokens):
    n_blocks = n_tokens // Q_BLOCK
    out = lax.map(fn, jnp.arange(n_blocks) * Q_BLOCK)
    out = jnp.moveaxis(out, 0, 1)
    return out.reshape((out.shape[0], n_blocks * Q_BLOCK) + out.shape[3:])


def _pool_mixer(u, w, scale):
    B, S, _ = u.shape
    ug = u.astype(jnp.float32).reshape(B, S, N_POOL_GROUPS, POOL_GROUP)
    cs = jnp.cumsum(ug, axis=1)
    pos = jnp.arange(S)
    outs = []
    for gi, win in enumerate(POOL_WINDOWS):
        c = cs[:, :, gi]
        prev = jnp.pad(c, ((0, 0), (win, 0), (0, 0)))[:, :S]
        cnt = jnp.minimum(pos + 1, win).astype(jnp.float32)[None, :, None]
        outs.append((c - prev) / cnt - ug[:, :, gi])
    d = jnp.stack(outs, axis=2)
    y = jnp.einsum('bsgc,gcd->bsgd', d, w.astype(jnp.float32)).reshape(B, S, W_POOL)
    return (y * scale.astype(jnp.float32)).astype(u.dtype)


def _conv_module(a, gate, conv_w, conv_b, ln_g, ln_b, pw_w, pw_b):
    h = a * jax.nn.sigmoid(gate)
    hp = jnp.pad(h, ((0, 0), (CONV_WIDTH - 1, 0), (0, 0)))
    h = lax.conv_general_dilated(hp, conv_w[:, None, :], window_strides=(1,), padding='VALID',
                                 dimension_numbers=('NWC', 'WIO', 'NWC'),
                                 feature_group_count=W_CONV) + conv_b
    h = jax.nn.silu(_layernorm(h, ln_g, ln_b))
    return h @ pw_w + pw_b


def _dsa_attention(q, k, v, iq, ik, iw, slopes):
    B, S, _ = q.shape
    q = q.reshape(B, S, DSA_HEADS, HEAD_DIM)
    iq = iq.reshape(B, S, IDX_HEADS, IDX_DIM)
    iw = iw.astype(jnp.float32) * IDX_SCALE
    topk = min(DSA_TOPK_MAX, S // 4)
    s_pos = jnp.arange(S)
    b_ix = jnp.arange(B)[:, None, None]

    def block(t0):
        t_pos = t0 + jnp.arange(Q_BLOCK)
        qb = lax.dynamic_slice_in_dim(q, t0, Q_BLOCK, axis=1)
        iqb = lax.dynamic_slice_in_dim(iq, t0, Q_BLOCK, axis=1)
        iwb = lax.dynamic_slice_in_dim(iw, t0, Q_BLOCK, axis=1)
        logits = jnp.einsum('bthd,bsd->bths', iqb, ik).astype(jnp.float32)
        score = jnp.einsum('bths,bth->bts', jax.nn.relu(logits), iwb)
        score = jnp.where(s_pos[None, None, :] <= t_pos[None, :, None], score, -jnp.inf)
        _, idx = lax.top_k(score, topk)
        kg = k[b_ix, idx]
        vg = v[b_ix, idx]
        dist = (t_pos[None, :, None] - idx).astype(jnp.float32)
        a = jnp.einsum('bthd,btjd->bhtj', qb, kg).astype(jnp.float32) * ATTN_SCALE
        a = a - slopes[None, :, None, None] * dist[:, None]
        p = _masked_softmax(a, (dist >= 0)[:, None])
        return jnp.einsum('bhtj,btjd->bthd', p.astype(vg.dtype), vg)

    return _sweep_blocks(block, S).reshape(B, S, W_DSA)


def _compress_blocks(raw, blk_idx, pos, w1, w2):
    B = raw.shape[0]
    blocks = raw[:, blk_idx] + pos
    h = jax.nn.gelu(blocks.reshape(B, blk_idx.shape[0], -1) @ w1)
    return h @ w2


def _nsa_attention(q, ck, cv, sk, sv, wk, wv, g, pos_k, pos_v, k_w1, k_w2, v_w1, v_w2, slopes):
    B, S, _ = q.shape
    q = q.reshape(B, S, NSA_HEADS, HEAD_DIM)
    t_all = jnp.arange(S)
    n_cmp = (S - NSA_CMP_BLOCK) // NSA_CMP_STRIDE + 1
    starts = np.arange(n_cmp) * NSA_CMP_STRIDE
    blk_idx = starts[:, None] + np.arange(NSA_CMP_BLOCK)[None, :]
    ends = blk_idx[:, -1]
    k_cmp = _compress_blocks(ck, blk_idx, pos_k, k_w1, k_w2)
    v_cmp = _compress_blocks(cv, blk_idx, pos_v, v_w1, v_w2)
    cd = (t_all[:, None] - ends[None, :]).astype(jnp.float32)
    a = jnp.einsum('bthd,bcd->bhtc', q, k_cmp).astype(jnp.float32) * ATTN_SCALE
    a = a - slopes[:, None, None] * cd
    p_cmp = _masked_softmax(a, cd >= 0)
    o_cmp = jnp.einsum('bhtc,bcd->bthd', p_cmp.astype(v_cmp.dtype), v_cmp)
    n_sel = S // NSA_SEL_BLOCK
    sel_start = np.arange(n_sel) * NSA_SEL_BLOCK
    overlap = ((starts[:, None] < sel_start[None, :] + NSA_SEL_BLOCK)
               & (ends[:, None] >= sel_start[None, :])).astype(np.float32)
    imp = jnp.einsum('bhtc,cn->btn', p_cmp, overlap)
    t_blk = t_all // NSA_SEL_BLOCK
    j = jnp.arange(n_sel)
    forced = (j[None, :] == 0) | (j[None, :] == t_blk[:, None]) | (j[None, :] == t_blk[:, None] - 1)
    imp = jnp.where(forced[None], jnp.inf, imp)
    imp = jnp.where((j[None, :] <= t_blk[:, None])[None], imp, -jnp.inf)
    n_top = min(NSA_SEL_TOPK, n_sel)
    _, sel_idx = lax.top_k(imp, n_top)
    ks_b = sk.reshape(B, n_sel, NSA_SEL_BLOCK, HEAD_DIM)
    vs_b = sv.reshape(B, n_sel, NSA_SEL_BLOCK, HEAD_DIM)
    wk_p = jnp.pad(wk, ((0, 0), (NSA_WINDOW, 0), (0, 0)))
    wv_p = jnp.pad(wv, ((0, 0), (NSA_WINDOW, 0), (0, 0)))
    b_ix = jnp.arange(B)[:, None, None]
    in_blk = jnp.arange(NSA_SEL_BLOCK)
    win_off = jnp.arange(Q_BLOCK + NSA_WINDOW) - NSA_WINDOW

    def block(t0):
        t_pos = t0 + jnp.arange(Q_BLOCK)
        qb = lax.dynamic_slice_in_dim(q, t0, Q_BLOCK, axis=1)
        idx = lax.dynamic_slice_in_dim(sel_idx, t0, Q_BLOCK, axis=1)
        kg = ks_b[b_ix, idx].reshape(B, Q_BLOCK, n_top * NSA_SEL_BLOCK, HEAD_DIM)
        vg = vs_b[b_ix, idx].reshape(B, Q_BLOCK, n_top * NSA_SEL_BLOCK, HEAD_DIM)
        kpos = (idx[..., None] * NSA_SEL_BLOCK + in_blk).reshape(B, Q_BLOCK, -1)
        sd = (t_pos[None, :, None] - kpos).astype(jnp.float32)
        a = jnp.einsum('bthd,btjd->bhtj', qb, kg).astype(jnp.float32) * ATTN_SCALE
        a = a - slopes[None, :, None, None] * sd[:, None]
        o_slc = jnp.einsum('bhtj,btjd->bthd', _masked_softmax(a, (sd >= 0)[:, None]).astype(vg.dtype), vg)
        kw_b = lax.dynamic_slice_in_dim(wk_p, t0, Q_BLOCK + NSA_WINDOW, axis=1)
        vw_b = lax.dynamic_slice_in_dim(wv_p, t0, Q_BLOCK + NSA_WINDOW, axis=1)
        wpos = t0 + win_off
        wd = (t_pos[:, None] - wpos[None, :]).astype(jnp.float32)
        wmask = (wd >= 0) & (wd < NSA_WINDOW) & (wpos[None, :] >= 0)
        a = jnp.einsum('bthd,bkd->bhtk', qb, kw_b).astype(jnp.float32) * ATTN_SCALE
        a = a - slopes[:, None, None] * wd
        o_win = jnp.einsum('bhtk,bkd->bthd', _masked_softmax(a, wmask).astype(vw_b.dtype), vw_b)
        return jnp.stack([o_slc, o_win], axis=2)

    o_sw = _sweep_blocks(block, S)
    gates = jax.nn.sigmoid(g.reshape(B, S, NSA_HEADS, 3))
    o = gates[..., 0:1] * o_cmp + gates[..., 1:2] * o_sw[:, :, 0] + gates[..., 2:3] * o_sw[:, :, 1]
    return o.reshape(B, S, W_NSA)


def setup_inputs(seed: int = 0) -> dict:
    key = jax.random.key(seed)
    ks = jax.random.split(key, 24)
    L = DEPTH

    def nrm(k, shape, scale):
        return jax.random.normal(k, shape, jnp.float32) * scale

    return {
        'x': nrm(ks[0], (BATCH, SEQ, D_MODEL), 1.0),
        'w_in': nrm(ks[1], (L, D_MODEL, D_IN), D_MODEL ** -0.5),
        'w_out': nrm(ks[2], (L, D_MIX, D_MODEL), D_MIX ** -0.5 * DN_BETA),
        'pool_w': nrm(ks[3], (L, N_POOL_GROUPS, POOL_GROUP, POOL_GROUP), POOL_GROUP ** -0.5),
        'pool_scale': 1.0 + nrm(ks[4], (L, W_POOL), 0.1),
        'conv_w': nrm(ks[5], (L, CONV_WIDTH, W_CONV), CONV_WIDTH ** -0.5),
        'conv_b': nrm(ks[6], (L, W_CONV), 0.02),
        'conv_ln_g': 1.0 + nrm(ks[7], (L, W_CONV), 0.05),
        'conv_ln_b': nrm(ks[8], (L, W_CONV), 0.02),
        'conv_pw_w': nrm(ks[9], (L, W_CONV, W_CONV), W_CONV ** -0.5),
        'conv_pw_b': nrm(ks[10], (L, W_CONV), 0.02),
        'cmp_pos_k': nrm(ks[11], (L, NSA_CMP_BLOCK, HEAD_DIM), 0.1),
        'cmp_pos_v': nrm(ks[12], (L, NSA_CMP_BLOCK, HEAD_DIM), 0.1),
        'cmp_k_w1': nrm(ks[13], (L, NSA_CMP_BLOCK * HEAD_DIM, HEAD_DIM), (NSA_CMP_BLOCK * HEAD_DIM) ** -0.5),
        'cmp_k_w2': nrm(ks[14], (L, HEAD_DIM, HEAD_DIM), HEAD_DIM ** -0.5),
        'cmp_v_w1': nrm(ks[15], (L, NSA_CMP_BLOCK * HEAD_DIM, HEAD_DIM), (NSA_CMP_BLOCK * HEAD_DIM) ** -0.5),
        'cmp_v_w2': nrm(ks[16], (L, HEAD_DIM, HEAD_DIM), HEAD_DIM ** -0.5),
        'ln1_g': 1.0 + nrm(ks[17], (L, D_MODEL), 0.05),
        'ln1_b': nrm(ks[18], (L, D_MODEL), 0.02),
        'ln2_g': 1.0 + nrm(ks[19], (L, D_MODEL), 0.05),
        'ln2_b': nrm(ks[20], (L, D_MODEL), 0.02),
        'w_gate_up': nrm(ks[21], (L, D_MODEL, 2 * D_FF), D_MODEL ** -0.5),
        'w_down': nrm(ks[22], (L, D_FF, D_MODEL), D_FF ** -0.5 * DN_BETA),
    }


def reference(x, w_in, w_out, pool_w, pool_scale, conv_w, conv_b, conv_ln_g, conv_ln_b,
              conv_pw_w, conv_pw_b, cmp_pos_k, cmp_pos_v, cmp_k_w1, cmp_k_w2, cmp_v_w1, cmp_v_w2,
              ln1_g, ln1_b, ln2_g, ln2_b, w_gate_up, w_down):
    slopes_dsa, slopes_nsa = _alibi_slopes()
    for l in range(DEPTH):
        (u_pool, c_a, c_g, d_q, d_k, d_v, i_q, i_k, i_w,
         n_q, n_ck, n_cv, n_sk, n_sv, n_wk, n_wv, n_g) = jnp.split(x @ w_in[l], IN_OFFSETS, axis=-1)
        y_pool = _pool_mixer(u_pool, pool_w[l], pool_scale[l])
        y_conv = _conv_module(c_a, c_g, conv_w[l], conv_b[l], conv_ln_g[l], conv_ln_b[l],
                              conv_pw_w[l], conv_pw_b[l])
        y_dsa = _dsa_attention(d_q, d_k, d_v, i_q, i_k, i_w, slopes_dsa)
        y_nsa = _nsa_attention(n_q, n_ck, n_cv, n_sk, n_sv, n_wk, n_wv, n_g,
                               cmp_pos_k[l], cmp_pos_v[l], cmp_k_w1[l], cmp_k_w2[l],
                               cmp_v_w1[l], cmp_v_w2[l], slopes_nsa)
        mix = jnp.concatenate([y_pool, y_conv, y_dsa, y_nsa], axis=-1)
        x = _layernorm(DN_ALPHA * x + mix @ w_out[l], ln1_g[l], ln1_b[l])
        gate, up = jnp.split(x @ w_gate_up[l], 2, axis=-1)
        x = _layernorm(DN_ALPHA * x + (jax.nn.silu(gate) * up) @ w_down[l], ln2_g[l], ln2_b[l])
    return x
```

```python
import functools

import numpy as np
import jax
import jax.numpy as jnp
from jax import lax
from jax.experimental import pallas as pl
from jax.experimental.pallas import tpu as pltpu

F32 = jnp.float32
BF16 = jnp.bfloat16
I32 = jnp.int32

D_MODEL = 2048
DEPTH = 2
W_MIX = D_MODEL // 4
HEAD_DIM = 128
N_HEADS = W_MIX // HEAD_DIM
POOL_WINDOWS = (2, 4, 8, 16)
POOL_GROUP = W_MIX // len(POOL_WINDOWS)
CONV_WIDTH = 31
IDX_HEADS = 8
IDX_DIM = 64
DSA_TOPK_MAX = 256
CMP_BLOCK = 32
CMP_STRIDE = 16
SEL_BLOCK = 64
SEL_TOPK = 16
NSA_WINDOW = 512
D_FF = ((8 * D_MODEL + 3 * 256 - 1) // (3 * 256)) * 256
DN_ALPHA = (2 * DEPTH) ** 0.25
ATTN_SCALE = HEAD_DIM ** -0.5
IDX_SCALE = (IDX_HEADS * IDX_DIM) ** -0.5
LN_EPS = 1e-5

LANES = 128
VMEM_LIMIT = 48 * 1024 * 1024

NEG = -1e30
INT_MIN = -2 ** 31

_SRC_SIZES = (W_MIX, W_MIX, W_MIX, W_MIX, HEAD_DIM, HEAD_DIM, IDX_HEADS * IDX_DIM, IDX_DIM, IDX_HEADS,
              W_MIX, HEAD_DIM, HEAD_DIM, HEAD_DIM, HEAD_DIM, HEAD_DIM, HEAD_DIM, N_HEADS * 3)
_SRC_NAMES = ('pool', 'c_a', 'c_g', 'd_q', 'd_k', 'd_v', 'i_q', 'i_k', 'i_w',
              'n_q', 'n_ck', 'n_cv', 'n_sk', 'n_sv', 'n_wk', 'n_wv', 'n_g')
_SRC_OFF = dict(zip(_SRC_NAMES, np.concatenate([[0], np.cumsum(_SRC_SIZES)[:-1]]).tolist()))
_SRC_SIZE = dict(zip(_SRC_NAMES, _SRC_SIZES))
_PACK_ORDER = ('pool', 'c_a', 'c_g', 'd_q', 'i_q', 'n_q', 'd_k', 'd_v',
               'n_ck', 'n_cv', 'n_sk', 'n_sv', 'n_wk', 'n_wv', 'i_k', 'i_w', 'n_g')
_PACK_OFF = {}
_o = 0
for _n in _PACK_ORDER:
    _PACK_OFF[_n] = _o
    _o += _SRC_SIZE[_n]
D_PACK = ((_o + LANES - 1) // LANES) * LANES
SMALL_OFF = _PACK_OFF['i_k']
IK_LANE = 0
IW_LANE = _PACK_OFF['i_w'] - SMALL_OFF
NG_LANE = _PACK_OFF['n_g'] - SMALL_OFF


def _alibi_slopes():
    n = 2 * N_HEADS
    s = np.power(2.0, -8.0 * np.arange(1, n + 1) / n).astype(np.float32)
    return [float(v) for v in s[0::2]], [float(v) for v in s[1::2]]


SLOPES_DSA, SLOPES_NSA = _alibi_slopes()


def _cparams(sem):
    return pltpu.CompilerParams(dimension_semantics=sem, vmem_limit_bytes=VMEM_LIMIT)


def _resident(shape):
    nd = len(shape)
    return pl.BlockSpec(shape, lambda *_: (0,) * nd, pipeline_mode=pl.Buffered(1))


def _dot(a, b):
    return jnp.dot(a, b, preferred_element_type=F32)


def _dot_nt(a, b):
    return lax.dot_general(a, b, (((1,), (1,)), ((), ())), preferred_element_type=F32)


def _layernorm(x, g, b):
    mu = jnp.mean(x, axis=-1, keepdims=True)
    xc = x - mu
    var = jnp.mean(xc * xc, axis=-1, keepdims=True)
    return xc * lax.rsqrt(var + LN_EPS) * g + b


def _softmax_rows(a, mask):
    a = jnp.where(mask, a, NEG)
    m = jnp.max(a, axis=-1, keepdims=True)
    e = jnp.where(mask, jnp.exp(a - m), 0.0)
    s = jnp.sum(e, axis=-1, keepdims=True)
    return e / jnp.maximum(s, 1e-30)


INPROJ_TM = 512
INPROJ_CHUNK = 512


def _inproj_kernel(x_ref, w_ref, o_ref):
    x = x_ref[...]
    n = o_ref.shape[1]
    for c0 in range(0, n, INPROJ_CHUNK):
        c1 = min(c0 + INPROJ_CHUNK, n)
        o_ref[:, c0:c1] = _dot(x, w_ref[:, c0:c1]).astype(BF16)


def _inproj(xb, w_pack):
    n_tok, d = xb.shape
    tm = INPROJ_TM
    return pl.pallas_call(
        _inproj_kernel,
        out_shape=jax.ShapeDtypeStruct((n_tok, D_PACK), BF16),
        grid=(n_tok // tm,),
        in_specs=[pl.BlockSpec((tm, d), lambda i: (i, 0)), _resident((d, D_PACK))],
        out_specs=pl.BlockSpec((tm, D_PACK), lambda i: (i, 0)),
        compiler_params=_cparams(("parallel",)),
        name="inproj",
    )(xb, w_pack)


POOL_TS = 512
POOL_HALO = 16


def _pool_kernel(u_ref, halo_ref, w_ref, sc_ref, o_ref, xs_ref):
    i = pl.program_id(1)
    ts = u_ref.shape[1]
    xs_ref[POOL_HALO:POOL_HALO + ts, :] = u_ref[0].astype(F32)
    xs_ref[0:POOL_HALO, :] = jnp.where(i > 0, halo_ref[0].astype(F32), 0.0)
    pos = i * ts + lax.broadcasted_iota(I32, (ts, 1), 0)
    for g, win in enumerate(POOL_WINDOWS):
        c = slice(g * POOL_GROUP, (g + 1) * POOL_GROUP)
        x = xs_ref[POOL_HALO:POOL_HALO + ts, c]
        acc = x
        for k in range(1, win):
            acc = acc + xs_ref[POOL_HALO - k:POOL_HALO - k + ts, c]
        cnt = jnp.minimum(pos + 1, win).astype(F32)
        d = acc / cnt - x
        y = _dot(d.astype(BF16), w_ref[g])
        o_ref[0, :, c] = (y * sc_ref[:, c]).astype(BF16)


def _pool_mixer(proj, pool_w, pool_scale):
    b, s, _ = proj.shape
    ts = min(POOL_TS, s)
    hb = ts // POOL_HALO
    blk = _PACK_OFF['pool'] // W_MIX
    return pl.pallas_call(
        _pool_kernel,
        out_shape=jax.ShapeDtypeStruct((b, s, W_MIX), BF16),
        grid=(b, s // ts),
        in_specs=[
            pl.BlockSpec((1, ts, W_MIX), lambda bi, i: (bi, i, blk)),
            pl.BlockSpec((1, POOL_HALO, W_MIX),
                         lambda bi, i: (bi, jnp.maximum(i * hb - 1, 0), blk * (W_MIX // W_MIX))),
            _resident(pool_w.shape),
            _resident((1, W_MIX)),
        ],
        out_specs=pl.BlockSpec((1, ts, W_MIX), lambda bi, i: (bi, i, 0)),
        scratch_shapes=[pltpu.VMEM((ts + POOL_HALO, W_MIX), F32)],
        compiler_params=_cparams(("parallel", "parallel")),
        name="pool_mixer",
    )(proj, proj, pool_w.astype(BF16), pool_scale.reshape(1, W_MIX).astype(F32))


CONV_TS = 256
CONV_HALO = 32
CONV_ROWS = 32


def _conv_kernel(a_ref, g_ref, ha_ref, hg_ref, cw_ref, cb_ref, lg_ref, lb_ref, pw_ref, pb_ref,
                 o_ref, hs_ref, y_ref):
    i = pl.program_id(1)
    ts = a_ref.shape[1]
    hs_ref[CONV_HALO:CONV_HALO + ts, :] = a_ref[0].astype(F32) * jax.nn.sigmoid(g_ref[0].astype(F32))
    halo = ha_ref[0].astype(F32) * jax.nn.sigmoid(hg_ref[0].astype(F32))
    hs_ref[0:CONV_HALO, :] = jnp.where(i > 0, halo, 0.0)
    base = CONV_HALO - (CONV_WIDTH - 1)
    for r0 in range(0, ts, CONV_ROWS):
        acc = jnp.broadcast_to(cb_ref[...], (CONV_ROWS, W_MIX))
        for j in range(CONV_WIDTH):
            acc = acc + hs_ref[base + r0 + j:base + r0 + j + CONV_ROWS, :] * cw_ref[j:j + 1, :]
        y = _layernorm(acc, lg_ref[...], lb_ref[...])
        y_ref[r0:r0 + CONV_ROWS, :] = (y * jax.nn.sigmoid(y)).astype(BF16)
    o_ref[0] = (_dot(y_ref[...], pw_ref[...]) + pb_ref[...]).astype(BF16)


def _conv_mixer(proj, conv_w, conv_b, ln_g, ln_b, pw_w, pw_b):
    b, s, _ = proj.shape
    ts = min(CONV_TS, s)
    hb = ts // CONV_HALO
    ba = _PACK_OFF['c_a'] // W_MIX
    bg = _PACK_OFF['c_g'] // W_MIX
    row = lambda v: v.reshape(1, W_MIX).astype(F32)
    cw = jnp.concatenate([conv_w.astype(F32), jnp.zeros((1, W_MIX), F32)], axis=0)
    halo_map = lambda blk: (lambda bi, i: (bi, jnp.maximum(i * hb - 1, 0), blk))
    return pl.pallas_call(
        _conv_kernel,
        out_shape=jax.ShapeDtypeStruct((b, s, W_MIX), BF16),
        grid=(b, s // ts),
        in_specs=[
            pl.BlockSpec((1, ts, W_MIX), lambda bi, i: (bi, i, ba)),
            pl.BlockSpec((1, ts, W_MIX), lambda bi, i: (bi, i, bg)),
            pl.BlockSpec((1, CONV_HALO, W_MIX), halo_map(ba)),
            pl.BlockSpec((1, CONV_HALO, W_MIX), halo_map(bg)),
            _resident(cw.shape), _resident((1, W_MIX)), _resident((1, W_MIX)), _resident((1, W_MIX)),
            _resident((W_MIX, W_MIX)), _resident((1, W_MIX)),
        ],
        out_specs=pl.BlockSpec((1, ts, W_MIX), lambda bi, i: (bi, i, 0)),
        scratch_shapes=[pltpu.VMEM((ts + CONV_HALO, W_MIX), F32), pltpu.VMEM((ts, W_MIX), BF16)],
        compiler_params=_cparams(("parallel", "parallel")),
        name="conv_mixer",
    )(proj, proj, proj, proj, cw, row(conv_b), row(ln_g), row(ln_b), pw_w.astype(BF16), row(pw_b))


def _compress_kernel(r_ref, pos_ref, w1_ref, w2_ref, o_ref):
    r = r_ref[0]
    half = r.shape[1]
    n = r.shape[0]
    top = _dot(r, w1_ref[0:half, :])
    bot = _dot(r, w1_ref[half:2 * half, :])
    bot_next = pltpu.roll(bot, n - 1, 0)
    posb = _dot(jnp.broadcast_to(pos_ref[...], (8, 2 * half)), w1_ref[...])[0:1, :]
    h = jax.nn.gelu(top + bot_next + posb)
    out = _dot(h.astype(BF16), w2_ref[...])
    row = lax.broadcasted_iota(I32, out.shape, 0)
    o_ref[0] = jnp.where(row < n - 1, out, 0.0).astype(BF16)


def _compress(raw, pos, w1, w2):
    b, s, d = raw.shape
    n = s // CMP_STRIDE
    r = raw.reshape(b, n, CMP_STRIDE * d)
    return pl.pallas_call(
        _compress_kernel,
        out_shape=jax.ShapeDtypeStruct((b, n, d), BF16),
        grid=(b,),
        in_specs=[pl.BlockSpec((1, n, CMP_STRIDE * d), lambda bi: (bi, 0, 0)),
                  _resident((1, CMP_BLOCK * d)), _resident((CMP_BLOCK * d, d)), _resident((d, d))],
        out_specs=pl.BlockSpec((1, n, d), lambda bi: (bi, 0, 0)),
        compiler_params=_cparams(("parallel",)),
        name="nsa_compress",
    )(r, pos.reshape(1, CMP_BLOCK * d).astype(BF16), w1.astype(BF16), w2.astype(BF16))


ATT_TQ = 128
ATT_TK = 512


def _split3(x):
    hi = x.astype(BF16)
    r1 = x - hi.astype(F32)
    mid = r1.astype(BF16)
    lo = (r1 - mid.astype(F32)).astype(BF16)
    return hi, mid, lo


def _nsa_kernel(q_ref, sm_ref, kc_ref, vc_ref, sk_ref, sv_ref, wk_ref, wv_ref, o_ref,
                q4_ref, p4_ref, pw_ref):
    tq = q_ref.shape[1]
    s_len = sk_ref.shape[1]
    n_cmp = kc_ref.shape[1]
    n_sel = s_len // SEL_BLOCK
    n_top = min(SEL_TOPK, n_sel)
    t0 = pl.program_id(1) * tq
    for h in range(N_HEADS):
        q4_ref[h * tq:(h + 1) * tq, :] = q_ref[0, :, h * HEAD_DIM:(h + 1) * HEAD_DIM]
    q4 = q4_ref[...]
    t_col = t0 + lax.broadcasted_iota(I32, (tq, 1), 0)

    a_all = _dot_nt(q4, kc_ref[0]) * ATTN_SCALE
    c_idx = lax.broadcasted_iota(I32, (1, n_cmp), 1)
    cd = t_col - (c_idx * CMP_STRIDE + (CMP_BLOCK - 1))
    cmask = (cd >= 0) & (c_idx < n_cmp - 1)
    cdf = cd.astype(F32)
    p_sum = jnp.zeros((tq, n_cmp), F32)
    for h in range(N_HEADS):
        p = _softmax_rows(a_all[h * tq:(h + 1) * tq] - SLOPES_NSA[h] * cdf, cmask)
        p_sum = p_sum + p
        p4_ref[h * tq:(h + 1) * tq, 0:n_cmp] = p.astype(BF16)
    o_cmp = _dot(p4_ref[:, 0:n_cmp], vc_ref[0])

    cc = lax.broadcasted_iota(I32, (n_cmp, n_sel), 0)
    jj = lax.broadcasted_iota(I32, (n_cmp, n_sel), 1)
    c_start = cc * CMP_STRIDE
    overlap = ((c_start < (jj + 1) * SEL_BLOCK) & (c_start + (CMP_BLOCK - 1) >= jj * SEL_BLOCK)
               & (cc < n_cmp - 1))
    ov = jnp.where(overlap, 1.0, 0.0).astype(BF16)
    hi, mid, lo = _split3(p_sum)
    imp = _dot(hi, ov) + _dot(mid, ov) + _dot(lo, ov)
    j_row = lax.broadcasted_iota(I32, (1, n_sel), 1)
    t_blk = t_col // SEL_BLOCK
    forced = (j_row == 0) | (j_row == t_blk) | (j_row == t_blk - 1)
    imp = jnp.where(forced, jnp.inf, imp)
    imp = jnp.where(j_row <= t_blk, imp, -jnp.inf)
    rank = jnp.zeros((tq, n_sel), F32)
    for i2 in range(n_sel):
        ci = imp[:, i2:i2 + 1]
        beats = (ci > imp) | ((ci == imp) & (j_row > i2))
        rank = rank + jnp.where(beats, 1.0, 0.0)
    selm = jnp.where(rank < n_top, 1.0, 0.0).astype(BF16)

    tk = min(ATT_TK, s_len)
    n_kt = (t0 + tq - 1) // tk + 1
    e_row = lax.broadcasted_iota(I32, (n_sel, tk), 0)
    e_col = lax.broadcasted_iota(I32, (n_sel, tk), 1)

    def sel_body(kt, carry):
        ms, ls, acc = carry
        s0 = pl.multiple_of(kt * tk, tk)
        kk = sk_ref[0, pl.ds(s0, tk), :]
        vv = sv_ref[0, pl.ds(s0, tk), :]
        a = _dot_nt(q4, kk) * ATTN_SCALE
        s_pos = s0 + lax.broadcasted_iota(I32, (1, tk), 1)
        dist = t_col - s_pos
        expand = jnp.where(e_row == (s0 + e_col) // SEL_BLOCK, 1.0, 0.0).astype(BF16)
        mask = (dist >= 0) & (_dot(selm, expand) > 0.5)
        distf = dist.astype(F32)
        new_ms, new_ls, alphas = [], [], []
        for h in range(N_HEADS):
            ah = jnp.where(mask, a[h * tq:(h + 1) * tq] - SLOPES_NSA[h] * distf, NEG)
            m_new = jnp.maximum(ms[h], jnp.max(ah, axis=-1, keepdims=True))
            alpha = jnp.exp(ms[h] - m_new)
            p = jnp.where(mask, jnp.exp(ah - m_new), 0.0)
            new_ls.append(alpha * ls[h] + jnp.sum(p, axis=-1, keepdims=True))
            new_ms.append(m_new)
            alphas.append(jnp.broadcast_to(alpha, (tq, HEAD_DIM)))
            p4_ref[h * tq:(h + 1) * tq, 0:tk] = p.astype(BF16)
        acc = jnp.concatenate(alphas, axis=0) * acc + _dot(p4_ref[:, 0:tk], vv)
        return tuple(new_ms), tuple(new_ls), acc

    m0 = tuple(jnp.full((tq, 1), NEG, F32) for _ in range(N_HEADS))
    l0 = tuple(jnp.zeros((tq, 1), F32) for _ in range(N_HEADS))
    _, ls, acc = lax.fori_loop(0, n_kt, sel_body, (m0, l0, jnp.zeros((N_HEADS * tq, HEAD_DIM), F32)))

    wlen = min(NSA_WINDOW + tq, s_len)
    ks = pl.multiple_of(jnp.maximum(t0 + tq - wlen, 0), tq)
    kw = wk_ref[0, pl.ds(ks, wlen), :]
    vw = wv_ref[0, pl.ds(ks, wlen), :]
    aw = _dot_nt(q4, kw) * ATTN_SCALE
    wd = t_col - (ks + lax.broadcasted_iota(I32, (1, wlen), 1))
    wmask = (wd >= 0) & (wd < NSA_WINDOW)
    wdf = wd.astype(F32)
    for h in range(N_HEADS):
        p = _softmax_rows(aw[h * tq:(h + 1) * tq] - SLOPES_NSA[h] * wdf, wmask)
        pw_ref[h * tq:(h + 1) * tq, :] = p.astype(BF16)
    o_win = _dot(pw_ref[...], vw)

    gates = jax.nn.sigmoid(sm_ref[0].astype(F32))
    for h in range(N_HEADS):
        rows = slice(h * tq, (h + 1) * tq)
        o_slc = acc[rows] / jnp.maximum(ls[h], 1e-30)
        g0 = gates[:, NG_LANE + 3 * h:NG_LANE + 3 * h + 1]
        g1 = gates[:, NG_LANE + 3 * h + 1:NG_LANE + 3 * h + 2]
        g2 = gates[:, NG_LANE + 3 * h + 2:NG_LANE + 3 * h + 3]
        o = g0 * o_cmp[rows] + g1 * o_slc + g2 * o_win[rows]
        o_ref[0, :, h * HEAD_DIM:(h + 1) * HEAD_DIM] = o.astype(BF16)


def _nsa_mixer(proj, k_cmp, v_cmp):
    b, s, _ = proj.shape
    tq = min(ATT_TQ, s)
    tk = min(ATT_TK, s)
    n_cmp = k_cmp.shape[1]
    wlen = min(NSA_WINDOW + tq, s)
    col = lambda name: _PACK_OFF[name] // HEAD_DIM
    full = lambda name: pl.BlockSpec((1, s, HEAD_DIM), functools.partial(lambda c, bi, i: (bi, 0, c), col(name)))
    return pl.pallas_call(
        _nsa_kernel,
        out_shape=jax.ShapeDtypeStruct((b, s, W_MIX), BF16),
        grid=(b, s // tq),
        in_specs=[
            pl.BlockSpec((1, tq, W_MIX), lambda bi, i: (bi, i, _PACK_OFF['n_q'] // W_MIX)),
            pl.BlockSpec((1, tq, LANES), lambda bi, i: (bi, i, SMALL_OFF // LANES)),
            pl.BlockSpec((1, n_cmp, HEAD_DIM), lambda bi, i: (bi, 0, 0)),
            pl.BlockSpec((1, n_cmp, HEAD_DIM), lambda bi, i: (bi, 0, 0)),
            full('n_sk'), full('n_sv'), full('n_wk'), full('n_wv'),
        ],
        out_specs=pl.BlockSpec((1, tq, W_MIX), lambda bi, i: (bi, i, 0)),
        scratch_shapes=[pltpu.VMEM((N_HEADS * tq, HEAD_DIM), BF16),
                        pltpu.VMEM((N_HEADS * tq, max(tk, n_cmp)), BF16),
                        pltpu.VMEM((N_HEADS * tq, wlen), BF16)],
        compiler_params=_cparams(("parallel", "arbitrary")),
        name="nsa_attention",
    )(proj, proj, k_cmp, v_cmp, proj, proj, proj, proj)


def _dsa_kernel(q_ref, iq_ref, sm_ref, smf_ref, k_ref, v_ref, o_ref, q4_ref, p4_ref, key_ref):
    tq = q_ref.shape[1]
    s_len = k_ref.shape[1]
    topk = min(DSA_TOPK_MAX, s_len // 4)
    tk = min(ATT_TK, s_len)
    t0 = pl.program_id(1) * tq
    n_kt = (t0 + tq - 1) // tk + 1
    for h in range(N_HEADS):
        q4_ref[h * tq:(h + 1) * tq, :] = q_ref[0, :, h * HEAD_DIM:(h + 1) * HEAD_DIM]
    q4 = q4_ref[...]
    t_col = t0 + lax.broadcasted_iota(I32, (tq, 1), 0)
    iq = iq_ref[0]
    iw = sm_ref[0].astype(F32) * IDX_SCALE

    def score_body(kt, _):
        s0 = pl.multiple_of(kt * tk, tk)
        ik = smf_ref[0, pl.ds(s0, tk), IK_LANE:IK_LANE + IDX_DIM]
        sc = jnp.zeros((tq, tk), F32)
        for h in range(IDX_HEADS):
            lg = _dot_nt(iq[:, h * IDX_DIM:(h + 1) * IDX_DIM], ik)
            sc = sc + jnp.maximum(lg, 0.0) * iw[:, IW_LANE + h:IW_LANE + h + 1]
        sc = jnp.where(sc == 0.0, 0.0, sc)
        bits = lax.bitcast_convert_type(sc, I32)
        key = bits ^ ((bits >> 31) & 0x7FFFFFFF)
        s_pos = s0 + lax.broadcasted_iota(I32, (1, tk), 1)
        key_ref[:, pl.ds(s0, tk)] = jnp.where(s_pos <= t_col, key, INT_MIN)
        return 0

    lax.fori_loop(0, n_kt, score_body, 0)

    def bit_body(it, cand):
        trial = cand | jnp.left_shift(jnp.int32(1), 31 - it)
        thr = trial ^ INT_MIN

        def cnt_body(kt, cnt):
            s0 = pl.multiple_of(kt * tk, tk)
            for c in range(tk // LANES):
                kc = key_ref[:, pl.ds(s0 + c * LANES, LANES)]
                cnt = cnt + jnp.where(kc >= thr, 1.0, 0.0)
            return cnt

        cnt = lax.fori_loop(0, n_kt, cnt_body, jnp.zeros((tq, LANES), F32))
        total = jnp.sum(cnt, axis=-1, keepdims=True)
        return jnp.where(total >= topk, trial, cand)

    cand = lax.fori_loop(0, 32, bit_body, jnp.zeros((tq, LANES), I32))
    tau = cand ^ INT_MIN

    def att_body(kt, carry):
        ms, ls, acc = carry
        s0 = pl.multiple_of(kt * tk, tk)
        kk = k_ref[0, pl.ds(s0, tk), :]
        vv = v_ref[0, pl.ds(s0, tk), :]
        a = _dot_nt(q4, kk) * ATTN_SCALE
        s_pos = s0 + lax.broadcasted_iota(I32, (1, tk), 1)
        dist = t_col - s_pos
        sel = jnp.concatenate(
            [key_ref[:, pl.ds(s0 + c * LANES, LANES)] >= tau for c in range(tk // LANES)], axis=1)
        mask = (dist >= 0) & sel
        distf = dist.astype(F32)
        new_ms, new_ls, alphas = [], [], []
        for h in range(N_HEADS):
            ah = jnp.where(mask, a[h * tq:(h + 1) * tq] - SLOPES_DSA[h] * distf, NEG)
            m_new = jnp.maximum(ms[h], jnp.max(ah, axis=-1, keepdims=True))
            alpha = jnp.exp(ms[h] - m_new)
            p = jnp.where(mask, jnp.exp(ah - m_new), 0.0)
            new_ls.append(alpha * ls[h] + jnp.sum(p, axis=-1, keepdims=True))
            new_ms.append(m_new)
            alphas.append(jnp.broadcast_to(alpha, (tq, HEAD_DIM)))
            p4_ref[h * tq:(h + 1) * tq, :] = p.astype(BF16)
        acc = jnp.concatenate(alphas, axis=0) * acc + _dot(p4_ref[...], vv)
        return tuple(new_ms), tuple(new_ls), acc

    m0 = tuple(jnp.full((tq, 1), NEG, F32) for _ in range(N_HEADS))
    l0 = tuple(jnp.zeros((tq, 1), F32) for _ in range(N_HEADS))
    _, ls, acc = lax.fori_loop(0, n_kt, att_body, (m0, l0, jnp.zeros((N_HEADS * tq, HEAD_DIM), F32)))
    for h in range(N_HEADS):
        o = acc[h * tq:(h + 1) * tq] / jnp.maximum(ls[h], 1e-30)
        o_ref[0, :, h * HEAD_DIM:(h + 1) * HEAD_DIM] = o.astype(BF16)


def _dsa_mixer(proj):
    b, s, _ = proj.shape
    tq = min(ATT_TQ, s)
    tk = min(ATT_TK, s)
    full = lambda off: pl.BlockSpec((1, s, LANES), functools.partial(lambda c, bi, i: (bi, 0, c), off // LANES))
    return pl.pallas_call(
        _dsa_kernel,
        out_shape=jax.ShapeDtypeStruct((b, s, W_MIX), BF16),
        grid=(b, s // tq),
        in_specs=[
            pl.BlockSpec((1, tq, W_MIX), lambda bi, i: (bi, i, _PACK_OFF['d_q'] // W_MIX)),
            pl.BlockSpec((1, tq, W_MIX), lambda bi, i: (bi, i, _PACK_OFF['i_q'] // W_MIX)),
            pl.BlockSpec((1, tq, LANES), lambda bi, i: (bi, i, SMALL_OFF // LANES)),
            full(SMALL_OFF), full(_PACK_OFF['d_k']), full(_PACK_OFF['d_v']),
        ],
        out_specs=pl.BlockSpec((1, tq, W_MIX), lambda bi, i: (bi, i, 0)),
        scratch_shapes=[pltpu.VMEM((N_HEADS * tq, HEAD_DIM), BF16),
                        pltpu.VMEM((N_HEADS * tq, tk), BF16),
                        pltpu.VMEM((tq, s), I32)],
        compiler_params=_cparams(("parallel", "arbitrary")),
        name="dsa_attention",
    )(proj, proj, proj, proj, proj, proj)


OUTPROJ_TM = 256


def _outproj_kernel(y0_ref, y1_ref, y2_ref, y3_ref, w_ref, x_ref, g_ref, b_ref, o_ref, ob_ref):
    acc = DN_ALPHA * x_ref[...]
    for gi, y_ref in enumerate((y0_ref, y1_ref, y2_ref, y3_ref)):
        acc = acc + _dot(y_ref[...], w_ref[gi * W_MIX:(gi + 1) * W_MIX, :])
    y = _layernorm(acc, g_ref[...], b_ref[...])
    o_ref[...] = y
    ob_ref[...] = y.astype(BF16)


def _outproj_ln(ys, w_out, x, g, b):
    n_tok, d = x.shape
    tm = OUTPROJ_TM
    yspec = pl.BlockSpec((tm, W_MIX), lambda i: (i, 0))
    xspec = pl.BlockSpec((tm, d), lambda i: (i, 0))
    return pl.pallas_call(
        _outproj_kernel,
        out_shape=(jax.ShapeDtypeStruct((n_tok, d), F32), jax.ShapeDtypeStruct((n_tok, d), BF16)),
        grid=(n_tok // tm,),
        in_specs=[yspec, yspec, yspec, yspec, _resident((d, d)), xspec, _resident((1, d)), _resident((1, d))],
        out_specs=(xspec, xspec),
        compiler_params=_cparams(("parallel",)),
        name="outproj_ln",
    )(*ys, w_out, x, g.reshape(1, d).astype(F32), b.reshape(1, d).astype(F32))


FFN_TM = 512
FFN_TF = 512


def _ffn_kernel(xb_ref, wg_ref, wu_ref, wd_ref, x_ref, g_ref, b_ref, o_ref, ob_ref, acc_ref):
    f = pl.program_id(1)

    @pl.when(f == 0)
    def _():
        acc_ref[...] = DN_ALPHA * x_ref[...]

    xb = xb_ref[...]
    gate = _dot(xb, wg_ref[...])
    up = _dot(xb, wu_ref[...])
    h = (gate * jax.nn.sigmoid(gate) * up).astype(BF16)
    acc_ref[...] += _dot(h, wd_ref[...])

    @pl.when(f == pl.num_programs(1) - 1)
    def _():
        y = _layernorm(acc_ref[...], g_ref[...], b_ref[...])
        o_ref[...] = y
        ob_ref[...] = y.astype(BF16)


def _ffn_ln(xb, w_gate_up, w_down, x, g, b):
    n_tok, d = x.shape
    d_ff = w_down.shape[0]
    tm, tf = FFN_TM, FFN_TF
    nf = d_ff // tf
    xspec = pl.BlockSpec((tm, d), lambda i, f: (i, 0))
    return pl.pallas_call(
        _ffn_kernel,
        out_shape=(jax.ShapeDtypeStruct((n_tok, d), F32), jax.ShapeDtypeStruct((n_tok, d), BF16)),
        grid=(n_tok // tm, nf),
        in_specs=[
            xspec,
            pl.BlockSpec((d, tf), lambda i, f: (0, f)),
            pl.BlockSpec((d, tf), lambda i, f: (0, f + nf)),
            pl.BlockSpec((tf, d), lambda i, f: (f, 0)),
            xspec, _resident((1, d)), _resident((1, d)),
        ],
        out_specs=(xspec, xspec),
        scratch_shapes=[pltpu.VMEM((tm, d), F32)],
        compiler_params=_cparams(("parallel", "arbitrary")),
        name="ffn_ln",
    )(xb, w_gate_up, w_gate_up, w_down, x, g.reshape(1, d).astype(F32), b.reshape(1, d).astype(F32))


def _pack_w_in(w):
    cols = [w[:, _SRC_OFF[n]:_SRC_OFF[n] + _SRC_SIZE[n]] for n in _PACK_ORDER]
    used = sum(_SRC_SIZE[n] for n in _PACK_ORDER)
    cols.append(jnp.zeros((w.shape[0], D_PACK - used), w.dtype))
    return jnp.concatenate(cols, axis=1).astype(BF16)


def _layer(x, xb, p):
    b, s, d = x.shape
    n_tok = b * s
    proj = _inproj(xb.reshape(n_tok, d), _pack_w_in(p['w_in'])).reshape(b, s, D_PACK)
    y_pool = _pool_mixer(proj, p['pool_w'], p['pool_scale'])
    y_conv = _conv_mixer(proj, p['conv_w'], p['conv_b'], p['conv_ln_g'], p['conv_ln_b'],
                         p['conv_pw_w'], p['conv_pw_b'])
    y_dsa = _dsa_mixer(proj)
    ck = proj[:, :, _PACK_OFF['n_ck']:_PACK_OFF['n_ck'] + HEAD_DIM]
    cv = proj[:, :, _PACK_OFF['n_cv']:_PACK_OFF['n_cv'] + HEAD_DIM]
    k_cmp = _compress(ck, p['cmp_pos_k'], p['cmp_k_w1'], p['cmp_k_w2'])
    v_cmp = _compress(cv, p['cmp_pos_v'], p['cmp_v_w1'], p['cmp_v_w2'])
    y_nsa = _nsa_mixer(proj, k_cmp, v_cmp)
    ys = [y.reshape(n_tok, W_MIX) for y in (y_pool, y_conv, y_dsa, y_nsa)]
    x1, x1b = _outproj_ln(ys, p['w_out'].astype(BF16), x.reshape(n_tok, d), p['ln1_g'], p['ln1_b'])
    x2, x2b = _ffn_ln(x1b, p['w_gate_up'].astype(BF16), p['w_down'].astype(BF16), x1, p['ln2_g'], p['ln2_b'])
    return x2.reshape(b, s, d), x2b.reshape(b, s, d)


def kernel(x, w_in, w_out, pool_w, pool_scale, conv_w, conv_b, conv_ln_g, conv_ln_b, conv_pw_w, conv_pw_b,
           cmp_pos_k, cmp_pos_v, cmp_k_w1, cmp_k_w2, cmp_v_w1, cmp_v_w2, ln1_g, ln1_b, ln2_g, ln2_b,
           w_gate_up, w_down):
    params = dict(w_in=w_in, w_out=w_out, pool_w=pool_w, pool_scale=pool_scale, conv_w=conv_w, conv_b=conv_b,
                  conv_ln_g=conv_ln_g, conv_ln_b=conv_ln_b, conv_pw_w=conv_pw_w, conv_pw_b=conv_pw_b,
                  cmp_pos_k=cmp_pos_k, cmp_pos_v=cmp_pos_v, cmp_k_w1=cmp_k_w1, cmp_k_w2=cmp_k_w2,
                  cmp_v_w1=cmp_v_w1, cmp_v_w2=cmp_v_w2, ln1_g=ln1_g, ln1_b=ln1_b, ln2_g=ln2_g, ln2_b=ln2_b,
                  w_gate_up=w_gate_up, w_down=w_down)
    xb = x.astype(BF16)
    for l in range(w_in.shape[0]):
        x, xb = _layer(x, xb, {k: v[l] for k, v in params.items()})
    return x
```

```python
import functools

import numpy as np
import jax
import jax.numpy as jnp
from jax import lax
from jax.experimental import pallas as pl
from jax.experimental.pallas import tpu as pltpu

F32 = jnp.float32
BF16 = jnp.bfloat16
I32 = jnp.int32

D_MODEL = 2048
DEPTH = 2
W_MIX = D_MODEL // 4
HEAD_DIM = 128
N_HEADS = W_MIX // HEAD_DIM
POOL_WINDOWS = (2, 4, 8, 16)
POOL_GROUP = W_MIX // len(POOL_WINDOWS)
CONV_WIDTH = 31
IDX_HEADS = 8
IDX_DIM = 64
DSA_TOPK_MAX = 256
CMP_BLOCK = 32
CMP_STRIDE = 16
SEL_BLOCK = 64
SEL_TOPK = 16
NSA_WINDOW = 512
D_FF = ((8 * D_MODEL + 3 * 256 - 1) // (3 * 256)) * 256
DN_ALPHA = (2 * DEPTH) ** 0.25
ATTN_SCALE = HEAD_DIM ** -0.5
IDX_SCALE = (IDX_HEADS * IDX_DIM) ** -0.5
LN_EPS = 1e-5
LOG2E = 1.4426950408889634

LANES = 128
VMEM_LIMIT = 48 * 1024 * 1024

NEG = -1e30
INT_MIN = -2 ** 31

_SRC_SIZES = (W_MIX, W_MIX, W_MIX, W_MIX, HEAD_DIM, HEAD_DIM, IDX_HEADS * IDX_DIM, IDX_DIM, IDX_HEADS,
              W_MIX, HEAD_DIM, HEAD_DIM, HEAD_DIM, HEAD_DIM, HEAD_DIM, HEAD_DIM, N_HEADS * 3)
_SRC_NAMES = ('pool', 'c_a', 'c_g', 'd_q', 'd_k', 'd_v', 'i_q', 'i_k', 'i_w',
              'n_q', 'n_ck', 'n_cv', 'n_sk', 'n_sv', 'n_wk', 'n_wv', 'n_g')
_SRC_OFF = dict(zip(_SRC_NAMES, np.concatenate([[0], np.cumsum(_SRC_SIZES)[:-1]]).tolist()))
_SRC_SIZE = dict(zip(_SRC_NAMES, _SRC_SIZES))
_PACK_ORDER = ('pool', 'c_a', 'c_g', 'd_q', 'i_q', 'n_q', 'd_k', 'd_v',
               'n_ck', 'n_cv', 'n_sk', 'n_sv', 'n_wk', 'n_wv', 'i_k', 'i_w', 'n_g')
_PACK_OFF = {}
_o = 0
for _n in _PACK_ORDER:
    _PACK_OFF[_n] = _o
    _o += _SRC_SIZE[_n]
D_PACK = ((_o + LANES - 1) // LANES) * LANES
SMALL_OFF = _PACK_OFF['i_k']
IK_LANE = 0
IW_LANE = _PACK_OFF['i_w'] - SMALL_OFF
NG_LANE = _PACK_OFF['n_g'] - SMALL_OFF


def _alibi_slopes():
    n = 2 * N_HEADS
    s = np.power(2.0, -8.0 * np.arange(1, n + 1) / n).astype(np.float32)
    return [float(v) for v in s[0::2]], [float(v) for v in s[1::2]]


SLOPES_DSA, SLOPES_NSA = _alibi_slopes()


def _cparams(sem):
    return pltpu.CompilerParams(dimension_semantics=sem, vmem_limit_bytes=VMEM_LIMIT)


def _resident(shape):
    nd = len(shape)
    return pl.BlockSpec(shape, lambda *_: (0,) * nd, pipeline_mode=pl.Buffered(1))


def _dot(a, b):
    return jnp.dot(a, b, preferred_element_type=F32)


def _dot_nt(a, b):
    return lax.dot_general(a, b, (((1,), (1,)), ((), ())), preferred_element_type=F32)


def _layernorm(x, g, b):
    mu = jnp.mean(x, axis=-1, keepdims=True)
    xc = x - mu
    var = jnp.mean(xc * xc, axis=-1, keepdims=True)
    return xc * lax.rsqrt(var + LN_EPS) * g + b


INPROJ_TM = 512
INPROJ_CHUNK = 512


def _inproj_kernel(x_ref, w_ref, o_ref):
    x = x_ref[...]
    n = o_ref.shape[1]
    for c0 in range(0, n, INPROJ_CHUNK):
        c1 = min(c0 + INPROJ_CHUNK, n)
        o_ref[:, c0:c1] = _dot(x, w_ref[:, c0:c1]).astype(BF16)


def _inproj(xb, w_pack):
    n_tok, d = xb.shape
    tm = INPROJ_TM
    return pl.pallas_call(
        _inproj_kernel,
        out_shape=jax.ShapeDtypeStruct((n_tok, D_PACK), BF16),
        grid=(n_tok // tm,),
        in_specs=[pl.BlockSpec((tm, d), lambda i: (i, 0)), _resident((d, D_PACK))],
        out_specs=pl.BlockSpec((tm, D_PACK), lambda i: (i, 0)),
        compiler_params=_cparams(("parallel",)),
        name="inproj",
    )(xb, w_pack)


POOL_TS = 512
POOL_HALO = 16


def _pool_kernel(u_ref, halo_ref, w_ref, sc_ref, o_ref, xs_ref):
    i = pl.program_id(1)
    ts = u_ref.shape[1]
    xs_ref[POOL_HALO:POOL_HALO + ts, :] = u_ref[0].astype(F32)
    xs_ref[0:POOL_HALO, :] = jnp.where(i > 0, halo_ref[0].astype(F32), 0.0)
    pos = i * ts + lax.broadcasted_iota(I32, (ts, 1), 0)
    for g, win in enumerate(POOL_WINDOWS):
        c = slice(g * POOL_GROUP, (g + 1) * POOL_GROUP)
        x = xs_ref[POOL_HALO:POOL_HALO + ts, c]
        acc = x
        for k in range(1, win):
            acc = acc + xs_ref[POOL_HALO - k:POOL_HALO - k + ts, c]
        cnt = jnp.minimum(pos + 1, win).astype(F32)
        d = acc / cnt - x
        y = _dot(d.astype(BF16), w_ref[g])
        o_ref[0, :, c] = (y * sc_ref[:, c]).astype(BF16)


def _pool_mixer(proj, pool_w, pool_scale):
    b, s, _ = proj.shape
    ts = min(POOL_TS, s)
    hb = ts // POOL_HALO
    blk = _PACK_OFF['pool'] // W_MIX
    return pl.pallas_call(
        _pool_kernel,
        out_shape=jax.ShapeDtypeStruct((b, s, W_MIX), BF16),
        grid=(b, s // ts),
        in_specs=[
            pl.BlockSpec((1, ts, W_MIX), lambda bi, i: (bi, i, blk)),
            pl.BlockSpec((1, POOL_HALO, W_MIX), lambda bi, i: (bi, jnp.maximum(i * hb - 1, 0), blk)),
            _resident(pool_w.shape),
            _resident((1, W_MIX)),
        ],
        out_specs=pl.BlockSpec((1, ts, W_MIX), lambda bi, i: (bi, i, 0)),
        scratch_shapes=[pltpu.VMEM((ts + POOL_HALO, W_MIX), F32)],
        compiler_params=_cparams(("parallel", "parallel")),
        name="pool_mixer",
    )(proj, proj, pool_w.astype(BF16), pool_scale.reshape(1, W_MIX).astype(F32))


CONV_TS = 256
CONV_HALO = 32
CONV_ROWS = 32


def _conv_kernel(a_ref, g_ref, ha_ref, hg_ref, cw_ref, cb_ref, lg_ref, lb_ref, pw_ref, pb_ref,
                 o_ref, hs_ref, y_ref):
    i = pl.program_id(1)
    ts = a_ref.shape[1]
    hs_ref[CONV_HALO:CONV_HALO + ts, :] = a_ref[0].astype(F32) * jax.nn.sigmoid(g_ref[0].astype(F32))
    halo = ha_ref[0].astype(F32) * jax.nn.sigmoid(hg_ref[0].astype(F32))
    hs_ref[0:CONV_HALO, :] = jnp.where(i > 0, halo, 0.0)
    base = CONV_HALO - (CONV_WIDTH - 1)
    for r0 in range(0, ts, CONV_ROWS):
        acc = jnp.broadcast_to(cb_ref[...], (CONV_ROWS, W_MIX))
        for j in range(CONV_WIDTH):
            acc = acc + hs_ref[base + r0 + j:base + r0 + j + CONV_ROWS, :] * cw_ref[j:j + 1, :]
        y = _layernorm(acc, lg_ref[...], lb_ref[...])
        y_ref[r0:r0 + CONV_ROWS, :] = (y * jax.nn.sigmoid(y)).astype(BF16)
    o_ref[0] = (_dot(y_ref[...], pw_ref[...]) + pb_ref[...]).astype(BF16)


def _conv_mixer(proj, conv_w, conv_b, ln_g, ln_b, pw_w, pw_b):
    b, s, _ = proj.shape
    ts = min(CONV_TS, s)
    hb = ts // CONV_HALO
    ba = _PACK_OFF['c_a'] // W_MIX
    bg = _PACK_OFF['c_g'] // W_MIX
    row = lambda v: v.reshape(1, W_MIX).astype(F32)
    cw = jnp.concatenate([conv_w.astype(F32), jnp.zeros((1, W_MIX), F32)], axis=0)
    halo_map = lambda blk: (lambda bi, i: (bi, jnp.maximum(i * hb - 1, 0), blk))
    return pl.pallas_call(
        _conv_kernel,
        out_shape=jax.ShapeDtypeStruct((b, s, W_MIX), BF16),
        grid=(b, s // ts),
        in_specs=[
            pl.BlockSpec((1, ts, W_MIX), lambda bi, i: (bi, i, ba)),
            pl.BlockSpec((1, ts, W_MIX), lambda bi, i: (bi, i, bg)),
            pl.BlockSpec((1, CONV_HALO, W_MIX), halo_map(ba)),
            pl.BlockSpec((1, CONV_HALO, W_MIX), halo_map(bg)),
            _resident(cw.shape), _resident((1, W_MIX)), _resident((1, W_MIX)), _resident((1, W_MIX)),
            _resident((W_MIX, W_MIX)), _resident((1, W_MIX)),
        ],
        out_specs=pl.BlockSpec((1, ts, W_MIX), lambda bi, i: (bi, i, 0)),
        scratch_shapes=[pltpu.VMEM((ts + CONV_HALO, W_MIX), F32), pltpu.VMEM((ts, W_MIX), BF16)],
        compiler_params=_cparams(("parallel", "parallel")),
        name="conv_mixer",
    )(proj, proj, proj, proj, cw, row(conv_b), row(ln_g), row(ln_b), pw_w.astype(BF16), row(pw_b))


def _compress_kernel(r_ref, pos_ref, w1_ref, w2_ref, o_ref):
    r = r_ref[0]
    half = r.shape[1]
    n = r.shape[0]
    top = _dot(r, w1_ref[0:half, :])
    bot = _dot(r, w1_ref[half:2 * half, :])
    bot_next = pltpu.roll(bot, n - 1, 0)
    posb = _dot(jnp.broadcast_to(pos_ref[...], (8, 2 * half)), w1_ref[...])[0:1, :]
    h = jax.nn.gelu(top + bot_next + posb)
    out = _dot(h.astype(BF16), w2_ref[...])
    row = lax.broadcasted_iota(I32, out.shape, 0)
    o_ref[0] = jnp.where(row < n - 1, out, 0.0).astype(BF16)


def _compress(raw, pos, w1, w2):
    b, s, d = raw.shape
    n = s // CMP_STRIDE
    r = raw.reshape(b, n, CMP_STRIDE * d)
    return pl.pallas_call(
        _compress_kernel,
        out_shape=jax.ShapeDtypeStruct((b, n, d), BF16),
        grid=(b,),
        in_specs=[pl.BlockSpec((1, n, CMP_STRIDE * d), lambda bi: (bi, 0, 0)),
                  _resident((1, CMP_BLOCK * d)), _resident((CMP_BLOCK * d, d)), _resident((d, d))],
        out_specs=pl.BlockSpec((1, n, d), lambda bi: (bi, 0, 0)),
        compiler_params=_cparams(("parallel",)),
        name="nsa_compress",
    )(r, pos.reshape(1, CMP_BLOCK * d).astype(BF16), w1.astype(BF16), w2.astype(BF16))


ATT_TQ = 256
ATT_TK = 512
POS_RADIX = 64


def _pos_features(s_len):
    assert s_len <= POS_RADIX * 256
    s = np.arange(s_len)
    f = np.zeros((s_len, LANES), np.float32)
    f[:, 0] = f[:, 1] = s // POS_RADIX
    f[:, 2] = f[:, 3] = s % POS_RADIX
    return jnp.asarray(f, BF16)


def _slope_features(slopes, tq):
    f = np.zeros((len(slopes) * tq, LANES), np.float32)
    for h, sl in enumerate(slopes):
        c = np.float32(sl * LOG2E)
        ca = np.float32(np.asarray(c, dtype=BF16))
        cb = np.float32(np.asarray(c - ca, dtype=BF16))
        f[h * tq:(h + 1) * tq, 0:4] = [POS_RADIX * ca, POS_RADIX * cb, ca, cb]
    return jnp.asarray(f, BF16)


def _ones_feature(n):
    f = np.zeros((n, LANES), np.float32)
    f[:, 0] = 1.0
    return jnp.asarray(f, BF16)


def _stack_queries(q_ref, qf_ref, q4_ref):
    tq = q_ref.shape[1]
    for h in range(N_HEADS):
        q4_ref[h * tq:(h + 1) * tq, 0:HEAD_DIM] = q_ref[0, :, h * HEAD_DIM:(h + 1) * HEAD_DIM]
    q4_ref[:, HEAD_DIM:2 * HEAD_DIM] = qf_ref[...]


def _flash_step(q4_ref, kk, mask, ms, vv, acc_ref, tq):
    new_ms = []
    for h in range(N_HEADS):
        rows = slice(h * tq, (h + 1) * tq)
        ah = jnp.where(mask, _dot_nt(q4_ref[rows, :], kk), NEG)
        m_new = jnp.maximum(ms[h], jnp.max(ah, axis=-1, keepdims=True))
        alpha = jnp.exp2(ms[h] - m_new)
        p = jnp.exp2(ah - m_new).astype(BF16)
        acc_ref[rows, :] = alpha * acc_ref[rows, :] + _dot(p, vv)
        new_ms.append(m_new)
    return tuple(new_ms)


def _softmax2_rows(a, mask):
    a = jnp.where(mask, a, NEG)
    m = jnp.max(a, axis=-1, keepdims=True)
    e = jnp.where(mask, jnp.exp2(a - m), 0.0)
    s = jnp.sum(e, axis=-1, keepdims=True)
    return e / jnp.maximum(s, 1e-30)


def _split3(x):
    hi = x.astype(BF16)
    r1 = x - hi.astype(F32)
    mid = r1.astype(BF16)
    lo = (r1 - mid.astype(F32)).astype(BF16)
    return hi, mid, lo


def _nsa_kernel(q_ref, sm_ref, kc_ref, vc_ref, sk_ref, sv_ref, wk_ref, wv_ref, pf_ref, qf_ref, vf_ref,
                o_ref, q4_ref, p4_ref, acc_ref, selm_ref):
    tq = q_ref.shape[1]
    s_len = sk_ref.shape[1]
    n_cmp = kc_ref.shape[1]
    n_sel = s_len // SEL_BLOCK
    n_top = min(SEL_TOPK, n_sel)
    t0 = pl.program_id(1) * tq
    _stack_queries(q_ref, qf_ref, q4_ref)
    t_col = t0 + lax.broadcasted_iota(I32, (tq, 1), 0)

    a_all = _dot_nt(q4_ref[:, 0:HEAD_DIM], kc_ref[0])
    c_idx = lax.broadcasted_iota(I32, (1, n_cmp), 1)
    cd = t_col - (c_idx * CMP_STRIDE + (CMP_BLOCK - 1))
    cmask = (cd >= 0) & (c_idx < n_cmp - 1)
    cdf = cd.astype(F32)
    p_sum = jnp.zeros((tq, n_cmp), F32)
    for h in range(N_HEADS):
        p = _softmax2_rows(a_all[h * tq:(h + 1) * tq] - (SLOPES_NSA[h] * LOG2E) * cdf, cmask)
        p_sum = p_sum + p
        p4_ref[h * tq:(h + 1) * tq, 0:n_cmp] = p.astype(BF16)
    o_cmp = _dot(p4_ref[:, 0:n_cmp], vc_ref[0])

    n_selp = selm_ref.shape[0]
    jj = lax.broadcasted_iota(I32, (n_selp, n_cmp), 0)
    cc = lax.broadcasted_iota(I32, (n_selp, n_cmp), 1)
    c_start = cc * CMP_STRIDE
    overlap = ((c_start < (jj + 1) * SEL_BLOCK) & (c_start + (CMP_BLOCK - 1) >= jj * SEL_BLOCK)
               & (cc < n_cmp - 1))
    ov = jnp.where(overlap, 1.0, 0.0).astype(BF16)
    hi, mid, lo = _split3(p_sum)
    imp = _dot_nt(ov, hi) + _dot_nt(ov, mid) + _dot_nt(ov, lo)
    j_col = lax.broadcasted_iota(I32, (n_selp, 1), 0)
    t_blk = (t0 + lax.broadcasted_iota(I32, (1, tq), 1)) // SEL_BLOCK
    forced = (j_col == 0) | (j_col == t_blk) | (j_col == t_blk - 1)
    imp = jnp.where(forced, jnp.inf, imp)
    imp = jnp.where(j_col <= t_blk, imp, -jnp.inf)
    rank = jnp.zeros((n_selp, tq), F32)
    for i2 in range(n_sel):
        ci = imp[i2:i2 + 1, :]
        tie_first = jnp.where(j_col > i2, 1.0, 0.0)
        rank = rank + jnp.where(ci > imp, 1.0, jnp.where(ci == imp, tie_first, 0.0))
    selm_ref[...] = jnp.where((rank < n_top) & (j_col < n_sel), 1.0, 0.0)
    selm = selm_ref[...].T.astype(BF16)

    tk = min(ATT_TK, s_len)
    n_kt = (t0 + tq - 1) // tk + 1
    e_row = lax.broadcasted_iota(I32, (n_selp, tk), 0)
    e_col = lax.broadcasted_iota(I32, (n_selp, tk), 1)
    acc_ref[...] = jnp.zeros(acc_ref.shape, F32)

    def sel_body(kt, ms):
        s0 = pl.multiple_of(kt * tk, tk)
        kk = jnp.concatenate([sk_ref[0, pl.ds(s0, tk), :], pf_ref[pl.ds(s0, tk), :]], axis=1)
        vv = jnp.concatenate([sv_ref[0, pl.ds(s0, tk), :], vf_ref[0:tk, :]], axis=1)
        s_pos = s0 + lax.broadcasted_iota(I32, (1, tk), 1)
        expand = jnp.where(e_row == (s0 + e_col) // SEL_BLOCK, 1.0, 0.0).astype(BF16)
        mask = (s_pos <= t_col) & (_dot(selm, expand) > 0.5)
        return _flash_step(q4_ref, kk, mask, ms, vv, acc_ref, tq)

    m0 = tuple(jnp.full((tq, 1), NEG, F32) for _ in range(N_HEADS))
    lax.fori_loop(0, n_kt, sel_body, m0)

    wlen = min(NSA_WINDOW + tq, s_len)
    ks = pl.multiple_of(jnp.maximum(t0 + tq - wlen, 0), LANES)
    kw = jnp.concatenate([wk_ref[0, pl.ds(ks, wlen), :], pf_ref[pl.ds(ks, wlen), :]], axis=1)
    vw = jnp.concatenate([wv_ref[0, pl.ds(ks, wlen), :], vf_ref[0:wlen, :]], axis=1)
    wd = t_col - (ks + lax.broadcasted_iota(I32, (1, wlen), 1))
    wmask = (wd >= 0) & (wd < NSA_WINDOW)
    o_win = []
    for h in range(N_HEADS):
        ah = jnp.where(wmask, _dot_nt(q4_ref[h * tq:(h + 1) * tq, :], kw), NEG)
        m = jnp.max(ah, axis=-1, keepdims=True)
        o_win.append(_dot(jnp.exp2(ah - m).astype(BF16), vw))

    gates = jax.nn.sigmoid(sm_ref[0].astype(F32))
    for h in range(N_HEADS):
        rows = slice(h * tq, (h + 1) * tq)
        o_slc = acc_ref[rows, 0:HEAD_DIM] / jnp.maximum(acc_ref[rows, HEAD_DIM:HEAD_DIM + 1], 1e-30)
        o_w = o_win[h][:, 0:HEAD_DIM] / jnp.maximum(o_win[h][:, HEAD_DIM:HEAD_DIM + 1], 1e-30)
        g0 = gates[:, NG_LANE + 3 * h:NG_LANE + 3 * h + 1]
        g1 = gates[:, NG_LANE + 3 * h + 1:NG_LANE + 3 * h + 2]
        g2 = gates[:, NG_LANE + 3 * h + 2:NG_LANE + 3 * h + 3]
        o = g0 * o_cmp[rows] + g1 * o_slc + g2 * o_w
        o_ref[0, :, h * HEAD_DIM:(h + 1) * HEAD_DIM] = o.astype(BF16)


def _nsa_mixer(proj, k_cmp, v_cmp):
    b, s, _ = proj.shape
    tq = min(ATT_TQ, s)
    tk = min(ATT_TK, s)
    n_cmp = k_cmp.shape[1]
    wlen = min(NSA_WINDOW + tq, s)
    col = lambda name: _PACK_OFF[name] // HEAD_DIM
    full = lambda name: pl.BlockSpec((1, s, HEAD_DIM), functools.partial(lambda c, bi, i: (bi, 0, c), col(name)))
    nv = max(tk, wlen)
    return pl.pallas_call(
        _nsa_kernel,
        out_shape=jax.ShapeDtypeStruct((b, s, W_MIX), BF16),
        grid=(b, s // tq),
        in_specs=[
            pl.BlockSpec((1, tq, W_MIX), lambda bi, i: (bi, i, _PACK_OFF['n_q'] // W_MIX)),
            pl.BlockSpec((1, tq, LANES), lambda bi, i: (bi, i, SMALL_OFF // LANES)),
            pl.BlockSpec((1, n_cmp, HEAD_DIM), lambda bi, i: (bi, 0, 0)),
            pl.BlockSpec((1, n_cmp, HEAD_DIM), lambda bi, i: (bi, 0, 0)),
            full('n_sk'), full('n_sv'), full('n_wk'), full('n_wv'),
            _resident((s, LANES)), _resident((N_HEADS * tq, LANES)), _resident((nv, LANES)),
        ],
        out_specs=pl.BlockSpec((1, tq, W_MIX), lambda bi, i: (bi, i, 0)),
        scratch_shapes=[pltpu.VMEM((N_HEADS * tq, 2 * HEAD_DIM), BF16),
                        pltpu.VMEM((N_HEADS * tq, n_cmp), BF16),
                        pltpu.VMEM((N_HEADS * tq, 2 * HEAD_DIM), F32),
                        pltpu.VMEM((((s // SEL_BLOCK + LANES - 1) // LANES) * LANES, tq), F32)],
        compiler_params=_cparams(("parallel", "arbitrary")),
        name="nsa_attention",
    )(proj, proj, k_cmp, v_cmp, proj, proj, proj, proj,
      _pos_features(s), _slope_features(SLOPES_NSA, tq), _ones_feature(nv))


I16 = jnp.int16
I16_MIN = -2 ** 15
SEL_ROWS = 64


def _dsa_kernel(q_ref, iq_ref, sm_ref, smf_ref, k_ref, v_ref, pf_ref, qf_ref, vf_ref, o_ref,
                q4_ref, acc_ref, hi_ref, lo_ref, lq_ref, selb_ref):
    tq = q_ref.shape[1]
    s_len = k_ref.shape[1]
    topk = min(DSA_TOPK_MAX, s_len // 4)
    tk = min(ATT_TK, s_len)
    n_ch = tk // SEL_ROWS
    t0 = pl.program_id(1) * tq
    n_kt = (t0 + tq - 1) // tk + 1
    _stack_queries(q_ref, qf_ref, q4_ref)
    t_row = t0 + lax.broadcasted_iota(I32, (1, tq), 1)
    iq = iq_ref[0]
    iw_t = sm_ref[0].astype(F32).T
    one = jnp.ones((), BF16)
    zero = jnp.zeros((), BF16)

    def score_body(kt, _):
        s0 = pl.multiple_of(kt * tk, tk)
        ik = smf_ref[0, pl.ds(s0, tk), IK_LANE:IK_LANE + IDX_DIM]
        sc = jnp.zeros((tk, tq), F32)
        for h in range(IDX_HEADS):
            lg = _dot_nt(ik, iq[:, h * IDX_DIM:(h + 1) * IDX_DIM])
            sc = sc + jnp.maximum(lg, 0.0) * iw_t[IW_LANE + h:IW_LANE + h + 1, :]
        sc = jnp.where(sc == 0.0, 0.0, sc)
        bits = lax.bitcast_convert_type(sc, I32)
        key = bits ^ ((bits >> 31) & 0x7FFFFFFF)
        s_pos = s0 + lax.broadcasted_iota(I32, (tk, 1), 0)
        key = jnp.where(s_pos <= t_row, key, INT_MIN)
        hi_ref[pl.ds(s0, tk), :] = (key >> 16).astype(I16)
        lo_ref[pl.ds(s0, tk), :] = ((key & 0xFFFF) + I16_MIN).astype(I16)
        return 0

    lax.fori_loop(0, n_kt, score_body, 0)

    def count_ge(ref, thr_row):
        thr = jnp.broadcast_to(thr_row, (SEL_ROWS, tq))

        def body(kt, cnt):
            s0 = pl.multiple_of(kt * tk, tk)
            for c in range(n_ch):
                cnt = cnt + jnp.where(ref[pl.ds(s0 + c * SEL_ROWS, SEL_ROWS), :] >= thr, one, zero)
            return cnt
        cnt = lax.fori_loop(0, n_kt, body, jnp.zeros((SEL_ROWS, tq), BF16))
        return jnp.sum(cnt.astype(F32), axis=0, keepdims=True)

    def kth_largest(ref, k):
        def bit_body(it, cand):
            trial = cand | jnp.left_shift(jnp.int32(1), 15 - it)
            total = count_ge(ref, (trial + I16_MIN).astype(I16))
            return jnp.where(total >= k, trial, cand)
        return lax.fori_loop(0, 16, bit_body, jnp.zeros((1, tq), I32))

    cand_hi = jnp.maximum(kth_largest(hi_ref, float(topk)), 1)
    p16 = jnp.broadcast_to((cand_hi + I16_MIN).astype(I16), (SEL_ROWS, tq))
    n_above = count_ge(hi_ref, (jnp.minimum(cand_hi + 1, 2 ** 16 - 1) + I16_MIN).astype(I16))
    need = float(topk) - n_above

    def tie_body(kt, _):
        s0 = pl.multiple_of(kt * tk, tk)
        for c in range(n_ch):
            ds = pl.ds(s0 + c * SEL_ROWS, SEL_ROWS)
            lq_ref[ds, :] = jnp.where(hi_ref[ds, :] == p16, lo_ref[ds, :], jnp.full((), I16_MIN, I16))
        return 0

    lax.fori_loop(0, n_kt, tie_body, 0)
    q16 = jnp.broadcast_to((kth_largest(lq_ref, need) + I16_MIN).astype(I16), (SEL_ROWS, tq))

    def sel_body(kt, _):
        s0 = pl.multiple_of(kt * tk, tk)
        for c in range(n_ch):
            ds = pl.ds(s0 + c * SEL_ROWS, SEL_ROWS)
            hi = hi_ref[ds, :]
            tie = jnp.where(hi == p16, jnp.where(lo_ref[ds, :] >= q16, one, zero), zero)
            selb_ref[ds, :] = jnp.where(hi > p16, one, tie)
        return 0

    lax.fori_loop(0, n_kt, sel_body, 0)

    acc_ref[...] = jnp.zeros(acc_ref.shape, F32)

    def att_body(kt, ms):
        s0 = pl.multiple_of(kt * tk, tk)
        kk = jnp.concatenate([k_ref[0, pl.ds(s0, tk), :], pf_ref[pl.ds(s0, tk), :]], axis=1)
        vv = jnp.concatenate([v_ref[0, pl.ds(s0, tk), :], vf_ref[...]], axis=1)
        mask = selb_ref[pl.ds(s0, tk), :].astype(F32).T > 0.5
        return _flash_step(q4_ref, kk, mask, ms, vv, acc_ref, tq)

    m0 = tuple(jnp.full((tq, 1), NEG, F32) for _ in range(N_HEADS))
    lax.fori_loop(0, n_kt, att_body, m0)
    for h in range(N_HEADS):
        rows = slice(h * tq, (h + 1) * tq)
        o = acc_ref[rows, 0:HEAD_DIM] / jnp.maximum(acc_ref[rows, HEAD_DIM:HEAD_DIM + 1], 1e-30)
        o_ref[0, :, h * HEAD_DIM:(h + 1) * HEAD_DIM] = o.astype(BF16)


def _dsa_mixer(proj):
    b, s, _ = proj.shape
    tq = min(ATT_TQ, s)
    tk = min(ATT_TK, s)
    full = lambda off: pl.BlockSpec((1, s, LANES), functools.partial(lambda c, bi, i: (bi, 0, c), off // LANES))
    return pl.pallas_call(
        _dsa_kernel,
        out_shape=jax.ShapeDtypeStruct((b, s, W_MIX), BF16),
        grid=(b, s // tq),
        in_specs=[
            pl.BlockSpec((1, tq, W_MIX), lambda bi, i: (bi, i, _PACK_OFF['d_q'] // W_MIX)),
            pl.BlockSpec((1, tq, W_MIX), lambda bi, i: (bi, i, _PACK_OFF['i_q'] // W_MIX)),
            pl.BlockSpec((1, tq, LANES), lambda bi, i: (bi, i, SMALL_OFF // LANES)),
            full(SMALL_OFF), full(_PACK_OFF['d_k']), full(_PACK_OFF['d_v']),
            _resident((s, LANES)), _resident((N_HEADS * tq, LANES)), _resident((tk, LANES)),
        ],
        out_specs=pl.BlockSpec((1, tq, W_MIX), lambda bi, i: (bi, i, 0)),
        scratch_shapes=[pltpu.VMEM((N_HEADS * tq, 2 * HEAD_DIM), BF16),
                        pltpu.VMEM((N_HEADS * tq, 2 * HEAD_DIM), F32),
                        pltpu.VMEM((s, tq), I16), pltpu.VMEM((s, tq), I16), pltpu.VMEM((s, tq), I16),
                        pltpu.VMEM((s, tq), BF16)],
        compiler_params=_cparams(("parallel", "arbitrary")),
        name="dsa_attention",
    )(proj, proj, proj, proj, proj, proj,
      _pos_features(s), _slope_features(SLOPES_DSA, tq), _ones_feature(tk))


OUTPROJ_TM = 256


def _outproj_kernel(y0_ref, y1_ref, y2_ref, y3_ref, w_ref, x_ref, g_ref, b_ref, o_ref, ob_ref):
    acc = DN_ALPHA * x_ref[...]
    for gi, y_ref in enumerate((y0_ref, y1_ref, y2_ref, y3_ref)):
        acc = acc + _dot(y_ref[...], w_ref[gi * W_MIX:(gi + 1) * W_MIX, :])
    y = _layernorm(acc, g_ref[...], b_ref[...])
    o_ref[...] = y
    ob_ref[...] = y.astype(BF16)


def _outproj_ln(ys, w_out, x, g, b):
    n_tok, d = x.shape
    tm = OUTPROJ_TM
    yspec = pl.BlockSpec((tm, W_MIX), lambda i: (i, 0))
    xspec = pl.BlockSpec((tm, d), lambda i: (i, 0))
    return pl.pallas_call(
        _outproj_kernel,
        out_shape=(jax.ShapeDtypeStruct((n_tok, d), F32), jax.ShapeDtypeStruct((n_tok, d), BF16)),
        grid=(n_tok // tm,),
        in_specs=[yspec, yspec, yspec, yspec, _resident((d, d)), xspec, _resident((1, d)), _resident((1, d))],
        out_specs=(xspec, xspec),
        compiler_params=_cparams(("parallel",)),
        name="outproj_ln",
    )(*ys, w_out, x, g.reshape(1, d).astype(F32), b.reshape(1, d).astype(F32))


FFN_TM = 512
FFN_TF = 512


def _ffn_kernel(xb_ref, wg_ref, wu_ref, wd_ref, x_ref, g_ref, b_ref, o_ref, ob_ref, acc_ref):
    f = pl.program_id(1)

    @pl.when(f == 0)
    def _():
        acc_ref[...] = DN_ALPHA * x_ref[...]

    xb = xb_ref[...]
    gate = _dot(xb, wg_ref[...])
    up = _dot(xb, wu_ref[...])
    h = (gate * jax.nn.sigmoid(gate) * up).astype(BF16)
    acc_ref[...] += _dot(h, wd_ref[...])

    @pl.when(f == pl.num_programs(1) - 1)
    def _():
        y = _layernorm(acc_ref[...], g_ref[...], b_ref[...])
        o_ref[...] = y
        ob_ref[...] = y.astype(BF16)


def _ffn_ln(xb, w_gate_up, w_down, x, g, b):
    n_tok, d = x.shape
    d_ff = w_down.shape[0]
    tm, tf = FFN_TM, FFN_TF
    nf = d_ff // tf
    xspec = pl.BlockSpec((tm, d), lambda i, f: (i, 0))
    return pl.pallas_call(
        _ffn_kernel,
        out_shape=(jax.ShapeDtypeStruct((n_tok, d), F32), jax.ShapeDtypeStruct((n_tok, d), BF16)),
        grid=(n_tok // tm, nf),
        in_specs=[
            xspec,
            pl.BlockSpec((d, tf), lambda i, f: (0, f)),
            pl.BlockSpec((d, tf), lambda i, f: (0, f + nf)),
            pl.BlockSpec((tf, d), lambda i, f: (f, 0)),
            xspec, _resident((1, d)), _resident((1, d)),
        ],
        out_specs=(xspec, xspec),
        scratch_shapes=[pltpu.VMEM((tm, d), F32)],
        compiler_params=_cparams(("parallel", "arbitrary")),
        name="ffn_ln",
    )(xb, w_gate_up, w_gate_up, w_down, x, g.reshape(1, d).astype(F32), b.reshape(1, d).astype(F32))


def _pack_w_in(w):
    fold = {'d_q': ATTN_SCALE * LOG2E, 'n_q': ATTN_SCALE * LOG2E, 'i_w': IDX_SCALE}
    cols = [w[:, _SRC_OFF[n]:_SRC_OFF[n] + _SRC_SIZE[n]] * fold.get(n, 1.0) for n in _PACK_ORDER]
    used = sum(_SRC_SIZE[n] for n in _PACK_ORDER)
    cols.append(jnp.zeros((w.shape[0], D_PACK - used), w.dtype))
    return jnp.concatenate(cols, axis=1).astype(BF16)


def _layer(x, xb, p):
    b, s, d = x.shape
    n_tok = b * s
    proj = _inproj(xb.reshape(n_tok, d), _pack_w_in(p['w_in'])).reshape(b, s, D_PACK)
    y_pool = _pool_mixer(proj, p['pool_w'], p['pool_scale'])
    y_conv = _conv_mixer(proj, p['conv_w'], p['conv_b'], p['conv_ln_g'], p['conv_ln_b'],
                         p['conv_pw_w'], p['conv_pw_b'])
    y_dsa = _dsa_mixer(proj)
    ck = proj[:, :, _PACK_OFF['n_ck']:_PACK_OFF['n_ck'] + HEAD_DIM]
    cv = proj[:, :, _PACK_OFF['n_cv']:_PACK_OFF['n_cv'] + HEAD_DIM]
    k_cmp = _compress(ck, p['cmp_pos_k'], p['cmp_k_w1'], p['cmp_k_w2'])
    v_cmp = _compress(cv, p['cmp_pos_v'], p['cmp_v_w1'], p['cmp_v_w2'])
    y_nsa = _nsa_mixer(proj, k_cmp, v_cmp)
    ys = [y.reshape(n_tok, W_MIX) for y in (y_pool, y_conv, y_dsa, y_nsa)]
    x1, x1b = _outproj_ln(ys, p['w_out'].astype(BF16), x.reshape(n_tok, d), p['ln1_g'], p['ln1_b'])
    x2, x2b = _ffn_ln(x1b, p['w_gate_up'].astype(BF16), p['w_down'].astype(BF16), x1, p['ln2_g'], p['ln2_b'])
    return x2.reshape(b, s, d), x2b.reshape(b, s, d)


def kernel(x, w_in, w_out, pool_w, pool_scale, conv_w, conv_b, conv_ln_g, conv_ln_b, conv_pw_w, conv_pw_b,
           cmp_pos_k, cmp_pos_v, cmp_k_w1, cmp_k_w2, cmp_v_w1, cmp_v_w2, ln1_g, ln1_b, ln2_g, ln2_b,
           w_gate_up, w_down):
    params = dict(w_in=w_in, w_out=w_out, pool_w=pool_w, pool_scale=pool_scale, conv_w=conv_w, conv_b=conv_b,
                  conv_ln_g=conv_ln_g, conv_ln_b=conv_ln_b, conv_pw_w=conv_pw_w, conv_pw_b=conv_pw_b,
                  cmp_pos_k=cmp_pos_k, cmp_pos_v=cmp_pos_v, cmp_k_w1=cmp_k_w1, cmp_k_w2=cmp_k_w2,
                  cmp_v_w1=cmp_v_w1, cmp_v_w2=cmp_v_w2, ln1_g=ln1_g, ln1_b=ln1_b, ln2_g=ln2_g, ln2_b=ln2_b,
                  w_gate_up=w_gate_up, w_down=w_down)
    xb = x.astype(BF16)
    for l in range(w_in.shape[0]):
        x, xb = _layer(x, xb, {k: v[l] for k, v in params.items()})
    return x
```

```python
import functools

import numpy as np
import jax
import jax.numpy as jnp
from jax import lax
from jax.experimental import pallas as pl
from jax.experimental.pallas import tpu as pltpu

F32 = jnp.float32
BF16 = jnp.bfloat16
I32 = jnp.int32

D_MODEL = 2048
DEPTH = 2
W_MIX = D_MODEL // 4
HEAD_DIM = 128
N_HEADS = W_MIX // HEAD_DIM
POOL_WINDOWS = (2, 4, 8, 16)
POOL_GROUP = W_MIX // len(POOL_WINDOWS)
CONV_WIDTH = 31
IDX_HEADS = 8
IDX_DIM = 64
DSA_TOPK_MAX = 256
CMP_BLOCK = 32
CMP_STRIDE = 16
SEL_BLOCK = 64
SEL_TOPK = 16
NSA_WINDOW = 512
D_FF = ((8 * D_MODEL + 3 * 256 - 1) // (3 * 256)) * 256
DN_ALPHA = (2 * DEPTH) ** 0.25
ATTN_SCALE = HEAD_DIM ** -0.5
IDX_SCALE = (IDX_HEADS * IDX_DIM) ** -0.5
LN_EPS = 1e-5
LOG2E = 1.4426950408889634

LANES = 128
SUBLANES = 8
VMEM_LIMIT = 48 * 1024 * 1024

NEG = -1e30
INT_MIN = -2 ** 31

_SRC_SIZES = (W_MIX, W_MIX, W_MIX, W_MIX, HEAD_DIM, HEAD_DIM, IDX_HEADS * IDX_DIM, IDX_DIM, IDX_HEADS,
              W_MIX, HEAD_DIM, HEAD_DIM, HEAD_DIM, HEAD_DIM, HEAD_DIM, HEAD_DIM, N_HEADS * 3)
_SRC_NAMES = ('pool', 'c_a', 'c_g', 'd_q', 'd_k', 'd_v', 'i_q', 'i_k', 'i_w',
              'n_q', 'n_ck', 'n_cv', 'n_sk', 'n_sv', 'n_wk', 'n_wv', 'n_g')
_SRC_OFF = dict(zip(_SRC_NAMES, np.concatenate([[0], np.cumsum(_SRC_SIZES)[:-1]]).tolist()))
_SRC_SIZE = dict(zip(_SRC_NAMES, _SRC_SIZES))
_PACK_ORDER = ('pool', 'c_a', 'c_g', 'd_q', 'i_q', 'n_q', 'd_k', 'd_v',
               'n_ck', 'n_cv', 'n_sk', 'n_sv', 'n_wk', 'n_wv', 'i_k', 'i_w', 'n_g')
_PACK_OFF = {}
_o = 0
for _n in _PACK_ORDER:
    _PACK_OFF[_n] = _o
    _o += _SRC_SIZE[_n]
D_PACK = ((_o + LANES - 1) // LANES) * LANES
SMALL_OFF = _PACK_OFF['i_k']
IK_LANE = 0
IW_LANE = _PACK_OFF['i_w'] - SMALL_OFF
NG_LANE = _PACK_OFF['n_g'] - SMALL_OFF


def _alibi_slopes():
    n = 2 * N_HEADS
    s = np.power(2.0, -8.0 * np.arange(1, n + 1) / n).astype(np.float32)
    return [float(v) for v in s[0::2]], [float(v) for v in s[1::2]]


SLOPES_DSA, SLOPES_NSA = _alibi_slopes()


def _cparams(sem):
    return pltpu.CompilerParams(dimension_semantics=sem, vmem_limit_bytes=VMEM_LIMIT)


def _resident(shape):
    nd = len(shape)
    return pl.BlockSpec(shape, lambda *_: (0,) * nd, pipeline_mode=pl.Buffered(1))


def _dot(a, b):
    return jnp.dot(a, b, preferred_element_type=F32)


def _dot_nt(a, b):
    return lax.dot_general(a, b, (((1,), (1,)), ((), ())), preferred_element_type=F32)


def _layernorm(x, g, b):
    mu = jnp.mean(x, axis=-1, keepdims=True)
    xc = x - mu
    var = jnp.mean(xc * xc, axis=-1, keepdims=True)
    return xc * lax.rsqrt(var + LN_EPS) * g + b


INPROJ_TM = 512
INPROJ_CHUNK = 512


def _inproj_kernel(x_ref, w_ref, o_ref):
    x = x_ref[...].astype(BF16)
    n = o_ref.shape[1]
    for c0 in range(0, n, INPROJ_CHUNK):
        c1 = min(c0 + INPROJ_CHUNK, n)
        o_ref[:, c0:c1] = _dot(x, w_ref[:, c0:c1]).astype(BF16)


def _inproj(xb, w_pack):
    n_tok, d = xb.shape
    tm = INPROJ_TM
    return pl.pallas_call(
        _inproj_kernel,
        out_shape=jax.ShapeDtypeStruct((n_tok, D_PACK), BF16),
        grid=(n_tok // tm,),
        in_specs=[pl.BlockSpec((tm, d), lambda i: (i, 0)), _resident((d, D_PACK))],
        out_specs=pl.BlockSpec((tm, D_PACK), lambda i: (i, 0)),
        compiler_params=_cparams(("parallel",)),
        name="inproj",
    )(xb, w_pack)


POOL_TS = 512
POOL_HALO = 16


def _pool_kernel(u_ref, halo_ref, w_ref, sc_ref, o_ref, xs_ref):
    i = pl.program_id(1)
    ts = u_ref.shape[1]
    xs_ref[POOL_HALO:POOL_HALO + ts, :] = u_ref[0].astype(F32)
    xs_ref[0:POOL_HALO, :] = jnp.where(i > 0, halo_ref[0].astype(F32), 0.0)
    pos = i * ts + lax.broadcasted_iota(I32, (ts, 1), 0)
    for g, win in enumerate(POOL_WINDOWS):
        c = slice(g * POOL_GROUP, (g + 1) * POOL_GROUP)
        x = xs_ref[POOL_HALO:POOL_HALO + ts, c]
        acc = x
        for k in range(1, win):
            acc = acc + xs_ref[POOL_HALO - k:POOL_HALO - k + ts, c]
        cnt = jnp.minimum(pos + 1, win).astype(F32)
        d = acc / cnt - x
        y = _dot(d.astype(BF16), w_ref[g])
        o_ref[0, :, c] = (y * sc_ref[:, c]).astype(BF16)


def _pool_mixer(proj, pool_w, pool_scale):
    b, s, _ = proj.shape
    ts = min(POOL_TS, s)
    hb = ts // POOL_HALO
    blk = _PACK_OFF['pool'] // W_MIX
    return pl.pallas_call(
        _pool_kernel,
        out_shape=jax.ShapeDtypeStruct((b, s, W_MIX), BF16),
        grid=(b, s // ts),
        in_specs=[
            pl.BlockSpec((1, ts, W_MIX), lambda bi, i: (bi, i, blk)),
            pl.BlockSpec((1, POOL_HALO, W_MIX), lambda bi, i: (bi, jnp.maximum(i * hb - 1, 0), blk)),
            _resident(pool_w.shape),
            _resident((1, W_MIX)),
        ],
        out_specs=pl.BlockSpec((1, ts, W_MIX), lambda bi, i: (bi, i, 0)),
        scratch_shapes=[pltpu.VMEM((ts + POOL_HALO, W_MIX), F32)],
        compiler_params=_cparams(("parallel", "parallel")),
        name="pool_mixer",
    )(proj, proj, pool_w.astype(BF16), pool_scale.reshape(1, W_MIX).astype(F32))


CONV_TS = 512
CONV_HALO = 32
CONV_ROWS = 32


def _conv_kernel(a_ref, g_ref, ha_ref, hg_ref, cw_ref, cb_ref, lg_ref, lb_ref, pw_ref, pb_ref,
                 o_ref, hs_ref, sh_ref, y_ref):
    i = pl.program_id(1)
    ts = a_ref.shape[1]
    hs_ref[CONV_HALO:CONV_HALO + ts, :] = a_ref[0].astype(F32) * jax.nn.sigmoid(g_ref[0].astype(F32))
    halo = ha_ref[0].astype(F32) * jax.nn.sigmoid(hg_ref[0].astype(F32))
    hs_ref[0:CONV_HALO, :] = jnp.where(i > 0, halo, 0.0)
    n_sh = ts + CONV_HALO - SUBLANES
    for b in range(1, SUBLANES):
        sh_ref[b - 1, 0:n_sh, :] = hs_ref[b:b + n_sh, :]
    base = CONV_HALO - (CONV_WIDTH - 1)
    for r0 in range(0, ts, CONV_ROWS):
        acc = jnp.broadcast_to(cb_ref[...], (CONV_ROWS, W_MIX))
        for j in range(CONV_WIDTH):
            a8, b = divmod(base + j, SUBLANES)
            r = r0 + a8 * SUBLANES
            src = hs_ref[r:r + CONV_ROWS, :] if b == 0 else sh_ref[b - 1, r:r + CONV_ROWS, :]
            acc = acc + src * cw_ref[j:j + 1, :]
        y = _layernorm(acc, lg_ref[...], lb_ref[...])
        y_ref[r0:r0 + CONV_ROWS, :] = (y * jax.nn.sigmoid(y)).astype(BF16)
    o_ref[0] = (_dot(y_ref[...], pw_ref[...]) + pb_ref[...]).astype(BF16)


def _conv_mixer(proj, conv_w, conv_b, ln_g, ln_b, pw_w, pw_b):
    b, s, _ = proj.shape
    ts = min(CONV_TS, s)
    hb = ts // CONV_HALO
    ba = _PACK_OFF['c_a'] // W_MIX
    bg = _PACK_OFF['c_g'] // W_MIX
    row = lambda v: v.reshape(1, W_MIX).astype(F32)
    cw = jnp.concatenate([conv_w.astype(F32), jnp.zeros((1, W_MIX), F32)], axis=0)
    halo_map = lambda blk: (lambda bi, i: (bi, jnp.maximum(i * hb - 1, 0), blk))
    return pl.pallas_call(
        _conv_kernel,
        out_shape=jax.ShapeDtypeStruct((b, s, W_MIX), BF16),
        grid=(b, s // ts),
        in_specs=[
            pl.BlockSpec((1, ts, W_MIX), lambda bi, i: (bi, i, ba)),
            pl.BlockSpec((1, ts, W_MIX), lambda bi, i: (bi, i, bg)),
            pl.BlockSpec((1, CONV_HALO, W_MIX), halo_map(ba)),
            pl.BlockSpec((1, CONV_HALO, W_MIX), halo_map(bg)),
            _resident(cw.shape), _resident((1, W_MIX)), _resident((1, W_MIX)), _resident((1, W_MIX)),
            _resident((W_MIX, W_MIX)), _resident((1, W_MIX)),
        ],
        out_specs=pl.BlockSpec((1, ts, W_MIX), lambda bi, i: (bi, i, 0)),
        scratch_shapes=[pltpu.VMEM((ts + CONV_HALO, W_MIX), F32),
                        pltpu.VMEM((SUBLANES - 1, ts + CONV_HALO - SUBLANES, W_MIX), F32),
                        pltpu.VMEM((ts, W_MIX), BF16)],
        compiler_params=_cparams(("parallel", "parallel")),
        name="conv_mixer",
    )(proj, proj, proj, proj, cw, row(conv_b), row(ln_g), row(ln_b), pw_w.astype(BF16), row(pw_b))


def _compress_kernel(r_ref, pos_ref, w1_ref, w2_ref, o_ref):
    r = r_ref[0]
    half = r.shape[1]
    n = r.shape[0]
    top = _dot(r, w1_ref[0:half, :])
    bot = _dot(r, w1_ref[half:2 * half, :])
    bot_next = pltpu.roll(bot, n - 1, 0)
    posb = _dot(jnp.broadcast_to(pos_ref[...], (8, 2 * half)), w1_ref[...])[0:1, :]
    h = jax.nn.gelu(top + bot_next + posb)
    out = _dot(h.astype(BF16), w2_ref[...])
    row = lax.broadcasted_iota(I32, out.shape, 0)
    o_ref[0] = jnp.where(row < n - 1, out, 0.0).astype(BF16)


def _compress(raw, pos, w1, w2):
    b, s, d = raw.shape
    n = s // CMP_STRIDE
    r = raw.reshape(b, n, CMP_STRIDE * d)
    return pl.pallas_call(
        _compress_kernel,
        out_shape=jax.ShapeDtypeStruct((b, n, d), BF16),
        grid=(b,),
        in_specs=[pl.BlockSpec((1, n, CMP_STRIDE * d), lambda bi: (bi, 0, 0)),
                  _resident((1, CMP_BLOCK * d)), _resident((CMP_BLOCK * d, d)), _resident((d, d))],
        out_specs=pl.BlockSpec((1, n, d), lambda bi: (bi, 0, 0)),
        compiler_params=_cparams(("parallel",)),
        name="nsa_compress",
    )(r, pos.reshape(1, CMP_BLOCK * d).astype(BF16), w1.astype(BF16), w2.astype(BF16))


ATT_TQ = 256
ATT_TK = 512
POS_RADIX = 64


def _pos_features(s_len):
    assert s_len <= POS_RADIX * 256
    s = np.arange(s_len)
    f = np.zeros((s_len, LANES), np.float32)
    f[:, 0] = f[:, 1] = s // POS_RADIX
    f[:, 2] = f[:, 3] = s % POS_RADIX
    return jnp.asarray(f, BF16)


def _slope_features(slopes, tq):
    f = np.zeros((len(slopes) * tq, LANES), np.float32)
    for h, sl in enumerate(slopes):
        c = np.float32(sl * LOG2E)
        ca = np.float32(np.asarray(c, dtype=BF16))
        cb = np.float32(np.asarray(c - ca, dtype=BF16))
        f[h * tq:(h + 1) * tq, 0:4] = [POS_RADIX * ca, POS_RADIX * cb, ca, cb]
    return jnp.asarray(f, BF16)


def _ones_feature(n):
    f = np.zeros((n, LANES), np.float32)
    f[:, 0] = 1.0
    return jnp.asarray(f, BF16)


def _stack_queries(q_ref, qf_ref, q4_ref):
    tq = q_ref.shape[1]
    for h in range(N_HEADS):
        q4_ref[h * tq:(h + 1) * tq, 0:HEAD_DIM] = q_ref[0, :, h * HEAD_DIM:(h + 1) * HEAD_DIM]
    q4_ref[:, HEAD_DIM:2 * HEAD_DIM] = qf_ref[...]


def _flash_attention(q4_ref, acc_ref, p_ref, a_ref, n_kt, keys_fn, vals_fn, tq):
    acc_ref[...] = jnp.zeros(acc_ref.shape, F32)
    p_ref[...] = jnp.zeros(p_ref.shape, BF16)
    heads = [slice(h * tq, (h + 1) * tq) for h in range(N_HEADS)]

    def logits(rows, kk, mask):
        return jnp.where(mask, _dot_nt(q4_ref[rows, :], kk), NEG)

    def step(kt, ms, with_next):
        vv = vals_fn(jnp.maximum(kt - 1, 0))
        nxt = keys_fn(kt + 1) if with_next else None
        new_ms = []
        for h, rows in enumerate(heads):
            ah = a_ref[rows, :]
            pv = _dot(p_ref[rows, :], vv)
            if with_next:
                a_ref[rows, :] = logits(rows, *nxt)
            m_new = jnp.maximum(ms[h], jnp.max(ah, axis=-1, keepdims=True))
            p_ref[rows, :] = jnp.exp2(ah - m_new).astype(BF16)
            acc_ref[rows, :] = jnp.exp2(ms[h] - m_new) * (acc_ref[rows, :] + pv)
            new_ms.append(m_new)
        return tuple(new_ms)

    first = keys_fn(0)
    for rows in heads:
        a_ref[rows, :] = logits(rows, *first)
    m0 = tuple(jnp.full((tq, 1), NEG, F32) for _ in range(N_HEADS))
    ms = lax.fori_loop(0, n_kt - 1, lambda kt, ms: step(kt, ms, True), m0)
    step(n_kt - 1, ms, False)
    vv = vals_fn(n_kt - 1)
    for rows in heads:
        acc_ref[rows, :] += _dot(p_ref[rows, :], vv)


def _softmax2_rows(a, mask):
    a = jnp.where(mask, a, NEG)
    m = jnp.max(a, axis=-1, keepdims=True)
    e = jnp.where(mask, jnp.exp2(a - m), 0.0)
    s = jnp.sum(e, axis=-1, keepdims=True)
    return e / jnp.maximum(s, 1e-30)


def _split3(x):
    hi = x.astype(BF16)
    r1 = x - hi.astype(F32)
    mid = r1.astype(BF16)
    lo = (r1 - mid.astype(F32)).astype(BF16)
    return hi, mid, lo


def _nsa_kernel(q_ref, sm_ref, kc_ref, vc_ref, sk_ref, sv_ref, wk_ref, wv_ref, pf_ref, qf_ref, vf_ref,
                o_ref, q4_ref, p4_ref, acc_ref, selm_ref, p_ref, a_ref):
    tq = q_ref.shape[1]
    s_len = sk_ref.shape[1]
    n_cmp = kc_ref.shape[1]
    n_sel = s_len // SEL_BLOCK
    n_top = min(SEL_TOPK, n_sel)
    t0 = pl.program_id(1) * tq
    _stack_queries(q_ref, qf_ref, q4_ref)
    t_col = t0 + lax.broadcasted_iota(I32, (tq, 1), 0)

    a_all = _dot_nt(q4_ref[:, 0:HEAD_DIM], kc_ref[0])
    c_idx = lax.broadcasted_iota(I32, (1, n_cmp), 1)
    cd = t_col - (c_idx * CMP_STRIDE + (CMP_BLOCK - 1))
    cmask = (cd >= 0) & (c_idx < n_cmp - 1)
    cdf = cd.astype(F32)
    p_sum = jnp.zeros((tq, n_cmp), F32)
    for h in range(N_HEADS):
        p = _softmax2_rows(a_all[h * tq:(h + 1) * tq] - (SLOPES_NSA[h] * LOG2E) * cdf, cmask)
        p_sum = p_sum + p
        p4_ref[h * tq:(h + 1) * tq, 0:n_cmp] = p.astype(BF16)
    o_cmp = _dot(p4_ref[:, 0:n_cmp], vc_ref[0])

    n_selp = selm_ref.shape[0]
    jj = lax.broadcasted_iota(I32, (n_selp, n_cmp), 0)
    cc = lax.broadcasted_iota(I32, (n_selp, n_cmp), 1)
    c_start = cc * CMP_STRIDE
    overlap = ((c_start < (jj + 1) * SEL_BLOCK) & (c_start + (CMP_BLOCK - 1) >= jj * SEL_BLOCK)
               & (cc < n_cmp - 1))
    ov = jnp.where(overlap, 1.0, 0.0).astype(BF16)
    hi, mid, lo = _split3(p_sum)
    imp = _dot_nt(ov, hi) + _dot_nt(ov, mid) + _dot_nt(ov, lo)
    j_col = lax.broadcasted_iota(I32, (n_selp, 1), 0)
    t_blk = (t0 + lax.broadcasted_iota(I32, (1, tq), 1)) // SEL_BLOCK
    forced = (j_col == 0) | (j_col == t_blk) | (j_col == t_blk - 1)
    imp = jnp.where(forced, jnp.inf, imp)
    imp = jnp.where(j_col <= t_blk, imp, -jnp.inf)
    rank = jnp.zeros((n_selp, tq), F32)
    for i2 in range(n_sel):
        ci = imp[i2:i2 + 1, :]
        tie_first = jnp.where(j_col > i2, 1.0, 0.0)
        rank = rank + jnp.where(ci > imp, 1.0, jnp.where(ci == imp, tie_first, 0.0))
    selm_ref[...] = jnp.where((rank < n_top) & (j_col < n_sel), 1.0, 0.0)
    selm = selm_ref[...].T.astype(BF16)

    tk = min(ATT_TK, s_len)
    n_kt = (t0 + tq - 1) // tk + 1
    e_row = lax.broadcasted_iota(I32, (n_selp, tk), 0)
    e_col = lax.broadcasted_iota(I32, (n_selp, tk), 1)

    def sel_keys(kt):
        s0 = pl.multiple_of(kt * tk, tk)
        kk = jnp.concatenate([sk_ref[0, pl.ds(s0, tk), :], pf_ref[pl.ds(s0, tk), :]], axis=1)
        s_pos = s0 + lax.broadcasted_iota(I32, (1, tk), 1)
        expand = jnp.where(e_row == (s0 + e_col) // SEL_BLOCK, 1.0, 0.0).astype(BF16)
        return kk, (s_pos <= t_col) & (_dot(selm, expand) > 0.5)

    def sel_vals(kt):
        s0 = pl.multiple_of(kt * tk, tk)
        return jnp.concatenate([sv_ref[0, pl.ds(s0, tk), :], vf_ref[0:tk, :]], axis=1)

    _flash_attention(q4_ref, acc_ref, p_ref, a_ref, n_kt, sel_keys, sel_vals, tq)

    wlen = min(NSA_WINDOW + tq, s_len)
    ks = pl.multiple_of(jnp.maximum(t0 + tq - wlen, 0), LANES)
    kw = jnp.concatenate([wk_ref[0, pl.ds(ks, wlen), :], pf_ref[pl.ds(ks, wlen), :]], axis=1)
    vw = jnp.concatenate([wv_ref[0, pl.ds(ks, wlen), :], vf_ref[0:wlen, :]], axis=1)
    wd = t_col - (ks + lax.broadcasted_iota(I32, (1, wlen), 1))
    wmask = (wd >= 0) & (wd < NSA_WINDOW)
    o_win = []
    for h in range(N_HEADS):
        ah = jnp.where(wmask, _dot_nt(q4_ref[h * tq:(h + 1) * tq, :], kw), NEG)
        m = jnp.max(ah, axis=-1, keepdims=True)
        o_win.append(_dot(jnp.exp2(ah - m).astype(BF16), vw))

    gates = jax.nn.sigmoid(sm_ref[0].astype(F32))
    for h in range(N_HEADS):
        rows = slice(h * tq, (h + 1) * tq)
        o_slc = acc_ref[rows, 0:HEAD_DIM] / jnp.maximum(acc_ref[rows, HEAD_DIM:HEAD_DIM + 1], 1e-30)
        o_w = o_win[h][:, 0:HEAD_DIM] / jnp.maximum(o_win[h][:, HEAD_DIM:HEAD_DIM + 1], 1e-30)
        g0 = gates[:, NG_LANE + 3 * h:NG_LANE + 3 * h + 1]
        g1 = gates[:, NG_LANE + 3 * h + 1:NG_LANE + 3 * h + 2]
        g2 = gates[:, NG_LANE + 3 * h + 2:NG_LANE + 3 * h + 3]
        o = g0 * o_cmp[rows] + g1 * o_slc + g2 * o_w
        o_ref[0, :, h * HEAD_DIM:(h + 1) * HEAD_DIM] = o.astype(BF16)


def _nsa_mixer(proj, k_cmp, v_cmp):
    b, s, _ = proj.shape
    tq = min(ATT_TQ, s)
    tk = min(ATT_TK, s)
    n_cmp = k_cmp.shape[1]
    wlen = min(NSA_WINDOW + tq, s)
    col = lambda name: _PACK_OFF[name] // HEAD_DIM
    full = lambda name: pl.BlockSpec((1, s, HEAD_DIM), functools.partial(lambda c, bi, i: (bi, 0, c), col(name)))
    nv = max(tk, wlen)
    return pl.pallas_call(
        _nsa_kernel,
        out_shape=jax.ShapeDtypeStruct((b, s, W_MIX), BF16),
        grid=(b, s // tq),
        in_specs=[
            pl.BlockSpec((1, tq, W_MIX), lambda bi, i: (bi, i, _PACK_OFF['n_q'] // W_MIX)),
            pl.BlockSpec((1, tq, LANES), lambda bi, i: (bi, i, SMALL_OFF // LANES)),
            pl.BlockSpec((1, n_cmp, HEAD_DIM), lambda bi, i: (bi, 0, 0)),
            pl.BlockSpec((1, n_cmp, HEAD_DIM), lambda bi, i: (bi, 0, 0)),
            full('n_sk'), full('n_sv'), full('n_wk'), full('n_wv'),
            _resident((s, LANES)), _resident((N_HEADS * tq, LANES)), _resident((nv, LANES)),
        ],
        out_specs=pl.BlockSpec((1, tq, W_MIX), lambda bi, i: (bi, i, 0)),
        scratch_shapes=[pltpu.VMEM((N_HEADS * tq, 2 * HEAD_DIM), BF16),
                        pltpu.VMEM((N_HEADS * tq, n_cmp), BF16),
                        pltpu.VMEM((N_HEADS * tq, 2 * HEAD_DIM), F32),
                        pltpu.VMEM((((s // SEL_BLOCK + LANES - 1) // LANES) * LANES, tq), F32),
                        pltpu.VMEM((N_HEADS * tq, tk), BF16),
                        pltpu.VMEM((N_HEADS * tq, tk), F32)],
        compiler_params=_cparams(("parallel", "arbitrary")),
        name="nsa_attention",
    )(proj, proj, k_cmp, v_cmp, proj, proj, proj, proj,
      _pos_features(s), _slope_features(SLOPES_NSA, tq), _ones_feature(nv))


I16 = jnp.int16
I16_MIN = -2 ** 15
SEL_ROWS = 64


def _dsa_kernel(q_ref, iq_ref, sm_ref, smf_ref, k_ref, v_ref, pf_ref, qf_ref, vf_ref, o_ref,
                q4_ref, acc_ref, hi_ref, lo_ref, lq_ref, selb_ref, p_ref, a_ref):
    tq = q_ref.shape[1]
    s_len = k_ref.shape[1]
    topk = min(DSA_TOPK_MAX, s_len // 4)
    tk = min(ATT_TK, s_len)
    n_ch = tk // SEL_ROWS
    t0 = pl.program_id(1) * tq
    n_kt = (t0 + tq - 1) // tk + 1
    _stack_queries(q_ref, qf_ref, q4_ref)
    t_row = t0 + lax.broadcasted_iota(I32, (1, tq), 1)
    iq = iq_ref[0]
    iw_t = sm_ref[0].astype(F32).T
    one = jnp.ones((), BF16)
    zero = jnp.zeros((), BF16)

    def score_body(kt, _):
        s0 = pl.multiple_of(kt * tk, tk)
        ik = smf_ref[0, pl.ds(s0, tk), IK_LANE:IK_LANE + IDX_DIM]
        sc = jnp.zeros((tk, tq), F32)
        for h in range(IDX_HEADS):
            lg = _dot_nt(ik, iq[:, h * IDX_DIM:(h + 1) * IDX_DIM])
            sc = sc + jnp.maximum(lg, 0.0) * iw_t[IW_LANE + h:IW_LANE + h + 1, :]
        sc = jnp.where(sc == 0.0, 0.0, sc)
        bits = lax.bitcast_convert_type(sc, I32)
        key = bits ^ ((bits >> 31) & 0x7FFFFFFF)
        s_pos = s0 + lax.broadcasted_iota(I32, (tk, 1), 0)
        key = jnp.where(s_pos <= t_row, key, INT_MIN)
        hi_ref[pl.ds(s0, tk), :] = (key >> 16).astype(I16)
        lo_ref[pl.ds(s0, tk), :] = ((key & 0xFFFF) + I16_MIN).astype(I16)
        return 0

    lax.fori_loop(0, n_kt, score_body, 0)

    def count_ge(ref, thr_row):
        thr = jnp.broadcast_to(thr_row, (SEL_ROWS, tq))

        def body(kt, cnt):
            s0 = pl.multiple_of(kt * tk, tk)
            for c in range(n_ch):
                cnt = cnt + jnp.where(ref[pl.ds(s0 + c * SEL_ROWS, SEL_ROWS), :] >= thr, one, zero)
            return cnt
        cnt = lax.fori_loop(0, n_kt, body, jnp.zeros((SEL_ROWS, tq), BF16))
        return jnp.sum(cnt.astype(F32), axis=0, keepdims=True)

    def kth_largest(ref, k):
        def bit_body(it, cand):
            trial = cand | jnp.left_shift(jnp.int32(1), 15 - it)
            total = count_ge(ref, (trial + I16_MIN).astype(I16))
            return jnp.where(total >= k, trial, cand)
        return lax.fori_loop(0, 16, bit_body, jnp.zeros((1, tq), I32))

    cand_hi = jnp.maximum(kth_largest(hi_ref, float(topk)), 1)
    p16 = jnp.broadcast_to((cand_hi + I16_MIN).astype(I16), (SEL_ROWS, tq))
    n_above = count_ge(hi_ref, (jnp.minimum(cand_hi + 1, 2 ** 16 - 1) + I16_MIN).astype(I16))
    need = float(topk) - n_above

    def tie_body(kt, _):
        s0 = pl.multiple_of(kt * tk, tk)
        for c in range(n_ch):
            ds = pl.ds(s0 + c * SEL_ROWS, SEL_ROWS)
            lq_ref[ds, :] = jnp.where(hi_ref[ds, :] == p16, lo_ref[ds, :], jnp.full((), I16_MIN, I16))
        return 0

    lax.fori_loop(0, n_kt, tie_body, 0)
    q16 = jnp.broadcast_to((kth_largest(lq_ref, need) + I16_MIN).astype(I16), (SEL_ROWS, tq))

    def sel_body(kt, _):
        s0 = pl.multiple_of(kt * tk, tk)
        for c in range(n_ch):
            ds = pl.ds(s0 + c * SEL_ROWS, SEL_ROWS)
            hi = hi_ref[ds, :]
            tie = jnp.where(hi == p16, jnp.where(lo_ref[ds, :] >= q16, one, zero), zero)
            selb_ref[ds, :] = jnp.where(hi > p16, one, tie)
        return 0

    lax.fori_loop(0, n_kt, sel_body, 0)

    def att_keys(kt):
        s0 = pl.multiple_of(kt * tk, tk)
        kk = jnp.concatenate([k_ref[0, pl.ds(s0, tk), :], pf_ref[pl.ds(s0, tk), :]], axis=1)
        return kk, selb_ref[pl.ds(s0, tk), :].astype(F32).T > 0.5

    def att_vals(kt):
        s0 = pl.multiple_of(kt * tk, tk)
        return jnp.concatenate([v_ref[0, pl.ds(s0, tk), :], vf_ref[...]], axis=1)

    _flash_attention(q4_ref, acc_ref, p_ref, a_ref, n_kt, att_keys, att_vals, tq)
    for h in range(N_HEADS):
        rows = slice(h * tq, (h + 1) * tq)
        o = acc_ref[rows, 0:HEAD_DIM] / jnp.maximum(acc_ref[rows, HEAD_DIM:HEAD_DIM + 1], 1e-30)
        o_ref[0, :, h * HEAD_DIM:(h + 1) * HEAD_DIM] = o.astype(BF16)


def _dsa_mixer(proj):
    b, s, _ = proj.shape
    tq = min(ATT_TQ, s)
    tk = min(ATT_TK, s)
    full = lambda off: pl.BlockSpec((1, s, LANES), functools.partial(lambda c, bi, i: (bi, 0, c), off // LANES))
    return pl.pallas_call(
        _dsa_kernel,
        out_shape=jax.ShapeDtypeStruct((b, s, W_MIX), BF16),
        grid=(b, s // tq),
        in_specs=[
            pl.BlockSpec((1, tq, W_MIX), lambda bi, i: (bi, i, _PACK_OFF['d_q'] // W_MIX)),
            pl.BlockSpec((1, tq, W_MIX), lambda bi, i: (bi, i, _PACK_OFF['i_q'] // W_MIX)),
            pl.BlockSpec((1, tq, LANES), lambda bi, i: (bi, i, SMALL_OFF // LANES)),
            full(SMALL_OFF), full(_PACK_OFF['d_k']), full(_PACK_OFF['d_v']),
            _resident((s, LANES)), _resident((N_HEADS * tq, LANES)), _resident((tk, LANES)),
        ],
        out_specs=pl.BlockSpec((1, tq, W_MIX), lambda bi, i: (bi, i, 0)),
        scratch_shapes=[pltpu.VMEM((N_HEADS * tq, 2 * HEAD_DIM), BF16),
                        pltpu.VMEM((N_HEADS * tq, 2 * HEAD_DIM), F32),
                        pltpu.VMEM((s, tq), I16), pltpu.VMEM((s, tq), I16), pltpu.VMEM((s, tq), I16),
                        pltpu.VMEM((s, tq), BF16),
                        pltpu.VMEM((N_HEADS * tq, tk), BF16),
                        pltpu.VMEM((N_HEADS * tq, tk), F32)],
        compiler_params=_cparams(("parallel", "arbitrary")),
        name="dsa_attention",
    )(proj, proj, proj, proj, proj, proj,
      _pos_features(s), _slope_features(SLOPES_DSA, tq), _ones_feature(tk))


OUTPROJ_TM = 256


def _outproj_kernel(y0_ref, y1_ref, y2_ref, y3_ref, w_ref, x_ref, g_ref, b_ref, o_ref, ob_ref):
    acc = DN_ALPHA * x_ref[...]
    for gi, y_ref in enumerate((y0_ref, y1_ref, y2_ref, y3_ref)):
        acc = acc + _dot(y_ref[...], w_ref[gi * W_MIX:(gi + 1) * W_MIX, :])
    y = _layernorm(acc, g_ref[...], b_ref[...])
    o_ref[...] = y
    ob_ref[...] = y.astype(BF16)


def _outproj_ln(ys, w_out, x, g, b):
    n_tok, d = x.shape
    tm = OUTPROJ_TM
    yspec = pl.BlockSpec((tm, W_MIX), lambda i: (i, 0))
    xspec = pl.BlockSpec((tm, d), lambda i: (i, 0))
    return pl.pallas_call(
        _outproj_kernel,
        out_shape=(jax.ShapeDtypeStruct((n_tok, d), F32), jax.ShapeDtypeStruct((n_tok, d), BF16)),
        grid=(n_tok // tm,),
        in_specs=[yspec, yspec, yspec, yspec, _resident((d, d)), xspec, _resident((1, d)), _resident((1, d))],
        out_specs=(xspec, xspec),
        compiler_params=_cparams(("parallel",)),
        name="outproj_ln",
    )(*ys, w_out, x, g.reshape(1, d).astype(F32), b.reshape(1, d).astype(F32))


FFN_TM = 512
FFN_TF = 512


def _ffn_kernel(xb_ref, wg_ref, wu_ref, wd_ref, x_ref, g_ref, b_ref, o_ref, ob_ref, acc_ref):
    f = pl.program_id(1)

    @pl.when(f == 0)
    def _():
        acc_ref[...] = DN_ALPHA * x_ref[...]

    xb = xb_ref[...]
    gate = _dot(xb, wg_ref[...])
    up = _dot(xb, wu_ref[...])
    h = (gate * jax.nn.sigmoid(gate) * up).astype(BF16)
    acc_ref[...] += _dot(h, wd_ref[...])

    @pl.when(f == pl.num_programs(1) - 1)
    def _():
        y = _layernorm(acc_ref[...], g_ref[...], b_ref[...])
        o_ref[...] = y
        ob_ref[...] = y.astype(BF16)


def _ffn_ln(xb, w_gate_up, w_down, x, g, b):
    n_tok, d = x.shape
    d_ff = w_down.shape[0]
    tm, tf = FFN_TM, FFN_TF
    nf = d_ff // tf
    xspec = pl.BlockSpec((tm, d), lambda i, f: (i, 0))
    return pl.pallas_call(
        _ffn_kernel,
        out_shape=(jax.ShapeDtypeStruct((n_tok, d), F32), jax.ShapeDtypeStruct((n_tok, d), BF16)),
        grid=(n_tok // tm, nf),
        in_specs=[
            xspec,
            pl.BlockSpec((d, tf), lambda i, f: (0, f)),
            pl.BlockSpec((d, tf), lambda i, f: (0, f + nf)),
            pl.BlockSpec((tf, d), lambda i, f: (f, 0)),
            xspec, _resident((1, d)), _resident((1, d)),
        ],
        out_specs=(xspec, xspec),
        scratch_shapes=[pltpu.VMEM((tm, d), F32)],
        compiler_params=_cparams(("parallel", "arbitrary")),
        name="ffn_ln",
    )(xb, w_gate_up, w_gate_up, w_down, x, g.reshape(1, d).astype(F32), b.reshape(1, d).astype(F32))


def _pack_w_in(w):
    fold = {'d_q': ATTN_SCALE * LOG2E, 'n_q': ATTN_SCALE * LOG2E, 'i_w': IDX_SCALE}
    cols = [w[:, _SRC_OFF[n]:_SRC_OFF[n] + _SRC_SIZE[n]] * fold.get(n, 1.0) for n in _PACK_ORDER]
    used = sum(_SRC_SIZE[n] for n in _PACK_ORDER)
    cols.append(jnp.zeros((w.shape[0], D_PACK - used), w.dtype))
    return jnp.concatenate(cols, axis=1).astype(BF16)


def _layer(x, xb, p):
    b, s, d = x.shape
    n_tok = b * s
    x_in = x if xb is None else xb
    proj = _inproj(x_in.reshape(n_tok, d), _pack_w_in(p['w_in'])).reshape(b, s, D_PACK)
    y_pool = _pool_mixer(proj, p['pool_w'], p['pool_scale'])
    y_conv = _conv_mixer(proj, p['conv_w'], p['conv_b'], p['conv_ln_g'], p['conv_ln_b'],
                         p['conv_pw_w'], p['conv_pw_b'])
    y_dsa = _dsa_mixer(proj)
    ck = proj[:, :, _PACK_OFF['n_ck']:_PACK_OFF['n_ck'] + HEAD_DIM]
    cv = proj[:, :, _PACK_OFF['n_cv']:_PACK_OFF['n_cv'] + HEAD_DIM]
    k_cmp = _compress(ck, p['cmp_pos_k'], p['cmp_k_w1'], p['cmp_k_w2'])
    v_cmp = _compress(cv, p['cmp_pos_v'], p['cmp_v_w1'], p['cmp_v_w2'])
    y_nsa = _nsa_mixer(proj, k_cmp, v_cmp)
    ys = [y.reshape(n_tok, W_MIX) for y in (y_pool, y_conv, y_dsa, y_nsa)]
    x1, x1b = _outproj_ln(ys, p['w_out'].astype(BF16), x.reshape(n_tok, d), p['ln1_g'], p['ln1_b'])
    x2, x2b = _ffn_ln(x1b, p['w_gate_up'].astype(BF16), p['w_down'].astype(BF16), x1, p['ln2_g'], p['ln2_b'])
    return x2.reshape(b, s, d), x2b.reshape(b, s, d)


def kernel(x, w_in, w_out, pool_w, pool_scale, conv_w, conv_b, conv_ln_g, conv_ln_b, conv_pw_w, conv_pw_b,
           cmp_pos_k, cmp_pos_v, cmp_k_w1, cmp_k_w2, cmp_v_w1, cmp_v_w2, ln1_g, ln1_b, ln2_g, ln2_b,
           w_gate_up, w_down):
    params = dict(w_in=w_in, w_out=w_out, pool_w=pool_w, pool_scale=pool_scale, conv_w=conv_w, conv_b=conv_b,
                  conv_ln_g=conv_ln_g, conv_ln_b=conv_ln_b, conv_pw_w=conv_pw_w, conv_pw_b=conv_pw_b,
                  cmp_pos_k=cmp_pos_k, cmp_pos_v=cmp_pos_v, cmp_k_w1=cmp_k_w1, cmp_k_w2=cmp_k_w2,
                  cmp_v_w1=cmp_v_w1, cmp_v_w2=cmp_v_w2, ln1_g=ln1_g, ln1_b=ln1_b, ln2_g=ln2_g, ln2_b=ln2_b,
                  w_gate_up=w_gate_up, w_down=w_down)
    xb = None
    for l in range(w_in.shape[0]):
        x, xb = _layer(x, xb, {k: v[l] for k, v in params.items()})
    return x
```

```python
import functools

import numpy as np
import jax
import jax.numpy as jnp
from jax import lax
from jax.experimental import pallas as pl
from jax.experimental.pallas import tpu as pltpu

F32 = jnp.float32
BF16 = jnp.bfloat16
I32 = jnp.int32

D_MODEL = 2048
DEPTH = 2
W_MIX = D_MODEL // 4
HEAD_DIM = 128
N_HEADS = W_MIX // HEAD_DIM
POOL_WINDOWS = (2, 4, 8, 16)
POOL_GROUP = W_MIX // len(POOL_WINDOWS)
CONV_WIDTH = 31
IDX_HEADS = 8
IDX_DIM = 64
DSA_TOPK_MAX = 256
CMP_BLOCK = 32
CMP_STRIDE = 16
SEL_BLOCK = 64
SEL_TOPK = 16
NSA_WINDOW = 512
D_FF = ((8 * D_MODEL + 3 * 256 - 1) // (3 * 256)) * 256
DN_ALPHA = (2 * DEPTH) ** 0.25
ATTN_SCALE = HEAD_DIM ** -0.5
IDX_SCALE = (IDX_HEADS * IDX_DIM) ** -0.5
LN_EPS = 1e-5
LOG2E = 1.4426950408889634

LANES = 128
SUBLANES = 8
VMEM_LIMIT = 48 * 1024 * 1024

NEG = -1e30
INT_MIN = -2 ** 31

_SRC_SIZES = (W_MIX, W_MIX, W_MIX, W_MIX, HEAD_DIM, HEAD_DIM, IDX_HEADS * IDX_DIM, IDX_DIM, IDX_HEADS,
              W_MIX, HEAD_DIM, HEAD_DIM, HEAD_DIM, HEAD_DIM, HEAD_DIM, HEAD_DIM, N_HEADS * 3)
_SRC_NAMES = ('pool', 'c_a', 'c_g', 'd_q', 'd_k', 'd_v', 'i_q', 'i_k', 'i_w',
              'n_q', 'n_ck', 'n_cv', 'n_sk', 'n_sv', 'n_wk', 'n_wv', 'n_g')
_SRC_OFF = dict(zip(_SRC_NAMES, np.concatenate([[0], np.cumsum(_SRC_SIZES)[:-1]]).tolist()))
_SRC_SIZE = dict(zip(_SRC_NAMES, _SRC_SIZES))
_PACK_ORDER = ('pool', 'c_a', 'c_g', 'd_q', 'i_q', 'n_q', 'd_k', 'd_v',
               'n_ck', 'n_cv', 'n_sk', 'n_sv', 'n_wk', 'n_wv', 'i_k', 'i_w', 'n_g')
_PACK_OFF = {}
_o = 0
for _n in _PACK_ORDER:
    _PACK_OFF[_n] = _o
    _o += _SRC_SIZE[_n]
D_PACK = ((_o + LANES - 1) // LANES) * LANES
SMALL_OFF = _PACK_OFF['i_k']
IK_LANE = 0
IW_LANE = _PACK_OFF['i_w'] - SMALL_OFF
NG_LANE = _PACK_OFF['n_g'] - SMALL_OFF


def _alibi_slopes():
    n = 2 * N_HEADS
    s = np.power(2.0, -8.0 * np.arange(1, n + 1) / n).astype(np.float32)
    return [float(v) for v in s[0::2]], [float(v) for v in s[1::2]]


SLOPES_DSA, SLOPES_NSA = _alibi_slopes()


def _cparams(sem):
    return pltpu.CompilerParams(dimension_semantics=sem, vmem_limit_bytes=VMEM_LIMIT)


def _resident(shape):
    nd = len(shape)
    return pl.BlockSpec(shape, lambda *_: (0,) * nd, pipeline_mode=pl.Buffered(1))


def _dot(a, b):
    return jnp.dot(a, b, preferred_element_type=F32)


def _dot_nt(a, b):
    return lax.dot_general(a, b, (((1,), (1,)), ((), ())), preferred_element_type=F32)


def _layernorm(x, g, b):
    mu = jnp.mean(x, axis=-1, keepdims=True)
    xc = x - mu
    var = jnp.mean(xc * xc, axis=-1, keepdims=True)
    return xc * lax.rsqrt(var + LN_EPS) * g + b


INPROJ_TM = 512
INPROJ_CHUNK = 512


def _inproj_kernel(x_ref, w_ref, o_ref, ck_ref, cv_ref):
    x = x_ref[...].astype(BF16)
    n = o_ref.shape[1]
    for c0 in range(0, n, INPROJ_CHUNK):
        c1 = min(c0 + INPROJ_CHUNK, n)
        o_ref[:, c0:c1] = _dot(x, w_ref[:, c0:c1]).astype(BF16)
    ck_ref[...] = o_ref[:, _PACK_OFF['n_ck']:_PACK_OFF['n_ck'] + HEAD_DIM]
    cv_ref[...] = o_ref[:, _PACK_OFF['n_cv']:_PACK_OFF['n_cv'] + HEAD_DIM]


def _inproj(x, w_pack):
    n_tok, d = x.shape
    tm = INPROJ_TM
    col = pl.BlockSpec((tm, HEAD_DIM), lambda i: (i, 0))
    return pl.pallas_call(
        _inproj_kernel,
        out_shape=(jax.ShapeDtypeStruct((n_tok, D_PACK), BF16),
                   jax.ShapeDtypeStruct((n_tok, HEAD_DIM), BF16), jax.ShapeDtypeStruct((n_tok, HEAD_DIM), BF16)),
        grid=(n_tok // tm,),
        in_specs=[pl.BlockSpec((tm, d), lambda i: (i, 0)), _resident((d, D_PACK))],
        out_specs=(pl.BlockSpec((tm, D_PACK), lambda i: (i, 0)), col, col),
        compiler_params=_cparams(("parallel",)),
        name="inproj",
    )(x, w_pack)


POOL_TS = 512
POOL_HALO = 16


def _pool_kernel(u_ref, halo_ref, w_ref, sc_ref, o_ref, xs_ref):
    i = pl.program_id(1)
    ts = u_ref.shape[1]
    xs_ref[POOL_HALO:POOL_HALO + ts, :] = u_ref[0].astype(F32)
    xs_ref[0:POOL_HALO, :] = jnp.where(i > 0, halo_ref[0].astype(F32), 0.0)
    pos = i * ts + lax.broadcasted_iota(I32, (ts, 1), 0)
    for g, win in enumerate(POOL_WINDOWS):
        c = slice(g * POOL_GROUP, (g + 1) * POOL_GROUP)
        x = xs_ref[POOL_HALO:POOL_HALO + ts, c]
        acc = x
        for k in range(1, win):
            acc = acc + xs_ref[POOL_HALO - k:POOL_HALO - k + ts, c]
        cnt = jnp.minimum(pos + 1, win).astype(F32)
        d = acc / cnt - x
        y = _dot(d.astype(BF16), w_ref[g])
        o_ref[0, :, c] = (y * sc_ref[:, c]).astype(BF16)


def _pool_mixer(proj, pool_w, pool_scale):
    b, s, _ = proj.shape
    ts = min(POOL_TS, s)
    hb = ts // POOL_HALO
    blk = _PACK_OFF['pool'] // W_MIX
    return pl.pallas_call(
        _pool_kernel,
        out_shape=jax.ShapeDtypeStruct((b, s, W_MIX), BF16),
        grid=(b, s // ts),
        in_specs=[
            pl.BlockSpec((1, ts, W_MIX), lambda bi, i: (bi, i, blk)),
            pl.BlockSpec((1, POOL_HALO, W_MIX), lambda bi, i: (bi, jnp.maximum(i * hb - 1, 0), blk)),
            _resident(pool_w.shape),
            _resident((1, W_MIX)),
        ],
        out_specs=pl.BlockSpec((1, ts, W_MIX), lambda bi, i: (bi, i, 0)),
        scratch_shapes=[pltpu.VMEM((ts + POOL_HALO, W_MIX), F32)],
        compiler_params=_cparams(("parallel", "parallel")),
        name="pool_mixer",
    )(proj, proj, pool_w.astype(BF16), pool_scale.reshape(1, W_MIX).astype(F32))


CONV_TS = 512
CONV_HALO = 32
CONV_ROWS = 32


def _conv_kernel(a_ref, g_ref, ha_ref, hg_ref, cw_ref, cb_ref, lg_ref, lb_ref, pw_ref, pb_ref,
                 o_ref, hs_ref, sh_ref, y_ref):
    i = pl.program_id(1)
    ts = a_ref.shape[1]
    hs_ref[CONV_HALO:CONV_HALO + ts, :] = a_ref[0].astype(F32) * jax.nn.sigmoid(g_ref[0].astype(F32))
    halo = ha_ref[0].astype(F32) * jax.nn.sigmoid(hg_ref[0].astype(F32))
    hs_ref[0:CONV_HALO, :] = jnp.where(i > 0, halo, 0.0)
    n_sh = ts + CONV_HALO - SUBLANES
    for b in range(1, SUBLANES):
        sh_ref[b - 1, 0:n_sh, :] = hs_ref[b:b + n_sh, :]
    base = CONV_HALO - (CONV_WIDTH - 1)
    for r0 in range(0, ts, CONV_ROWS):
        acc = jnp.broadcast_to(cb_ref[...], (CONV_ROWS, W_MIX))
        for j in range(CONV_WIDTH):
            a8, b = divmod(base + j, SUBLANES)
            r = r0 + a8 * SUBLANES
            src = hs_ref[r:r + CONV_ROWS, :] if b == 0 else sh_ref[b - 1, r:r + CONV_ROWS, :]
            acc = acc + src * cw_ref[j:j + 1, :]
        y = _layernorm(acc, lg_ref[...], lb_ref[...])
        y_ref[r0:r0 + CONV_ROWS, :] = (y * jax.nn.sigmoid(y)).astype(BF16)
    o_ref[0] = (_dot(y_ref[...], pw_ref[...]) + pb_ref[...]).astype(BF16)


def _conv_mixer(proj, conv_w, conv_b, ln_g, ln_b, pw_w, pw_b):
    b, s, _ = proj.shape
    ts = min(CONV_TS, s)
    hb = ts // CONV_HALO
    ba = _PACK_OFF['c_a'] // W_MIX
    bg = _PACK_OFF['c_g'] // W_MIX
    row = lambda v: v.reshape(1, W_MIX).astype(F32)
    cw = jnp.concatenate([conv_w.astype(F32), jnp.zeros((1, W_MIX), F32)], axis=0)
    halo_map = lambda blk: (lambda bi, i: (bi, jnp.maximum(i * hb - 1, 0), blk))
    return pl.pallas_call(
        _conv_kernel,
        out_shape=jax.ShapeDtypeStruct((b, s, W_MIX), BF16),
        grid=(b, s // ts),
        in_specs=[
            pl.BlockSpec((1, ts, W_MIX), lambda bi, i: (bi, i, ba)),
            pl.BlockSpec((1, ts, W_MIX), lambda bi, i: (bi, i, bg)),
            pl.BlockSpec((1, CONV_HALO, W_MIX), halo_map(ba)),
            pl.BlockSpec((1, CONV_HALO, W_MIX), halo_map(bg)),
            _resident(cw.shape), _resident((1, W_MIX)), _resident((1, W_MIX)), _resident((1, W_MIX)),
            _resident((W_MIX, W_MIX)), _resident((1, W_MIX)),
        ],
        out_specs=pl.BlockSpec((1, ts, W_MIX), lambda bi, i: (bi, i, 0)),
        scratch_shapes=[pltpu.VMEM((ts + CONV_HALO, W_MIX), F32),
                        pltpu.VMEM((SUBLANES - 1, ts + CONV_HALO - SUBLANES, W_MIX), F32),
                        pltpu.VMEM((ts, W_MIX), BF16)],
        compiler_params=_cparams(("parallel", "parallel")),
        name="conv_mixer",
    )(proj, proj, proj, proj, cw, row(conv_b), row(ln_g), row(ln_b), pw_w.astype(BF16), row(pw_b))


def _compress_kernel(r_ref, pos_ref, w1_ref, w2_ref, o_ref):
    r = r_ref[0]
    half = r.shape[1]
    n = r.shape[0]
    top = _dot(r, w1_ref[0:half, :])
    bot = _dot(r, w1_ref[half:2 * half, :])
    bot_next = pltpu.roll(bot, n - 1, 0)
    posb = _dot(jnp.broadcast_to(pos_ref[...], (8, 2 * half)), w1_ref[...])[0:1, :]
    h = jax.nn.gelu(top + bot_next + posb)
    out = _dot(h.astype(BF16), w2_ref[...])
    row = lax.broadcasted_iota(I32, out.shape, 0)
    o_ref[0] = jnp.where(row < n - 1, out, 0.0).astype(BF16)


def _compress(raw, pos, w1, w2):
    b, s, d = raw.shape
    n = s // CMP_STRIDE
    r = raw.reshape(b, n, CMP_STRIDE * d)
    return pl.pallas_call(
        _compress_kernel,
        out_shape=jax.ShapeDtypeStruct((b, n, d), BF16),
        grid=(b,),
        in_specs=[pl.BlockSpec((1, n, CMP_STRIDE * d), lambda bi: (bi, 0, 0)),
                  _resident((1, CMP_BLOCK * d)), _resident((CMP_BLOCK * d, d)), _resident((d, d))],
        out_specs=pl.BlockSpec((1, n, d), lambda bi: (bi, 0, 0)),
        compiler_params=_cparams(("parallel",)),
        name="nsa_compress",
    )(r, pos.reshape(1, CMP_BLOCK * d).astype(BF16), w1.astype(BF16), w2.astype(BF16))


ATT_TQ = 256
ATT_TK = 512
POS_RADIX = 64


def _pos_features(s_len):
    assert s_len <= POS_RADIX * 256
    s = np.arange(s_len)
    f = np.zeros((s_len, LANES), np.float32)
    f[:, 0] = f[:, 1] = s // POS_RADIX
    f[:, 2] = f[:, 3] = s % POS_RADIX
    return jnp.asarray(f, BF16)


def _slope_features(slopes, tq):
    f = np.zeros((len(slopes) * tq, LANES), np.float32)
    for h, sl in enumerate(slopes):
        c = np.float32(sl * LOG2E)
        ca = np.float32(np.asarray(c, dtype=BF16))
        cb = np.float32(np.asarray(c - ca, dtype=BF16))
        f[h * tq:(h + 1) * tq, 0:4] = [POS_RADIX * ca, POS_RADIX * cb, ca, cb]
    return jnp.asarray(f, BF16)


def _ones_feature(n):
    f = np.zeros((n, LANES), np.float32)
    f[:, 0] = 1.0
    return jnp.asarray(f, BF16)


def _stack_queries(q_ref, qf_ref, q4_ref):
    tq = q_ref.shape[1]
    for h in range(N_HEADS):
        q4_ref[h * tq:(h + 1) * tq, 0:HEAD_DIM] = q_ref[0, :, h * HEAD_DIM:(h + 1) * HEAD_DIM]
    q4_ref[:, HEAD_DIM:2 * HEAD_DIM] = qf_ref[...]


def _flash_attention(q4_ref, acc_ref, p_ref, a_ref, n_kt, keys_fn, vals_fn, tq):
    acc_ref[...] = jnp.zeros(acc_ref.shape, F32)
    p_ref[...] = jnp.zeros(p_ref.shape, BF16)
    heads = [slice(h * tq, (h + 1) * tq) for h in range(N_HEADS)]

    def logits(rows, kk, mask):
        return jnp.where(mask, _dot_nt(q4_ref[rows, :], kk), NEG)

    def step(kt, ms, with_next):
        vv = vals_fn(jnp.maximum(kt - 1, 0))
        nxt = keys_fn(kt + 1) if with_next else None
        new_ms = []
        for h, rows in enumerate(heads):
            ah = a_ref[rows, :]
            pv = _dot(p_ref[rows, :], vv)
            if with_next:
                a_ref[rows, :] = logits(rows, *nxt)
            m_new = jnp.maximum(ms[h], jnp.max(ah, axis=-1, keepdims=True))
            p_ref[rows, :] = jnp.exp2(ah - m_new).astype(BF16)
            acc_ref[rows, :] = jnp.exp2(ms[h] - m_new) * (acc_ref[rows, :] + pv)
            new_ms.append(m_new)
        return tuple(new_ms)

    first = keys_fn(0)
    for rows in heads:
        a_ref[rows, :] = logits(rows, *first)
    m0 = tuple(jnp.full((tq, 1), NEG, F32) for _ in range(N_HEADS))
    ms = lax.fori_loop(0, n_kt - 1, lambda kt, ms: step(kt, ms, True), m0)
    step(n_kt - 1, ms, False)
    vv = vals_fn(n_kt - 1)
    for rows in heads:
        acc_ref[rows, :] += _dot(p_ref[rows, :], vv)


def _softmax2_rows(a, mask):
    a = jnp.where(mask, a, NEG)
    m = jnp.max(a, axis=-1, keepdims=True)
    e = jnp.where(mask, jnp.exp2(a - m), 0.0)
    s = jnp.sum(e, axis=-1, keepdims=True)
    return e / jnp.maximum(s, 1e-30)


def _split3(x):
    hi = x.astype(BF16)
    r1 = x - hi.astype(F32)
    mid = r1.astype(BF16)
    lo = (r1 - mid.astype(F32)).astype(BF16)
    return hi, mid, lo


def _nsa_kernel(q_ref, sm_ref, kc_ref, vc_ref, sk_ref, sv_ref, wk_ref, wv_ref, pf_ref, qf_ref, vf_ref,
                o_ref, q4_ref, p4_ref, acc_ref, selm_ref, p_ref, a_ref):
    tq = q_ref.shape[1]
    s_len = sk_ref.shape[1]
    n_cmp = kc_ref.shape[1]
    n_sel = s_len // SEL_BLOCK
    n_top = min(SEL_TOPK, n_sel)
    t0 = pl.program_id(1) * tq
    _stack_queries(q_ref, qf_ref, q4_ref)
    t_col = t0 + lax.broadcasted_iota(I32, (tq, 1), 0)

    a_all = _dot_nt(q4_ref[:, 0:HEAD_DIM], kc_ref[0])
    c_idx = lax.broadcasted_iota(I32, (1, n_cmp), 1)
    cd = t_col - (c_idx * CMP_STRIDE + (CMP_BLOCK - 1))
    cmask = (cd >= 0) & (c_idx < n_cmp - 1)
    cdf = cd.astype(F32)
    p_sum = jnp.zeros((tq, n_cmp), F32)
    for h in range(N_HEADS):
        p = _softmax2_rows(a_all[h * tq:(h + 1) * tq] - (SLOPES_NSA[h] * LOG2E) * cdf, cmask)
        p_sum = p_sum + p
        p4_ref[h * tq:(h + 1) * tq, 0:n_cmp] = p.astype(BF16)
    o_cmp = _dot(p4_ref[:, 0:n_cmp], vc_ref[0])

    n_selp = selm_ref.shape[0]
    jj = lax.broadcasted_iota(I32, (n_selp, n_cmp), 0)
    cc = lax.broadcasted_iota(I32, (n_selp, n_cmp), 1)
    c_start = cc * CMP_STRIDE
    overlap = ((c_start < (jj + 1) * SEL_BLOCK) & (c_start + (CMP_BLOCK - 1) >= jj * SEL_BLOCK)
               & (cc < n_cmp - 1))
    ov = jnp.where(overlap, 1.0, 0.0).astype(BF16)
    hi, mid, lo = _split3(p_sum)
    imp = _dot_nt(ov, hi) + _dot_nt(ov, mid) + _dot_nt(ov, lo)
    j_col = lax.broadcasted_iota(I32, (n_selp, 1), 0)
    t_blk = (t0 + lax.broadcasted_iota(I32, (1, tq), 1)) // SEL_BLOCK
    forced = (j_col == 0) | (j_col == t_blk) | (j_col == t_blk - 1)
    imp = jnp.where(forced, jnp.inf, imp)
    imp = jnp.where(j_col <= t_blk, imp, -jnp.inf)
    rank = jnp.zeros((n_selp, tq), F32)
    for i2 in range(n_sel):
        ci = imp[i2:i2 + 1, :]
        tie_first = jnp.where(j_col > i2, 1.0, 0.0)
        rank = rank + jnp.where(ci > imp, 1.0, jnp.where(ci == imp, tie_first, 0.0))
    selm_ref[...] = jnp.where((rank < n_top) & (j_col < n_sel), 1.0, 0.0)
    selm = selm_ref[...].T.astype(BF16)

    tk = min(ATT_TK, s_len)
    n_kt = (t0 + tq - 1) // tk + 1
    e_row = lax.broadcasted_iota(I32, (n_selp, tk), 0)
    e_col = lax.broadcasted_iota(I32, (n_selp, tk), 1)

    def sel_keys(kt):
        s0 = pl.multiple_of(kt * tk, tk)
        kk = jnp.concatenate([sk_ref[0, pl.ds(s0, tk), :], pf_ref[pl.ds(s0, tk), :]], axis=1)
        s_pos = s0 + lax.broadcasted_iota(I32, (1, tk), 1)
        expand = jnp.where(e_row == (s0 + e_col) // SEL_BLOCK, 1.0, 0.0).astype(BF16)
        return kk, (s_pos <= t_col) & (_dot(selm, expand) > 0.5)

    def sel_vals(kt):
        s0 = pl.multiple_of(kt * tk, tk)
        return jnp.concatenate([sv_ref[0, pl.ds(s0, tk), :], vf_ref[0:tk, :]], axis=1)

    _flash_attention(q4_ref, acc_ref, p_ref, a_ref, n_kt, sel_keys, sel_vals, tq)

    wlen = min(NSA_WINDOW + tq, s_len)
    ks = pl.multiple_of(jnp.maximum(t0 + tq - wlen, 0), LANES)
    kw = jnp.concatenate([wk_ref[0, pl.ds(ks, wlen), :], pf_ref[pl.ds(ks, wlen), :]], axis=1)
    vw = jnp.concatenate([wv_ref[0, pl.ds(ks, wlen), :], vf_ref[0:wlen, :]], axis=1)
    wd = t_col - (ks + lax.broadcasted_iota(I32, (1, wlen), 1))
    wmask = (wd >= 0) & (wd < NSA_WINDOW)
    o_win = []
    for h in range(N_HEADS):
        ah = jnp.where(wmask, _dot_nt(q4_ref[h * tq:(h + 1) * tq, :], kw), NEG)
        m = jnp.max(ah, axis=-1, keepdims=True)
        o_win.append(_dot(jnp.exp2(ah - m).astype(BF16), vw))

    gates = jax.nn.sigmoid(sm_ref[0].astype(F32))
    for h in range(N_HEADS):
        rows = slice(h * tq, (h + 1) * tq)
        o_slc = acc_ref[rows, 0:HEAD_DIM] / jnp.maximum(acc_ref[rows, HEAD_DIM:HEAD_DIM + 1], 1e-30)
        o_w = o_win[h][:, 0:HEAD_DIM] / jnp.maximum(o_win[h][:, HEAD_DIM:HEAD_DIM + 1], 1e-30)
        g0 = gates[:, NG_LANE + 3 * h:NG_LANE + 3 * h + 1]
        g1 = gates[:, NG_LANE + 3 * h + 1:NG_LANE + 3 * h + 2]
        g2 = gates[:, NG_LANE + 3 * h + 2:NG_LANE + 3 * h + 3]
        o = g0 * o_cmp[rows] + g1 * o_slc + g2 * o_w
        o_ref[0, :, h * HEAD_DIM:(h + 1) * HEAD_DIM] = o.astype(BF16)


def _nsa_mixer(proj, k_cmp, v_cmp):
    b, s, _ = proj.shape
    tq = min(ATT_TQ, s)
    tk = min(ATT_TK, s)
    n_cmp = k_cmp.shape[1]
    wlen = min(NSA_WINDOW + tq, s)
    col = lambda name: _PACK_OFF[name] // HEAD_DIM
    full = lambda name: pl.BlockSpec((1, s, HEAD_DIM), functools.partial(lambda c, bi, i: (bi, 0, c), col(name)))
    nv = max(tk, wlen)
    return pl.pallas_call(
        _nsa_kernel,
        out_shape=jax.ShapeDtypeStruct((b, s, W_MIX), BF16),
        grid=(b, s // tq),
        in_specs=[
            pl.BlockSpec((1, tq, W_MIX), lambda bi, i: (bi, i, _PACK_OFF['n_q'] // W_MIX)),
            pl.BlockSpec((1, tq, LANES), lambda bi, i: (bi, i, SMALL_OFF // LANES)),
            pl.BlockSpec((1, n_cmp, HEAD_DIM), lambda bi, i: (bi, 0, 0)),
            pl.BlockSpec((1, n_cmp, HEAD_DIM), lambda bi, i: (bi, 0, 0)),
            full('n_sk'), full('n_sv'), full('n_wk'), full('n_wv'),
            _resident((s, LANES)), _resident((N_HEADS * tq, LANES)), _resident((nv, LANES)),
        ],
        out_specs=pl.BlockSpec((1, tq, W_MIX), lambda bi, i: (bi, i, 0)),
        scratch_shapes=[pltpu.VMEM((N_HEADS * tq, 2 * HEAD_DIM), BF16),
                        pltpu.VMEM((N_HEADS * tq, n_cmp), BF16),
                        pltpu.VMEM((N_HEADS * tq, 2 * HEAD_DIM), F32),
                        pltpu.VMEM((((s // SEL_BLOCK + LANES - 1) // LANES) * LANES, tq), F32),
                        pltpu.VMEM((N_HEADS * tq, tk), BF16),
                        pltpu.VMEM((N_HEADS * tq, tk), F32)],
        compiler_params=_cparams(("parallel", "arbitrary")),
        name="nsa_attention",
    )(proj, proj, k_cmp, v_cmp, proj, proj, proj, proj,
      _pos_features(s), _slope_features(SLOPES_NSA, tq), _ones_feature(nv))


I16 = jnp.int16
I16_MIN = -2 ** 15
SEL_ROWS = 64


def _dsa_kernel(q_ref, iq_ref, sm_ref, smf_ref, k_ref, v_ref, pf_ref, qf_ref, vf_ref, o_ref,
                q4_ref, acc_ref, hi_ref, lo_ref, lq_ref, selb_ref, p_ref, a_ref):
    tq = q_ref.shape[1]
    s_len = k_ref.shape[1]
    topk = min(DSA_TOPK_MAX, s_len // 4)
    tk = min(ATT_TK, s_len)
    n_ch = tk // SEL_ROWS
    t0 = pl.program_id(1) * tq
    n_kt = (t0 + tq - 1) // tk + 1
    _stack_queries(q_ref, qf_ref, q4_ref)
    t_row = t0 + lax.broadcasted_iota(I32, (1, tq), 1)
    iq = iq_ref[0]
    iw_t = sm_ref[0].astype(F32).T
    one = jnp.ones((), BF16)
    zero = jnp.zeros((), BF16)

    def score_body(kt, _):
        for s0 in (pl.multiple_of(kt * tk, tk), pl.multiple_of(kt * tk + tk // 2, tk // 2)):
            ik = smf_ref[0, pl.ds(s0, tk // 2), IK_LANE:IK_LANE + IDX_DIM]
            sc = jnp.zeros((tk // 2, tq), F32)
            for h in range(IDX_HEADS):
                lg = _dot_nt(ik, iq[:, h * IDX_DIM:(h + 1) * IDX_DIM])
                sc = sc + jnp.maximum(lg, 0.0) * iw_t[IW_LANE + h:IW_LANE + h + 1, :]
            sc = jnp.where(sc == 0.0, 0.0, sc)
            bits = lax.bitcast_convert_type(sc, I32)
            key = bits ^ ((bits >> 31) & 0x7FFFFFFF)
            s_pos = s0 + lax.broadcasted_iota(I32, (tk // 2, 1), 0)
            key = jnp.where(s_pos <= t_row, key, INT_MIN)
            hi_ref[pl.ds(s0, tk // 2), :] = (key >> 16).astype(I16)
            lo_ref[pl.ds(s0, tk // 2), :] = ((key & 0xFFFF) + I16_MIN).astype(I16)
        return 0

    lax.fori_loop(0, n_kt, score_body, 0)

    def count_ge(ref, thr_row):
        thr = jnp.broadcast_to(thr_row, (SEL_ROWS, tq))

        def body(kt, cnt):
            s0 = pl.multiple_of(kt * tk, tk)
            for c in range(n_ch):
                cnt = cnt + jnp.where(ref[pl.ds(s0 + c * SEL_ROWS, SEL_ROWS), :] >= thr, one, zero)
            return cnt
        cnt = lax.fori_loop(0, n_kt, body, jnp.zeros((SEL_ROWS, tq), BF16))
        return jnp.sum(cnt.astype(F32), axis=0, keepdims=True)

    def kth_largest(ref, k):
        def bit_body(it, cand):
            trial = cand | jnp.left_shift(jnp.int32(1), 15 - it)
            total = count_ge(ref, (trial + I16_MIN).astype(I16))
            return jnp.where(total >= k, trial, cand)
        return lax.fori_loop(0, 16, bit_body, jnp.zeros((1, tq), I32))

    cand_hi = jnp.maximum(kth_largest(hi_ref, float(topk)), 1)
    p16 = jnp.broadcast_to((cand_hi + I16_MIN).astype(I16), (SEL_ROWS, tq))
    n_above = count_ge(hi_ref, (jnp.minimum(cand_hi + 1, 2 ** 16 - 1) + I16_MIN).astype(I16))
    need = float(topk) - n_above

    def tie_body(kt, _):
        s0 = pl.multiple_of(kt * tk, tk)
        for c in range(n_ch):
            ds = pl.ds(s0 + c * SEL_ROWS, SEL_ROWS)
            lq_ref[ds, :] = jnp.where(hi_ref[ds, :] == p16, lo_ref[ds, :], jnp.full((), I16_MIN, I16))
        return 0

    lax.fori_loop(0, n_kt, tie_body, 0)
    q16 = jnp.broadcast_to((kth_largest(lq_ref, need) + I16_MIN).astype(I16), (SEL_ROWS, tq))

    def sel_body(kt, _):
        s0 = pl.multiple_of(kt * tk, tk)
        for c in range(n_ch):
            ds = pl.ds(s0 + c * SEL_ROWS, SEL_ROWS)
            hi = hi_ref[ds, :]
            tie = jnp.where(hi == p16, jnp.where(lo_ref[ds, :] >= q16, one, zero), zero)
            selb_ref[ds, :] = jnp.where(hi > p16, one, tie)
        return 0

    lax.fori_loop(0, n_kt, sel_body, 0)

    def att_keys(kt):
        s0 = pl.multiple_of(kt * tk, tk)
        kk = jnp.concatenate([k_ref[0, pl.ds(s0, tk), :], pf_ref[pl.ds(s0, tk), :]], axis=1)
        return kk, selb_ref[pl.ds(s0, tk), :].astype(F32).T > 0.5

    def att_vals(kt):
        s0 = pl.multiple_of(kt * tk, tk)
        return jnp.concatenate([v_ref[0, pl.ds(s0, tk), :], vf_ref[...]], axis=1)

    _flash_attention(q4_ref, acc_ref, p_ref, a_ref, n_kt, att_keys, att_vals, tq)
    for h in range(N_HEADS):
        rows = slice(h * tq, (h + 1) * tq)
        o = acc_ref[rows, 0:HEAD_DIM] / jnp.maximum(acc_ref[rows, HEAD_DIM:HEAD_DIM + 1], 1e-30)
        o_ref[0, :, h * HEAD_DIM:(h + 1) * HEAD_DIM] = o.astype(BF16)


def _dsa_mixer(proj):
    b, s, _ = proj.shape
    tq = min(ATT_TQ, s)
    tk = min(ATT_TK, s)
    full = lambda off: pl.BlockSpec((1, s, LANES), functools.partial(lambda c, bi, i: (bi, 0, c), off // LANES))
    return pl.pallas_call(
        _dsa_kernel,
        out_shape=jax.ShapeDtypeStruct((b, s, W_MIX), BF16),
        grid=(b, s // tq),
        in_specs=[
            pl.BlockSpec((1, tq, W_MIX), lambda bi, i: (bi, i, _PACK_OFF['d_q'] // W_MIX)),
            pl.BlockSpec((1, tq, W_MIX), lambda bi, i: (bi, i, _PACK_OFF['i_q'] // W_MIX)),
            pl.BlockSpec((1, tq, LANES), lambda bi, i: (bi, i, SMALL_OFF // LANES)),
            full(SMALL_OFF), full(_PACK_OFF['d_k']), full(_PACK_OFF['d_v']),
            _resident((s, LANES)), _resident((N_HEADS * tq, LANES)), _resident((tk, LANES)),
        ],
        out_specs=pl.BlockSpec((1, tq, W_MIX), lambda bi, i: (bi, i, 0)),
        scratch_shapes=[pltpu.VMEM((N_HEADS * tq, 2 * HEAD_DIM), BF16),
                        pltpu.VMEM((N_HEADS * tq, 2 * HEAD_DIM), F32),
                        pltpu.VMEM((s, tq), I16), pltpu.VMEM((s, tq), I16), pltpu.VMEM((s, tq), I16),
                        pltpu.VMEM((s, tq), BF16),
                        pltpu.VMEM((N_HEADS * tq, tk), BF16),
                        pltpu.VMEM((N_HEADS * tq, tk), F32)],
        compiler_params=_cparams(("parallel", "arbitrary")),
        name="dsa_attention",
    )(proj, proj, proj, proj, proj, proj,
      _pos_features(s), _slope_features(SLOPES_DSA, tq), _ones_feature(tk))


OUTPROJ_TM = 512
LN_ROWS = 256


def _outproj_kernel(y0_ref, y1_ref, y2_ref, y3_ref, w_ref, x_ref, g_ref, b_ref, o_ref, ob_ref):
    for r0 in range(0, x_ref.shape[0], LN_ROWS):
        rows = slice(r0, r0 + LN_ROWS)
        acc = DN_ALPHA * x_ref[rows, :]
        for gi, y_ref in enumerate((y0_ref, y1_ref, y2_ref, y3_ref)):
            acc = acc + _dot(y_ref[rows, :], w_ref[gi * W_MIX:(gi + 1) * W_MIX, :])
        y = _layernorm(acc, g_ref[...], b_ref[...])
        o_ref[rows, :] = y
        ob_ref[rows, :] = y.astype(BF16)


def _outproj_ln(ys, w_out, x, g, b):
    n_tok, d = x.shape
    tm = OUTPROJ_TM
    yspec = pl.BlockSpec((tm, W_MIX), lambda i: (i, 0))
    xspec = pl.BlockSpec((tm, d), lambda i: (i, 0))
    return pl.pallas_call(
        _outproj_kernel,
        out_shape=(jax.ShapeDtypeStruct((n_tok, d), F32), jax.ShapeDtypeStruct((n_tok, d), BF16)),
        grid=(n_tok // tm,),
        in_specs=[yspec, yspec, yspec, yspec, _resident((d, d)), xspec, _resident((1, d)), _resident((1, d))],
        out_specs=(xspec, xspec),
        compiler_params=_cparams(("parallel",)),
        name="outproj_ln",
    )(*ys, w_out, x, g.reshape(1, d).astype(F32), b.reshape(1, d).astype(F32))


FFN_UP_TM = 1024
FFN_TF = 512
FFN_DOWN_TM = 256


def _ffn_up_kernel(xb_ref, wg_ref, wu_ref, h_ref):
    xb = xb_ref[...]
    gate = _dot(xb, wg_ref[...])
    up = _dot(xb, wu_ref[...])
    h_ref[...] = (gate * jax.nn.sigmoid(gate) * up).astype(BF16)


def _ffn_down_kernel(h_ref, wd_ref, x_ref, g_ref, b_ref, o_ref, ob_ref=None):
    y = _layernorm(DN_ALPHA * x_ref[...] + _dot(h_ref[...], wd_ref[...]), g_ref[...], b_ref[...])
    o_ref[...] = y
    if ob_ref is not None:
        ob_ref[...] = y.astype(BF16)


def _ffn_ln(xb, w_gate_up, w_down, x, g, b, want_bf16):
    n_tok, d = x.shape
    d_ff = w_down.shape[0]
    tm, tf = FFN_UP_TM, FFN_TF
    nf = d_ff // tf
    h = pl.pallas_call(
        _ffn_up_kernel,
        out_shape=jax.ShapeDtypeStruct((n_tok, d_ff), BF16),
        grid=(nf, n_tok // tm),
        in_specs=[
            pl.BlockSpec((tm, d), lambda f, i: (i, 0)),
            pl.BlockSpec((d, tf), lambda f, i: (0, f)),
            pl.BlockSpec((d, tf), lambda f, i: (0, f + nf)),
        ],
        out_specs=pl.BlockSpec((tm, tf), lambda f, i: (i, f)),
        compiler_params=_cparams(("parallel", "parallel")),
        name="ffn_up",
    )(xb, w_gate_up, w_gate_up)
    tm = FFN_DOWN_TM
    xspec = pl.BlockSpec((tm, d), lambda i: (i, 0))
    out_dtypes = (F32, BF16) if want_bf16 else (F32,)
    outs = pl.pallas_call(
        _ffn_down_kernel,
        out_shape=tuple(jax.ShapeDtypeStruct((n_tok, d), t) for t in out_dtypes),
        grid=(n_tok // tm,),
        in_specs=[pl.BlockSpec((tm, d_ff), lambda i: (i, 0)), _resident((d_ff, d)),
                  xspec, _resident((1, d)), _resident((1, d))],
        out_specs=tuple(xspec for _ in out_dtypes),
        compiler_params=_cparams(("parallel",)),
        name="ffn_down_ln",
    )(h, w_down, x, g.reshape(1, d).astype(F32), b.reshape(1, d).astype(F32))
    return outs if want_bf16 else (outs[0], None)


def _pack_w_in(w):
    fold = {'d_q': ATTN_SCALE * LOG2E, 'n_q': ATTN_SCALE * LOG2E, 'i_w': IDX_SCALE}
    cols = [w[:, _SRC_OFF[n]:_SRC_OFF[n] + _SRC_SIZE[n]] * fold.get(n, 1.0) for n in _PACK_ORDER]
    used = sum(_SRC_SIZE[n] for n in _PACK_ORDER)
    cols.append(jnp.zeros((w.shape[0], D_PACK - used), w.dtype))
    return jnp.concatenate(cols, axis=1).astype(BF16)


def _layer(x, xb, p, last):
    b, s, d = x.shape
    n_tok = b * s
    x_in = x if xb is None else xb
    proj, ck, cv = _inproj(x_in.reshape(n_tok, d), _pack_w_in(p['w_in']))
    proj = proj.reshape(b, s, D_PACK)
    y_pool = _pool_mixer(proj, p['pool_w'], p['pool_scale'])
    y_conv = _conv_mixer(proj, p['conv_w'], p['conv_b'], p['conv_ln_g'], p['conv_ln_b'],
                         p['conv_pw_w'], p['conv_pw_b'])
    y_dsa = _dsa_mixer(proj)
    k_cmp = _compress(ck.reshape(b, s, HEAD_DIM), p['cmp_pos_k'], p['cmp_k_w1'], p['cmp_k_w2'])
    v_cmp = _compress(cv.reshape(b, s, HEAD_DIM), p['cmp_pos_v'], p['cmp_v_w1'], p['cmp_v_w2'])
    y_nsa = _nsa_mixer(proj, k_cmp, v_cmp)
    ys = [y.reshape(n_tok, W_MIX) for y in (y_pool, y_conv, y_dsa, y_nsa)]
    x1, x1b = _outproj_ln(ys, p['w_out'].astype(BF16), x.reshape(n_tok, d), p['ln1_g'], p['ln1_b'])
    x2, x2b = _ffn_ln(x1b, p['w_gate_up'].astype(BF16), p['w_down'].astype(BF16), x1, p['ln2_g'], p['ln2_b'],
                      want_bf16=not last)
    return x2.reshape(b, s, d), None if last else x2b.reshape(b, s, d)


def kernel(x, w_in, w_out, pool_w, pool_scale, conv_w, conv_b, conv_ln_g, conv_ln_b, conv_pw_w, conv_pw_b,
           cmp_pos_k, cmp_pos_v, cmp_k_w1, cmp_k_w2, cmp_v_w1, cmp_v_w2, ln1_g, ln1_b, ln2_g, ln2_b,
           w_gate_up, w_down):
    params = dict(w_in=w_in, w_out=w_out, pool_w=pool_w, pool_scale=pool_scale, conv_w=conv_w, conv_b=conv_b,
                  conv_ln_g=conv_ln_g, conv_ln_b=conv_ln_b, conv_pw_w=conv_pw_w, conv_pw_b=conv_pw_b,
                  cmp_pos_k=cmp_pos_k, cmp_pos_v=cmp_pos_v, cmp_k_w1=cmp_k_w1, cmp_k_w2=cmp_k_w2,
                  cmp_v_w1=cmp_v_w1, cmp_v_w2=cmp_v_w2, ln1_g=ln1_g, ln1_b=ln1_b, ln2_g=ln2_g, ln2_b=ln2_b,
                  w_gate_up=w_gate_up, w_down=w_down)
    xb = None
    for l in range(w_in.shape[0]):
        x, xb = _layer(x, xb, {k: v[l] for k, v in params.items()}, last=l == w_in.shape[0] - 1)
    return x
```

```python
import functools

import numpy as np
import jax
import jax.numpy as jnp
from jax import lax
from jax.experimental import pallas as pl
from jax.experimental.pallas import tpu as pltpu

F32 = jnp.float32
BF16 = jnp.bfloat16
I32 = jnp.int32

D_MODEL = 2048
DEPTH = 2
W_MIX = D_MODEL // 4
HEAD_DIM = 128
N_HEADS = W_MIX // HEAD_DIM
POOL_WINDOWS = (2, 4, 8, 16)
POOL_GROUP = W_MIX // len(POOL_WINDOWS)
CONV_WIDTH = 31
IDX_HEADS = 8
IDX_DIM = 64
DSA_TOPK_MAX = 256
CMP_BLOCK = 32
CMP_STRIDE = 16
SEL_BLOCK = 64
SEL_TOPK = 16
NSA_WINDOW = 512
D_FF = ((8 * D_MODEL + 3 * 256 - 1) // (3 * 256)) * 256
DN_ALPHA = (2 * DEPTH) ** 0.25
ATTN_SCALE = HEAD_DIM ** -0.5
IDX_SCALE = (IDX_HEADS * IDX_DIM) ** -0.5
LN_EPS = 1e-5
LOG2E = 1.4426950408889634

LANES = 128
SUBLANES = 8
VMEM_LIMIT = 48 * 1024 * 1024

NEG = -1e30
INT_MIN = -2 ** 31

_SRC_SIZES = (W_MIX, W_MIX, W_MIX, W_MIX, HEAD_DIM, HEAD_DIM, IDX_HEADS * IDX_DIM, IDX_DIM, IDX_HEADS,
              W_MIX, HEAD_DIM, HEAD_DIM, HEAD_DIM, HEAD_DIM, HEAD_DIM, HEAD_DIM, N_HEADS * 3)
_SRC_NAMES = ('pool', 'c_a', 'c_g', 'd_q', 'd_k', 'd_v', 'i_q', 'i_k', 'i_w',
              'n_q', 'n_ck', 'n_cv', 'n_sk', 'n_sv', 'n_wk', 'n_wv', 'n_g')
_SRC_OFF = dict(zip(_SRC_NAMES, np.concatenate([[0], np.cumsum(_SRC_SIZES)[:-1]]).tolist()))
_SRC_SIZE = dict(zip(_SRC_NAMES, _SRC_SIZES))
_PACK_ORDER = ('pool', 'c_a', 'c_g', 'd_q', 'i_q', 'n_q', 'd_k', 'd_v',
               'n_ck', 'n_cv', 'n_sk', 'n_sv', 'n_wk', 'n_wv', 'i_k', 'i_w', 'n_g')
_PACK_OFF = {}
_o = 0
for _n in _PACK_ORDER:
    _PACK_OFF[_n] = _o
    _o += _SRC_SIZE[_n]
D_PACK = ((_o + LANES - 1) // LANES) * LANES
SMALL_OFF = _PACK_OFF['i_k']
IK_LANE = 0
IW_LANE = _PACK_OFF['i_w'] - SMALL_OFF
NG_LANE = _PACK_OFF['n_g'] - SMALL_OFF


def _alibi_slopes():
    n = 2 * N_HEADS
    s = np.power(2.0, -8.0 * np.arange(1, n + 1) / n).astype(np.float32)
    return [float(v) for v in s[0::2]], [float(v) for v in s[1::2]]


SLOPES_DSA, SLOPES_NSA = _alibi_slopes()


def _cparams(sem):
    return pltpu.CompilerParams(dimension_semantics=sem, vmem_limit_bytes=VMEM_LIMIT)


def _resident(shape):
    nd = len(shape)
    return pl.BlockSpec(shape, lambda *_: (0,) * nd, pipeline_mode=pl.Buffered(1))


def _of_layer(arr, l):
    nd = arr.ndim
    return pl.BlockSpec((None,) + arr.shape[1:], lambda *_: (l,) + (0,) * (nd - 1),
                        pipeline_mode=pl.Buffered(1))


def _dot(a, b):
    return jnp.dot(a, b, preferred_element_type=F32)


def _dot_nt(a, b):
    return lax.dot_general(a, b, (((1,), (1,)), ((), ())), preferred_element_type=F32)


def _layernorm(x, g, b):
    mu = jnp.mean(x, axis=-1, keepdims=True)
    xc = x - mu
    var = jnp.mean(xc * xc, axis=-1, keepdims=True)
    return xc * lax.rsqrt(var + LN_EPS) * g + b


INPROJ_TM = 512
INPROJ_CHUNK = 512


def _inproj_kernel(x_ref, w_ref, o_ref, ck_ref, cv_ref):
    x = x_ref[...].astype(BF16)
    n = o_ref.shape[1]
    for c0 in range(0, n, INPROJ_CHUNK):
        c1 = min(c0 + INPROJ_CHUNK, n)
        o_ref[:, c0:c1] = _dot(x, w_ref[:, c0:c1]).astype(BF16)
    ck_ref[...] = o_ref[:, _PACK_OFF['n_ck']:_PACK_OFF['n_ck'] + HEAD_DIM]
    cv_ref[...] = o_ref[:, _PACK_OFF['n_cv']:_PACK_OFF['n_cv'] + HEAD_DIM]


def _inproj(x, w_pack, l):
    n_tok, d = x.shape
    tm = INPROJ_TM
    col = pl.BlockSpec((tm, HEAD_DIM), lambda i: (i, 0))
    return pl.pallas_call(
        _inproj_kernel,
        out_shape=(jax.ShapeDtypeStruct((n_tok, D_PACK), BF16),
                   jax.ShapeDtypeStruct((n_tok, HEAD_DIM), BF16), jax.ShapeDtypeStruct((n_tok, HEAD_DIM), BF16)),
        grid=(n_tok // tm,),
        in_specs=[pl.BlockSpec((tm, d), lambda i: (i, 0)), _of_layer(w_pack, l)],
        out_specs=(pl.BlockSpec((tm, D_PACK), lambda i: (i, 0)), col, col),
        compiler_params=_cparams(("parallel",)),
        name="inproj",
    )(x, w_pack)


POOL_TS = 512
POOL_HALO = 16


def _pool_kernel(u_ref, halo_ref, w_ref, sc_ref, o_ref, xs_ref):
    i = pl.program_id(1)
    ts = u_ref.shape[1]
    xs_ref[POOL_HALO:POOL_HALO + ts, :] = u_ref[0].astype(F32)
    xs_ref[0:POOL_HALO, :] = jnp.where(i > 0, halo_ref[0].astype(F32), 0.0)
    pos = i * ts + lax.broadcasted_iota(I32, (ts, 1), 0)
    for g, win in enumerate(POOL_WINDOWS):
        c = slice(g * POOL_GROUP, (g + 1) * POOL_GROUP)
        x = xs_ref[POOL_HALO:POOL_HALO + ts, c]
        acc = x
        for k in range(1, win):
            acc = acc + xs_ref[POOL_HALO - k:POOL_HALO - k + ts, c]
        cnt = jnp.minimum(pos + 1, win).astype(F32)
        d = acc / cnt - x
        y = _dot(d.astype(BF16), w_ref[g])
        o_ref[0, :, c] = (y * sc_ref[:, c]).astype(BF16)


def _pool_mixer(proj, pool_w, pool_scale, l):
    b, s, _ = proj.shape
    ts = min(POOL_TS, s)
    hb = ts // POOL_HALO
    blk = _PACK_OFF['pool'] // W_MIX
    return pl.pallas_call(
        _pool_kernel,
        out_shape=jax.ShapeDtypeStruct((b, s, W_MIX), BF16),
        grid=(b, s // ts),
        in_specs=[
            pl.BlockSpec((1, ts, W_MIX), lambda bi, i: (bi, i, blk)),
            pl.BlockSpec((1, POOL_HALO, W_MIX), lambda bi, i: (bi, jnp.maximum(i * hb - 1, 0), blk)),
            _of_layer(pool_w, l),
            _of_layer(pool_scale, l),
        ],
        out_specs=pl.BlockSpec((1, ts, W_MIX), lambda bi, i: (bi, i, 0)),
        scratch_shapes=[pltpu.VMEM((ts + POOL_HALO, W_MIX), F32)],
        compiler_params=_cparams(("parallel", "parallel")),
        name="pool_mixer",
    )(proj, proj, pool_w, pool_scale)


CONV_TS = 512
CONV_HALO = 32
CONV_ROWS = 32


def _conv_kernel(a_ref, g_ref, ha_ref, hg_ref, cw_ref, cb_ref, lg_ref, lb_ref, pw_ref, pb_ref,
                 o_ref, hs_ref, sh_ref, y_ref):
    i = pl.program_id(1)
    ts = a_ref.shape[1]
    hs_ref[CONV_HALO:CONV_HALO + ts, :] = a_ref[0].astype(F32) * jax.nn.sigmoid(g_ref[0].astype(F32))
    halo = ha_ref[0].astype(F32) * jax.nn.sigmoid(hg_ref[0].astype(F32))
    hs_ref[0:CONV_HALO, :] = jnp.where(i > 0, halo, 0.0)
    n_sh = ts + CONV_HALO - SUBLANES
    for b in range(1, SUBLANES):
        sh_ref[b - 1, 0:n_sh, :] = hs_ref[b:b + n_sh, :]
    base = CONV_HALO - (CONV_WIDTH - 1)
    for r0 in range(0, ts, CONV_ROWS):
        acc = jnp.broadcast_to(cb_ref[...], (CONV_ROWS, W_MIX))
        for j in range(CONV_WIDTH):
            a8, b = divmod(base + j, SUBLANES)
            r = r0 + a8 * SUBLANES
            src = hs_ref[r:r + CONV_ROWS, :] if b == 0 else sh_ref[b - 1, r:r + CONV_ROWS, :]
            acc = acc + src * cw_ref[j:j + 1, :]
        y = _layernorm(acc, lg_ref[...], lb_ref[...])
        y_ref[r0:r0 + CONV_ROWS, :] = (y * jax.nn.sigmoid(y)).astype(BF16)
    o_ref[0] = (_dot(y_ref[...], pw_ref[...]) + pb_ref[...]).astype(BF16)


def _conv_mixer(proj, cw, conv_b, ln_g, ln_b, pw_w, pw_b, l):
    b, s, _ = proj.shape
    ts = min(CONV_TS, s)
    hb = ts // CONV_HALO
    ba = _PACK_OFF['c_a'] // W_MIX
    bg = _PACK_OFF['c_g'] // W_MIX
    halo_map = lambda blk: (lambda bi, i: (bi, jnp.maximum(i * hb - 1, 0), blk))
    return pl.pallas_call(
        _conv_kernel,
        out_shape=jax.ShapeDtypeStruct((b, s, W_MIX), BF16),
        grid=(b, s // ts),
        in_specs=[
            pl.BlockSpec((1, ts, W_MIX), lambda bi, i: (bi, i, ba)),
            pl.BlockSpec((1, ts, W_MIX), lambda bi, i: (bi, i, bg)),
            pl.BlockSpec((1, CONV_HALO, W_MIX), halo_map(ba)),
            pl.BlockSpec((1, CONV_HALO, W_MIX), halo_map(bg)),
            _of_layer(cw, l), _of_layer(conv_b, l), _of_layer(ln_g, l), _of_layer(ln_b, l),
            _of_layer(pw_w, l), _of_layer(pw_b, l),
        ],
        out_specs=pl.BlockSpec((1, ts, W_MIX), lambda bi, i: (bi, i, 0)),
        scratch_shapes=[pltpu.VMEM((ts + CONV_HALO, W_MIX), F32),
                        pltpu.VMEM((SUBLANES - 1, ts + CONV_HALO - SUBLANES, W_MIX), F32),
                        pltpu.VMEM((ts, W_MIX), BF16)],
        compiler_params=_cparams(("parallel", "parallel")),
        name="conv_mixer",
    )(proj, proj, proj, proj, cw, conv_b, ln_g, ln_b, pw_w, pw_b)


def _compress_kernel(r_ref, pos_ref, w1_ref, w2_ref, o_ref):
    r = r_ref[0]
    half = r.shape[1]
    n = r.shape[0]
    top = _dot(r, w1_ref[0:half, :])
    bot = _dot(r, w1_ref[half:2 * half, :])
    bot_next = pltpu.roll(bot, n - 1, 0)
    posb = _dot(jnp.broadcast_to(pos_ref[...], (8, 2 * half)), w1_ref[...])[0:1, :]
    h = jax.nn.gelu(top + bot_next + posb)
    out = _dot(h.astype(BF16), w2_ref[...])
    row = lax.broadcasted_iota(I32, out.shape, 0)
    o_ref[0] = jnp.where(row < n - 1, out, 0.0).astype(BF16)


def _compress(raw, pos, w1, w2, l):
    b, s, d = raw.shape
    n = s // CMP_STRIDE
    r = raw.reshape(b, n, CMP_STRIDE * d)
    return pl.pallas_call(
        _compress_kernel,
        out_shape=jax.ShapeDtypeStruct((b, n, d), BF16),
        grid=(b,),
        in_specs=[pl.BlockSpec((1, n, CMP_STRIDE * d), lambda bi: (bi, 0, 0)),
                  _of_layer(pos, l), _of_layer(w1, l), _of_layer(w2, l)],
        out_specs=pl.BlockSpec((1, n, d), lambda bi: (bi, 0, 0)),
        compiler_params=_cparams(("parallel",)),
        name="nsa_compress",
    )(r, pos, w1, w2)


ATT_TQ = 256
ATT_TK = 512
POS_RADIX = 64


def _pos_features(s_len):
    assert s_len <= POS_RADIX * 256
    s = np.arange(s_len)
    f = np.zeros((s_len, LANES), np.float32)
    f[:, 0] = f[:, 1] = s // POS_RADIX
    f[:, 2] = f[:, 3] = s % POS_RADIX
    return jnp.asarray(f, BF16)


def _slope_features(slopes, tq):
    f = np.zeros((len(slopes) * tq, LANES), np.float32)
    for h, sl in enumerate(slopes):
        c = np.float32(sl * LOG2E)
        ca = np.float32(np.asarray(c, dtype=BF16))
        cb = np.float32(np.asarray(c - ca, dtype=BF16))
        f[h * tq:(h + 1) * tq, 0:4] = [POS_RADIX * ca, POS_RADIX * cb, ca, cb]
    return jnp.asarray(f, BF16)


def _ones_feature(n):
    f = np.zeros((n, LANES), np.float32)
    f[:, 0] = 1.0
    return jnp.asarray(f, BF16)


def _stack_queries(q_ref, qf_ref, q4_ref):
    tq = q_ref.shape[1]
    for h in range(N_HEADS):
        q4_ref[h * tq:(h + 1) * tq, 0:HEAD_DIM] = q_ref[0, :, h * HEAD_DIM:(h + 1) * HEAD_DIM]
    q4_ref[:, HEAD_DIM:2 * HEAD_DIM] = qf_ref[...]


def _flash_attention(q4_ref, acc_ref, p_ref, a_ref, n_kt, keys_fn, vals_fn, tq):
    acc_ref[...] = jnp.zeros(acc_ref.shape, F32)
    p_ref[...] = jnp.zeros(p_ref.shape, BF16)
    heads = [slice(h * tq, (h + 1) * tq) for h in range(N_HEADS)]

    def logits(rows, kk, mask):
        return jnp.where(mask, _dot_nt(q4_ref[rows, :], kk), NEG)

    def step(kt, ms, with_next):
        vv = vals_fn(jnp.maximum(kt - 1, 0))
        nxt = keys_fn(kt + 1) if with_next else None
        new_ms = []
        for h, rows in enumerate(heads):
            ah = a_ref[rows, :]
            pv = _dot(p_ref[rows, :], vv)
            if with_next:
                a_ref[rows, :] = logits(rows, *nxt)
            m_new = jnp.maximum(ms[h], jnp.max(ah, axis=-1, keepdims=True))
            p_ref[rows, :] = jnp.exp2(ah - m_new).astype(BF16)
            acc_ref[rows, :] = jnp.exp2(ms[h] - m_new) * (acc_ref[rows, :] + pv)
            new_ms.append(m_new)
        return tuple(new_ms)

    first = keys_fn(0)
    for rows in heads:
        a_ref[rows, :] = logits(rows, *first)
    m0 = tuple(jnp.full((tq, 1), NEG, F32) for _ in range(N_HEADS))
    ms = lax.fori_loop(0, n_kt - 1, lambda kt, ms: step(kt, ms, True), m0)
    step(n_kt - 1, ms, False)
    vv = vals_fn(n_kt - 1)
    for rows in heads:
        acc_ref[rows, :] += _dot(p_ref[rows, :], vv)


def _softmax2_rows(a, mask):
    a = jnp.where(mask, a, NEG)
    m = jnp.max(a, axis=-1, keepdims=True)
    e = jnp.where(mask, jnp.exp2(a - m), 0.0)
    s = jnp.sum(e, axis=-1, keepdims=True)
    return e / jnp.maximum(s, 1e-30)


def _split3(x):
    hi = x.astype(BF16)
    r1 = x - hi.astype(F32)
    mid = r1.astype(BF16)
    lo = (r1 - mid.astype(F32)).astype(BF16)
    return hi, mid, lo


def _nsa_kernel(q_ref, sm_ref, kc_ref, vc_ref, sk_ref, sv_ref, wk_ref, wv_ref, pf_ref, qf_ref, vf_ref,
                o_ref, q4_ref, p4_ref, acc_ref, selm_ref, p_ref, a_ref):
    tq = q_ref.shape[1]
    s_len = sk_ref.shape[1]
    n_cmp = kc_ref.shape[1]
    n_sel = s_len // SEL_BLOCK
    n_top = min(SEL_TOPK, n_sel)
    t0 = pl.program_id(1) * tq
    _stack_queries(q_ref, qf_ref, q4_ref)
    t_col = t0 + lax.broadcasted_iota(I32, (tq, 1), 0)

    a_all = _dot_nt(q4_ref[:, 0:HEAD_DIM], kc_ref[0])
    c_idx = lax.broadcasted_iota(I32, (1, n_cmp), 1)
    cd = t_col - (c_idx * CMP_STRIDE + (CMP_BLOCK - 1))
    cmask = (cd >= 0) & (c_idx < n_cmp - 1)
    cdf = cd.astype(F32)
    p_sum = jnp.zeros((tq, n_cmp), F32)
    for h in range(N_HEADS):
        p = _softmax2_rows(a_all[h * tq:(h + 1) * tq] - (SLOPES_NSA[h] * LOG2E) * cdf, cmask)
        p_sum = p_sum + p
        p4_ref[h * tq:(h + 1) * tq, 0:n_cmp] = p.astype(BF16)
    o_cmp = _dot(p4_ref[:, 0:n_cmp], vc_ref[0])

    n_selp = selm_ref.shape[0]
    jj = lax.broadcasted_iota(I32, (n_selp, n_cmp), 0)
    cc = lax.broadcasted_iota(I32, (n_selp, n_cmp), 1)
    c_start = cc * CMP_STRIDE
    overlap = ((c_start < (jj + 1) * SEL_BLOCK) & (c_start + (CMP_BLOCK - 1) >= jj * SEL_BLOCK)
               & (cc < n_cmp - 1))
    ov = jnp.where(overlap, 1.0, 0.0).astype(BF16)
    hi, mid, lo = _split3(p_sum)
    imp = _dot_nt(ov, hi) + _dot_nt(ov, mid) + _dot_nt(ov, lo)
    j_col = lax.broadcasted_iota(I32, (n_selp, 1), 0)
    t_blk = (t0 + lax.broadcasted_iota(I32, (1, tq), 1)) // SEL_BLOCK
    forced = (j_col == 0) | (j_col == t_blk) | (j_col == t_blk - 1)
    imp = jnp.where(forced, jnp.inf, imp)
    imp = jnp.where(j_col <= t_blk, imp, -jnp.inf)
    rank = jnp.zeros((n_selp, tq), F32)
    for i2 in range(n_sel):
        ci = imp[i2:i2 + 1, :]
        tie_first = jnp.where(j_col > i2, 1.0, 0.0)
        rank = rank + jnp.where(ci > imp, 1.0, jnp.where(ci == imp, tie_first, 0.0))
    selm_ref[...] = jnp.where((rank < n_top) & (j_col < n_sel), 1.0, 0.0)
    selm = selm_ref[...].T.astype(BF16)

    tk = min(ATT_TK, s_len)
    n_kt = (t0 + tq - 1) // tk + 1
    e_row = lax.broadcasted_iota(I32, (n_selp, tk), 0)
    e_col = lax.broadcasted_iota(I32, (n_selp, tk), 1)

    def sel_keys(kt):
        s0 = pl.multiple_of(kt * tk, tk)
        kk = jnp.concatenate([sk_ref[0, pl.ds(s0, tk), :], pf_ref[pl.ds(s0, tk), :]], axis=1)
        s_pos = s0 + lax.broadcasted_iota(I32, (1, tk), 1)
        expand = jnp.where(e_row == (s0 + e_col) // SEL_BLOCK, 1.0, 0.0).astype(BF16)
        return kk, (s_pos <= t_col) & (_dot(selm, expand) > 0.5)

    def sel_vals(kt):
        s0 = pl.multiple_of(kt * tk, tk)
        return jnp.concatenate([sv_ref[0, pl.ds(s0, tk), :], vf_ref[0:tk, :]], axis=1)

    _flash_attention(q4_ref, acc_ref, p_ref, a_ref, n_kt, sel_keys, sel_vals, tq)

    wlen = min(NSA_WINDOW + tq, s_len)
    ks = pl.multiple_of(jnp.maximum(t0 + tq - wlen, 0), LANES)
    kw = jnp.concatenate([wk_ref[0, pl.ds(ks, wlen), :], pf_ref[pl.ds(ks, wlen), :]], axis=1)
    vw = jnp.concatenate([wv_ref[0, pl.ds(ks, wlen), :], vf_ref[0:wlen, :]], axis=1)
    wd = t_col - (ks + lax.broadcasted_iota(I32, (1, wlen), 1))
    wmask = (wd >= 0) & (wd < NSA_WINDOW)
    o_win = []
    for h in range(N_HEADS):
        ah = jnp.where(wmask, _dot_nt(q4_ref[h * tq:(h + 1) * tq, :], kw), NEG)
        m = jnp.max(ah, axis=-1, keepdims=True)
        o_win.append(_dot(jnp.exp2(ah - m).astype(BF16), vw))

    gates = jax.nn.sigmoid(sm_ref[0].astype(F32))
    for h in range(N_HEADS):
        rows = slice(h * tq, (h + 1) * tq)
        o_slc = acc_ref[rows, 0:HEAD_DIM] / jnp.maximum(acc_ref[rows, HEAD_DIM:HEAD_DIM + 1], 1e-30)
        o_w = o_win[h][:, 0:HEAD_DIM] / jnp.maximum(o_win[h][:, HEAD_DIM:HEAD_DIM + 1], 1e-30)
        g0 = gates[:, NG_LANE + 3 * h:NG_LANE + 3 * h + 1]
        g1 = gates[:, NG_LANE + 3 * h + 1:NG_LANE + 3 * h + 2]
        g2 = gates[:, NG_LANE + 3 * h + 2:NG_LANE + 3 * h + 3]
        o = g0 * o_cmp[rows] + g1 * o_slc + g2 * o_w
        o_ref[0, :, h * HEAD_DIM:(h + 1) * HEAD_DIM] = o.astype(BF16)


def _nsa_mixer(proj, k_cmp, v_cmp):
    b, s, _ = proj.shape
    tq = min(ATT_TQ, s)
    tk = min(ATT_TK, s)
    n_cmp = k_cmp.shape[1]
    wlen = min(NSA_WINDOW + tq, s)
    col = lambda name: _PACK_OFF[name] // HEAD_DIM
    full = lambda name: pl.BlockSpec((1, s, HEAD_DIM), functools.partial(lambda c, bi, i: (bi, 0, c), col(name)))
    nv = max(tk, wlen)
    return pl.pallas_call(
        _nsa_kernel,
        out_shape=jax.ShapeDtypeStruct((b, s, W_MIX), BF16),
        grid=(b, s // tq),
        in_specs=[
            pl.BlockSpec((1, tq, W_MIX), lambda bi, i: (bi, i, _PACK_OFF['n_q'] // W_MIX)),
            pl.BlockSpec((1, tq, LANES), lambda bi, i: (bi, i, SMALL_OFF // LANES)),
            pl.BlockSpec((1, n_cmp, HEAD_DIM), lambda bi, i: (bi, 0, 0)),
            pl.BlockSpec((1, n_cmp, HEAD_DIM), lambda bi, i: (bi, 0, 0)),
            full('n_sk'), full('n_sv'), full('n_wk'), full('n_wv'),
            _resident((s, LANES)), _resident((N_HEADS * tq, LANES)), _resident((nv, LANES)),
        ],
        out_specs=pl.BlockSpec((1, tq, W_MIX), lambda bi, i: (bi, i, 0)),
        scratch_shapes=[pltpu.VMEM((N_HEADS * tq, 2 * HEAD_DIM), BF16),
                        pltpu.VMEM((N_HEADS * tq, n_cmp), BF16),
                        pltpu.VMEM((N_HEADS * tq, 2 * HEAD_DIM), F32),
                        pltpu.VMEM((((s // SEL_BLOCK + LANES - 1) // LANES) * LANES, tq), F32),
                        pltpu.VMEM((N_HEADS * tq, tk), BF16),
                        pltpu.VMEM((N_HEADS * tq, tk), F32)],
        compiler_params=_cparams(("parallel", "arbitrary")),
        name="nsa_attention",
    )(proj, proj, k_cmp, v_cmp, proj, proj, proj, proj,
      _pos_features(s), _slope_features(SLOPES_NSA, tq), _ones_feature(nv))


I16 = jnp.int16
I16_MIN = -2 ** 15
SEL_ROWS = 64
SCORE_CHAINS = 4


def _dsa_kernel(q_ref, iq_ref, sm_ref, smf_ref, k_ref, v_ref, pf_ref, qf_ref, vf_ref, o_ref,
                q4_ref, acc_ref, hi_ref, lo_ref, lq_ref, selb_ref, p_ref, a_ref):
    tq = q_ref.shape[1]
    s_len = k_ref.shape[1]
    topk = min(DSA_TOPK_MAX, s_len // 4)
    tk = min(ATT_TK, s_len)
    n_ch = tk // SEL_ROWS
    t0 = pl.program_id(1) * tq
    n_kt = (t0 + tq - 1) // tk + 1
    _stack_queries(q_ref, qf_ref, q4_ref)
    t_row = t0 + lax.broadcasted_iota(I32, (1, tq), 1)
    iq = iq_ref[0]
    iw_t = sm_ref[0].astype(F32).T
    one = jnp.ones((), BF16)
    zero = jnp.zeros((), BF16)

    def score_body(kt, _):
        tc = tk // SCORE_CHAINS
        for c in range(SCORE_CHAINS):
            s0 = pl.multiple_of(kt * tk + c * tc, tc)
            ik = smf_ref[0, pl.ds(s0, tc), IK_LANE:IK_LANE + IDX_DIM]
            sc = jnp.zeros((tc, tq), F32)
            for h in range(IDX_HEADS):
                lg = _dot_nt(ik, iq[:, h * IDX_DIM:(h + 1) * IDX_DIM])
                sc = sc + jnp.maximum(lg, 0.0) * iw_t[IW_LANE + h:IW_LANE + h + 1, :]
            sc = jnp.where(sc == 0.0, 0.0, sc)
            bits = lax.bitcast_convert_type(sc, I32)
            key = bits ^ ((bits >> 31) & 0x7FFFFFFF)
            s_pos = s0 + lax.broadcasted_iota(I32, (tc, 1), 0)
            key = jnp.where(s_pos <= t_row, key, INT_MIN)
            hi_ref[pl.ds(s0, tc), :] = (key >> 16).astype(I16)
            lo_ref[pl.ds(s0, tc), :] = ((key & 0xFFFF) + I16_MIN).astype(I16)
        return 0

    lax.fori_loop(0, n_kt, score_body, 0)

    def count_ge(ref, thr_row):
        thr = jnp.broadcast_to(thr_row, (SEL_ROWS, tq))

        def body(kt, cnt):
            s0 = pl.multiple_of(kt * tk, tk)
            for c in range(n_ch):
                cnt = cnt + jnp.where(ref[pl.ds(s0 + c * SEL_ROWS, SEL_ROWS), :] >= thr, one, zero)
            return cnt
        cnt = lax.fori_loop(0, n_kt, body, jnp.zeros((SEL_ROWS, tq), BF16))
        return jnp.sum(cnt.astype(F32), axis=0, keepdims=True)

    def kth_largest(ref, k):
        def bit_body(it, cand):
            trial = cand | jnp.left_shift(jnp.int32(1), 15 - it)
            total = count_ge(ref, (trial + I16_MIN).astype(I16))
            return jnp.where(total >= k, trial, cand)
        return lax.fori_loop(0, 16, bit_body, jnp.zeros((1, tq), I32))

    cand_hi = jnp.maximum(kth_largest(hi_ref, float(topk)), 1)
    p16 = jnp.broadcast_to((cand_hi + I16_MIN).astype(I16), (SEL_ROWS, tq))
    n_above = count_ge(hi_ref, (jnp.minimum(cand_hi + 1, 2 ** 16 - 1) + I16_MIN).astype(I16))
    need = float(topk) - n_above

    def tie_body(kt, _):
        s0 = pl.multiple_of(kt * tk, tk)
        for c in range(n_ch):
            ds = pl.ds(s0 + c * SEL_ROWS, SEL_ROWS)
            lq_ref[ds, :] = jnp.where(hi_ref[ds, :] == p16, lo_ref[ds, :], jnp.full((), I16_MIN, I16))
        return 0

    lax.fori_loop(0, n_kt, tie_body, 0)
    q16 = jnp.broadcast_to((kth_largest(lq_ref, need) + I16_MIN).astype(I16), (SEL_ROWS, tq))

    def sel_body(kt, _):
        s0 = pl.multiple_of(kt * tk, tk)
        for c in range(n_ch):
            ds = pl.ds(s0 + c * SEL_ROWS, SEL_ROWS)
            hi = hi_ref[ds, :]
            tie = jnp.where(hi == p16, jnp.where(lo_ref[ds, :] >= q16, one, zero), zero)
            selb_ref[ds, :] = jnp.where(hi > p16, one, tie)
        return 0

    lax.fori_loop(0, n_kt, sel_body, 0)

    def att_keys(kt):
        s0 = pl.multiple_of(kt * tk, tk)
        kk = jnp.concatenate([k_ref[0, pl.ds(s0, tk), :], pf_ref[pl.ds(s0, tk), :]], axis=1)
        return kk, selb_ref[pl.ds(s0, tk), :].astype(F32).T > 0.5

    def att_vals(kt):
        s0 = pl.multiple_of(kt * tk, tk)
        return jnp.concatenate([v_ref[0, pl.ds(s0, tk), :], vf_ref[...]], axis=1)

    _flash_attention(q4_ref, acc_ref, p_ref, a_ref, n_kt, att_keys, att_vals, tq)
    for h in range(N_HEADS):
        rows = slice(h * tq, (h + 1) * tq)
        o = acc_ref[rows, 0:HEAD_DIM] / jnp.maximum(acc_ref[rows, HEAD_DIM:HEAD_DIM + 1], 1e-30)
        o_ref[0, :, h * HEAD_DIM:(h + 1) * HEAD_DIM] = o.astype(BF16)


def _dsa_mixer(proj):
    b, s, _ = proj.shape
    tq = min(ATT_TQ, s)
    tk = min(ATT_TK, s)
    full = lambda off: pl.BlockSpec((1, s, LANES), functools.partial(lambda c, bi, i: (bi, 0, c), off // LANES))
    return pl.pallas_call(
        _dsa_kernel,
        out_shape=jax.ShapeDtypeStruct((b, s, W_MIX), BF16),
        grid=(b, s // tq),
        in_specs=[
            pl.BlockSpec((1, tq, W_MIX), lambda bi, i: (bi, i, _PACK_OFF['d_q'] // W_MIX)),
            pl.BlockSpec((1, tq, W_MIX), lambda bi, i: (bi, i, _PACK_OFF['i_q'] // W_MIX)),
            pl.BlockSpec((1, tq, LANES), lambda bi, i: (bi, i, SMALL_OFF // LANES)),
            full(SMALL_OFF), full(_PACK_OFF['d_k']), full(_PACK_OFF['d_v']),
            _resident((s, LANES)), _resident((N_HEADS * tq, LANES)), _resident((tk, LANES)),
        ],
        out_specs=pl.BlockSpec((1, tq, W_MIX), lambda bi, i: (bi, i, 0)),
        scratch_shapes=[pltpu.VMEM((N_HEADS * tq, 2 * HEAD_DIM), BF16),
                        pltpu.VMEM((N_HEADS * tq, 2 * HEAD_DIM), F32),
                        pltpu.VMEM((s, tq), I16), pltpu.VMEM((s, tq), I16), pltpu.VMEM((s, tq), I16),
                        pltpu.VMEM((s, tq), BF16),
                        pltpu.VMEM((N_HEADS * tq, tk), BF16),
                        pltpu.VMEM((N_HEADS * tq, tk), F32)],
        compiler_params=_cparams(("parallel", "arbitrary")),
        name="dsa_attention",
    )(proj, proj, proj, proj, proj, proj,
      _pos_features(s), _slope_features(SLOPES_DSA, tq), _ones_feature(tk))


OUTPROJ_TM = 512
LN_ROWS = 256


def _outproj_kernel(y0_ref, y1_ref, y2_ref, y3_ref, w_ref, x_ref, g_ref, b_ref, o_ref, ob_ref):
    for r0 in range(0, x_ref.shape[0], LN_ROWS):
        rows = slice(r0, r0 + LN_ROWS)
        acc = DN_ALPHA * x_ref[rows, :]
        for gi, y_ref in enumerate((y0_ref, y1_ref, y2_ref, y3_ref)):
            acc = acc + _dot(y_ref[rows, :], w_ref[gi * W_MIX:(gi + 1) * W_MIX, :])
        y = _layernorm(acc, g_ref[...], b_ref[...])
        o_ref[rows, :] = y
        ob_ref[rows, :] = y.astype(BF16)


def _outproj_ln(ys, w_out, x, g, b, l):
    n_tok, d = x.shape
    tm = OUTPROJ_TM
    yspec = pl.BlockSpec((tm, W_MIX), lambda i: (i, 0))
    xspec = pl.BlockSpec((tm, d), lambda i: (i, 0))
    return pl.pallas_call(
        _outproj_kernel,
        out_shape=(jax.ShapeDtypeStruct((n_tok, d), F32), jax.ShapeDtypeStruct((n_tok, d), BF16)),
        grid=(n_tok // tm,),
        in_specs=[yspec, yspec, yspec, yspec, _of_layer(w_out, l), xspec, _of_layer(g, l), _of_layer(b, l)],
        out_specs=(xspec, xspec),
        compiler_params=_cparams(("parallel",)),
        name="outproj_ln",
    )(*ys, w_out, x, g, b)


FFN_UP_TM = 1024
FFN_TF = 512
FFN_DOWN_TM = 256


def _ffn_up_kernel(xb_ref, wg_ref, wu_ref, h_ref):
    xb = xb_ref[...]
    gate = _dot(xb, wg_ref[...])
    up = _dot(xb, wu_ref[...])
    h_ref[...] = (gate * jax.nn.sigmoid(gate) * up).astype(BF16)


def _ffn_down_kernel(h_ref, wd_ref, x_ref, g_ref, b_ref, o_ref, ob_ref=None):
    y = _layernorm(DN_ALPHA * x_ref[...] + _dot(h_ref[...], wd_ref[...]), g_ref[...], b_ref[...])
    o_ref[...] = y
    if ob_ref is not None:
        ob_ref[...] = y.astype(BF16)


def _ffn_ln(xb, w_gate_up, w_down, x, g, b, l, want_bf16):
    n_tok, d = x.shape
    d_ff = w_down.shape[1]
    tm, tf = FFN_UP_TM, FFN_TF
    nf = d_ff // tf
    h = pl.pallas_call(
        _ffn_up_kernel,
        out_shape=jax.ShapeDtypeStruct((n_tok, d_ff), BF16),
        grid=(nf, n_tok // tm),
        in_specs=[
            pl.BlockSpec((tm, d), lambda f, i: (i, 0)),
            pl.BlockSpec((None, d, tf), lambda f, i: (l, 0, f)),
            pl.BlockSpec((None, d, tf), lambda f, i: (l, 0, f + nf)),
        ],
        out_specs=pl.BlockSpec((tm, tf), lambda f, i: (i, f)),
        compiler_params=_cparams(("parallel", "parallel")),
        name="ffn_up",
    )(xb, w_gate_up, w_gate_up)
    tm = FFN_DOWN_TM
    xspec = pl.BlockSpec((tm, d), lambda i: (i, 0))
    out_dtypes = (F32, BF16) if want_bf16 else (F32,)
    outs = pl.pallas_call(
        _ffn_down_kernel,
        out_shape=tuple(jax.ShapeDtypeStruct((n_tok, d), t) for t in out_dtypes),
        grid=(n_tok // tm,),
        in_specs=[pl.BlockSpec((tm, d_ff), lambda i: (i, 0)), _of_layer(w_down, l),
                  xspec, _of_layer(g, l), _of_layer(b, l)],
        out_specs=tuple(xspec for _ in out_dtypes),
        compiler_params=_cparams(("parallel",)),
        name="ffn_down_ln",
    )(h, w_down, x, g, b)
    return outs if want_bf16 else (outs[0], None)


def _pack_w_in(w):
    fold = {'d_q': ATTN_SCALE * LOG2E, 'n_q': ATTN_SCALE * LOG2E, 'i_w': IDX_SCALE}
    cols = [w[..., _SRC_OFF[n]:_SRC_OFF[n] + _SRC_SIZE[n]] * fold.get(n, 1.0) for n in _PACK_ORDER]
    used = sum(_SRC_SIZE[n] for n in _PACK_ORDER)
    cols.append(jnp.zeros(w.shape[:-1] + (D_PACK - used,), w.dtype))
    return jnp.concatenate(cols, axis=-1).astype(BF16)


def _prepare(p):
    row = lambda v: v[:, None, :].astype(F32)
    n_layers = p['w_in'].shape[0]
    flat = lambda v: v.reshape(n_layers, 1, -1).astype(BF16)
    return dict(
        w_pack=_pack_w_in(p['w_in']), w_out=p['w_out'].astype(BF16),
        pool_w=p['pool_w'].astype(BF16), pool_scale=row(p['pool_scale']),
        conv_w=jnp.concatenate([p['conv_w'].astype(F32), jnp.zeros((n_layers, 1, W_MIX), F32)], axis=1),
        conv_b=row(p['conv_b']), conv_ln_g=row(p['conv_ln_g']), conv_ln_b=row(p['conv_ln_b']),
        conv_pw_w=p['conv_pw_w'].astype(BF16), conv_pw_b=row(p['conv_pw_b']),
        cmp_pos_k=flat(p['cmp_pos_k']), cmp_pos_v=flat(p['cmp_pos_v']),
        cmp_k_w1=p['cmp_k_w1'].astype(BF16), cmp_k_w2=p['cmp_k_w2'].astype(BF16),
        cmp_v_w1=p['cmp_v_w1'].astype(BF16), cmp_v_w2=p['cmp_v_w2'].astype(BF16),
        ln1_g=row(p['ln1_g']), ln1_b=row(p['ln1_b']), ln2_g=row(p['ln2_g']), ln2_b=row(p['ln2_b']),
        w_gate_up=p['w_gate_up'].astype(BF16), w_down=p['w_down'].astype(BF16))


def _layer(x, xb, p, l, last):
    b, s, d = x.shape
    n_tok = b * s
    x_in = x if xb is None else xb
    proj, ck, cv = _inproj(x_in.reshape(n_tok, d), p['w_pack'], l)
    proj = proj.reshape(b, s, D_PACK)
    y_pool = _pool_mixer(proj, p['pool_w'], p['pool_scale'], l)
    y_conv = _conv_mixer(proj, p['conv_w'], p['conv_b'], p['conv_ln_g'], p['conv_ln_b'],
                         p['conv_pw_w'], p['conv_pw_b'], l)
    y_dsa = _dsa_mixer(proj)
    k_cmp = _compress(ck.reshape(b, s, HEAD_DIM), p['cmp_pos_k'], p['cmp_k_w1'], p['cmp_k_w2'], l)
    v_cmp = _compress(cv.reshape(b, s, HEAD_DIM), p['cmp_pos_v'], p['cmp_v_w1'], p['cmp_v_w2'], l)
    y_nsa = _nsa_mixer(proj, k_cmp, v_cmp)
    ys = [y.reshape(n_tok, W_MIX) for y in (y_pool, y_conv, y_dsa, y_nsa)]
    x1, x1b = _outproj_ln(ys, p['w_out'], x.reshape(n_tok, d), p['ln1_g'], p['ln1_b'], l)
    x2, x2b = _ffn_ln(x1b, p['w_gate_up'], p['w_down'], x1, p['ln2_g'], p['ln2_b'], l,
                      want_bf16=not last)
    return x2.reshape(b, s, d), None if last else x2b.reshape(b, s, d)


def kernel(x, w_in, w_out, pool_w, pool_scale, conv_w, conv_b, conv_ln_g, conv_ln_b, conv_pw_w, conv_pw_b,
           cmp_pos_k, cmp_pos_v, cmp_k_w1, cmp_k_w2, cmp_v_w1, cmp_v_w2, ln1_g, ln1_b, ln2_g, ln2_b,
           w_gate_up, w_down):
    params = dict(w_in=w_in, w_out=w_out, pool_w=pool_w, pool_scale=pool_scale, conv_w=conv_w, conv_b=conv_b,
                  conv_ln_g=conv_ln_g, conv_ln_b=conv_ln_b, conv_pw_w=conv_pw_w, conv_pw_b=conv_pw_b,
                  cmp_pos_k=cmp_pos_k, cmp_pos_v=cmp_pos_v, cmp_k_w1=cmp_k_w1, cmp_k_w2=cmp_k_w2,
                  cmp_v_w1=cmp_v_w1, cmp_v_w2=cmp_v_w2, ln1_g=ln1_g, ln1_b=ln1_b, ln2_g=ln2_g, ln2_b=ln2_b,
                  w_gate_up=w_gate_up, w_down=w_down)
    prepared = _prepare(params)
    xb = None
    for l in range(w_in.shape[0]):
        x, xb = _layer(x, xb, prepared, l, last=l == w_in.shape[0] - 1)
    return x
```

```python
import functools

import numpy as np
import jax
import jax.numpy as jnp
from jax import lax
from jax.experimental import pallas as pl
from jax.experimental.pallas import tpu as pltpu

F32 = jnp.float32
BF16 = jnp.bfloat16
I32 = jnp.int32

D_MODEL = 2048
DEPTH = 2
W_MIX = D_MODEL // 4
HEAD_DIM = 128
N_HEADS = W_MIX // HEAD_DIM
POOL_WINDOWS = (2, 4, 8, 16)
POOL_GROUP = W_MIX // len(POOL_WINDOWS)
CONV_WIDTH = 31
IDX_HEADS = 8
IDX_DIM = 64
DSA_TOPK_MAX = 256
CMP_BLOCK = 32
CMP_STRIDE = 16
SEL_BLOCK = 64
SEL_TOPK = 16
NSA_WINDOW = 512
D_FF = ((8 * D_MODEL + 3 * 256 - 1) // (3 * 256)) * 256
DN_ALPHA = (2 * DEPTH) ** 0.25
ATTN_SCALE = HEAD_DIM ** -0.5
IDX_SCALE = (IDX_HEADS * IDX_DIM) ** -0.5
LN_EPS = 1e-5
LOG2E = 1.4426950408889634

LANES = 128
SUBLANES = 8
VMEM_LIMIT = 48 * 1024 * 1024

NEG = -1e30
INT_MIN = -2 ** 31

_SRC_SIZES = (W_MIX, W_MIX, W_MIX, W_MIX, HEAD_DIM, HEAD_DIM, IDX_HEADS * IDX_DIM, IDX_DIM, IDX_HEADS,
              W_MIX, HEAD_DIM, HEAD_DIM, HEAD_DIM, HEAD_DIM, HEAD_DIM, HEAD_DIM, N_HEADS * 3)
_SRC_NAMES = ('pool', 'c_a', 'c_g', 'd_q', 'd_k', 'd_v', 'i_q', 'i_k', 'i_w',
              'n_q', 'n_ck', 'n_cv', 'n_sk', 'n_sv', 'n_wk', 'n_wv', 'n_g')
_SRC_OFF = dict(zip(_SRC_NAMES, np.concatenate([[0], np.cumsum(_SRC_SIZES)[:-1]]).tolist()))
_SRC_SIZE = dict(zip(_SRC_NAMES, _SRC_SIZES))
_PACK_ORDER = ('pool', 'c_a', 'c_g', 'd_q', 'i_q', 'n_q', 'd_k', 'd_v',
               'n_ck', 'n_cv', 'n_sk', 'n_sv', 'n_wk', 'n_wv', 'i_k', 'i_w', 'n_g')
_PACK_OFF = {}
_o = 0
for _n in _PACK_ORDER:
    _PACK_OFF[_n] = _o
    _o += _SRC_SIZE[_n]
D_PACK = ((_o + LANES - 1) // LANES) * LANES
SMALL_OFF = _PACK_OFF['i_k']
IK_LANE = 0
IW_LANE = _PACK_OFF['i_w'] - SMALL_OFF
NG_LANE = _PACK_OFF['n_g'] - SMALL_OFF


def _alibi_slopes():
    n = 2 * N_HEADS
    s = np.power(2.0, -8.0 * np.arange(1, n + 1) / n).astype(np.float32)
    return [float(v) for v in s[0::2]], [float(v) for v in s[1::2]]


SLOPES_DSA, SLOPES_NSA = _alibi_slopes()


def _cparams(sem):
    return pltpu.CompilerParams(dimension_semantics=sem, vmem_limit_bytes=VMEM_LIMIT)


def _resident(shape):
    nd = len(shape)
    return pl.BlockSpec(shape, lambda *_: (0,) * nd, pipeline_mode=pl.Buffered(1))


def _of_layer(arr, l):
    nd = arr.ndim
    return pl.BlockSpec((None,) + arr.shape[1:], lambda *_: (l,) + (0,) * (nd - 1),
                        pipeline_mode=pl.Buffered(1))


def _dot(a, b):
    return jnp.dot(a, b, preferred_element_type=F32)


def _dot_nt(a, b):
    return lax.dot_general(a, b, (((1,), (1,)), ((), ())), preferred_element_type=F32)


def _layernorm(x, g, b):
    mu = jnp.mean(x, axis=-1, keepdims=True)
    xc = x - mu
    var = jnp.mean(xc * xc, axis=-1, keepdims=True)
    return xc * lax.rsqrt(var + LN_EPS) * g + b


INPROJ_TM = 512
INPROJ_CHUNK = 512


def _inproj_kernel(x_ref, w_ref, o_ref, ck_ref, cv_ref):
    x = x_ref[...].astype(BF16)
    n = o_ref.shape[1]
    for c0 in range(0, n, INPROJ_CHUNK):
        c1 = min(c0 + INPROJ_CHUNK, n)
        o_ref[:, c0:c1] = _dot(x, w_ref[:, c0:c1]).astype(BF16)
    ck_ref[...] = o_ref[:, _PACK_OFF['n_ck']:_PACK_OFF['n_ck'] + HEAD_DIM]
    cv_ref[...] = o_ref[:, _PACK_OFF['n_cv']:_PACK_OFF['n_cv'] + HEAD_DIM]


def _inproj(x, w_pack, l):
    n_tok, d = x.shape
    tm = INPROJ_TM
    col = pl.BlockSpec((tm, HEAD_DIM), lambda i: (i, 0))
    return pl.pallas_call(
        _inproj_kernel,
        out_shape=(jax.ShapeDtypeStruct((n_tok, D_PACK), BF16),
                   jax.ShapeDtypeStruct((n_tok, HEAD_DIM), BF16), jax.ShapeDtypeStruct((n_tok, HEAD_DIM), BF16)),
        grid=(n_tok // tm,),
        in_specs=[pl.BlockSpec((tm, d), lambda i: (i, 0)), _of_layer(w_pack, l)],
        out_specs=(pl.BlockSpec((tm, D_PACK), lambda i: (i, 0)), col, col),
        compiler_params=_cparams(("parallel",)),
        name="inproj",
    )(x, w_pack)


POOL_TS = 512
POOL_HALO = 16


def _pool_kernel(u_ref, halo_ref, w_ref, sc_ref, o_ref, xs_ref):
    i = pl.program_id(1)
    ts = u_ref.shape[1]
    xs_ref[POOL_HALO:POOL_HALO + ts, :] = u_ref[0].astype(F32)
    xs_ref[0:POOL_HALO, :] = jnp.where(i > 0, halo_ref[0].astype(F32), 0.0)
    pos = i * ts + lax.broadcasted_iota(I32, (ts, 1), 0)
    for g, win in enumerate(POOL_WINDOWS):
        c = slice(g * POOL_GROUP, (g + 1) * POOL_GROUP)
        x = xs_ref[POOL_HALO:POOL_HALO + ts, c]
        acc = x
        for k in range(1, win):
            acc = acc + xs_ref[POOL_HALO - k:POOL_HALO - k + ts, c]
        cnt = jnp.minimum(pos + 1, win).astype(F32)
        d = acc / cnt - x
        y = _dot(d.astype(BF16), w_ref[g])
        o_ref[0, :, c] = (y * sc_ref[:, c]).astype(BF16)


def _pool_mixer(proj, pool_w, pool_scale, l):
    b, s, _ = proj.shape
    ts = min(POOL_TS, s)
    hb = ts // POOL_HALO
    blk = _PACK_OFF['pool'] // W_MIX
    return pl.pallas_call(
        _pool_kernel,
        out_shape=jax.ShapeDtypeStruct((b, s, W_MIX), BF16),
        grid=(b, s // ts),
        in_specs=[
            pl.BlockSpec((1, ts, W_MIX), lambda bi, i: (bi, i, blk)),
            pl.BlockSpec((1, POOL_HALO, W_MIX), lambda bi, i: (bi, jnp.maximum(i * hb - 1, 0), blk)),
            _of_layer(pool_w, l),
            _of_layer(pool_scale, l),
        ],
        out_specs=pl.BlockSpec((1, ts, W_MIX), lambda bi, i: (bi, i, 0)),
        scratch_shapes=[pltpu.VMEM((ts + POOL_HALO, W_MIX), F32)],
        compiler_params=_cparams(("parallel", "parallel")),
        name="pool_mixer",
    )(proj, proj, pool_w, pool_scale)


CONV_TS = 512
CONV_HALO = 32
CONV_ROWS = 32


def _conv_kernel(a_ref, g_ref, ha_ref, hg_ref, cw_ref, cb_ref, lg_ref, lb_ref, pw_ref, pb_ref,
                 o_ref, hs_ref, sh_ref, y_ref):
    i = pl.program_id(1)
    ts = a_ref.shape[1]
    hs_ref[CONV_HALO:CONV_HALO + ts, :] = a_ref[0].astype(F32) * jax.nn.sigmoid(g_ref[0].astype(F32))
    halo = ha_ref[0].astype(F32) * jax.nn.sigmoid(hg_ref[0].astype(F32))
    hs_ref[0:CONV_HALO, :] = jnp.where(i > 0, halo, 0.0)
    n_sh = ts + CONV_HALO - SUBLANES
    for b in range(1, SUBLANES):
        sh_ref[b - 1, 0:n_sh, :] = hs_ref[b:b + n_sh, :]
    base = CONV_HALO - (CONV_WIDTH - 1)
    for r0 in range(0, ts, CONV_ROWS):
        acc = jnp.broadcast_to(cb_ref[...], (CONV_ROWS, W_MIX))
        for j in range(CONV_WIDTH):
            a8, b = divmod(base + j, SUBLANES)
            r = r0 + a8 * SUBLANES
            src = hs_ref[r:r + CONV_ROWS, :] if b == 0 else sh_ref[b - 1, r:r + CONV_ROWS, :]
            acc = acc + src * cw_ref[j:j + 1, :]
        y = _layernorm(acc, lg_ref[...], lb_ref[...])
        y_ref[r0:r0 + CONV_ROWS, :] = (y * jax.nn.sigmoid(y)).astype(BF16)
    o_ref[0] = (_dot(y_ref[...], pw_ref[...]) + pb_ref[...]).astype(BF16)


def _conv_mixer(proj, cw, conv_b, ln_g, ln_b, pw_w, pw_b, l):
    b, s, _ = proj.shape
    ts = min(CONV_TS, s)
    hb = ts // CONV_HALO
    ba = _PACK_OFF['c_a'] // W_MIX
    bg = _PACK_OFF['c_g'] // W_MIX
    halo_map = lambda blk: (lambda bi, i: (bi, jnp.maximum(i * hb - 1, 0), blk))
    return pl.pallas_call(
        _conv_kernel,
        out_shape=jax.ShapeDtypeStruct((b, s, W_MIX), BF16),
        grid=(b, s // ts),
        in_specs=[
            pl.BlockSpec((1, ts, W_MIX), lambda bi, i: (bi, i, ba)),
            pl.BlockSpec((1, ts, W_MIX), lambda bi, i: (bi, i, bg)),
            pl.BlockSpec((1, CONV_HALO, W_MIX), halo_map(ba)),
            pl.BlockSpec((1, CONV_HALO, W_MIX), halo_map(bg)),
            _of_layer(cw, l), _of_layer(conv_b, l), _of_layer(ln_g, l), _of_layer(ln_b, l),
            _of_layer(pw_w, l), _of_layer(pw_b, l),
        ],
        out_specs=pl.BlockSpec((1, ts, W_MIX), lambda bi, i: (bi, i, 0)),
        scratch_shapes=[pltpu.VMEM((ts + CONV_HALO, W_MIX), F32),
                        pltpu.VMEM((SUBLANES - 1, ts + CONV_HALO - SUBLANES, W_MIX), F32),
                        pltpu.VMEM((ts, W_MIX), BF16)],
        compiler_params=_cparams(("parallel", "parallel")),
        name="conv_mixer",
    )(proj, proj, proj, proj, cw, conv_b, ln_g, ln_b, pw_w, pw_b)


def _compress_kernel(r_ref, pos_ref, w1_ref, w2_ref, o_ref):
    r = r_ref[0]
    half = r.shape[1]
    n = r.shape[0]
    top = _dot(r, w1_ref[0:half, :])
    bot = _dot(r, w1_ref[half:2 * half, :])
    bot_next = pltpu.roll(bot, n - 1, 0)
    posb = _dot(jnp.broadcast_to(pos_ref[...], (8, 2 * half)), w1_ref[...])[0:1, :]
    h = jax.nn.gelu(top + bot_next + posb)
    out = _dot(h.astype(BF16), w2_ref[...])
    row = lax.broadcasted_iota(I32, out.shape, 0)
    o_ref[0] = jnp.where(row < n - 1, out, 0.0).astype(BF16)


def _compress(raw, pos, w1, w2, l):
    b, s, d = raw.shape
    n = s // CMP_STRIDE
    r = raw.reshape(b, n, CMP_STRIDE * d)
    return pl.pallas_call(
        _compress_kernel,
        out_shape=jax.ShapeDtypeStruct((b, n, d), BF16),
        grid=(b,),
        in_specs=[pl.BlockSpec((1, n, CMP_STRIDE * d), lambda bi: (bi, 0, 0)),
                  _of_layer(pos, l), _of_layer(w1, l), _of_layer(w2, l)],
        out_specs=pl.BlockSpec((1, n, d), lambda bi: (bi, 0, 0)),
        compiler_params=_cparams(("parallel",)),
        name="nsa_compress",
    )(r, pos, w1, w2)


ATT_TQ = 256
ATT_TK = 512
POS_RADIX = 64


def _pos_features(s_len):
    assert s_len <= POS_RADIX * 256
    s = np.arange(s_len)
    f = np.zeros((s_len, LANES), np.float32)
    f[:, 0] = f[:, 1] = s // POS_RADIX
    f[:, 2] = f[:, 3] = s % POS_RADIX
    return jnp.asarray(f, BF16)


def _slope_features(slopes, tq):
    f = np.zeros((len(slopes) * tq, LANES), np.float32)
    for h, sl in enumerate(slopes):
        c = np.float32(sl * LOG2E)
        ca = np.float32(np.asarray(c, dtype=BF16))
        cb = np.float32(np.asarray(c - ca, dtype=BF16))
        f[h * tq:(h + 1) * tq, 0:4] = [POS_RADIX * ca, POS_RADIX * cb, ca, cb]
    return jnp.asarray(f, BF16)


def _ones_feature(n):
    f = np.zeros((n, LANES), np.float32)
    f[:, 0] = 1.0
    return jnp.asarray(f, BF16)


def _stack_queries(q_ref, qf_ref, q4_ref):
    tq = q_ref.shape[1]
    for h in range(N_HEADS):
        q4_ref[h * tq:(h + 1) * tq, 0:HEAD_DIM] = q_ref[0, :, h * HEAD_DIM:(h + 1) * HEAD_DIM]
    q4_ref[:, HEAD_DIM:2 * HEAD_DIM] = qf_ref[...]


def _flash_attention(q4_ref, acc_ref, p_ref, a_ref, n_kt, keys_fn, vals_fn, tq):
    acc_ref[...] = jnp.zeros(acc_ref.shape, F32)
    p_ref[...] = jnp.zeros(p_ref.shape, BF16)
    heads = [slice(h * tq, (h + 1) * tq) for h in range(N_HEADS)]

    def logits(rows, kk, mask):
        return jnp.where(mask, _dot_nt(q4_ref[rows, :], kk), NEG)

    def step(kt, ms, with_next):
        vv = vals_fn(jnp.maximum(kt - 1, 0))
        nxt = keys_fn(kt + 1) if with_next else None
        new_ms = []
        for h, rows in enumerate(heads):
            ah = a_ref[rows, :]
            pv = _dot(p_ref[rows, :], vv)
            if with_next:
                a_ref[rows, :] = logits(rows, *nxt)
            m_new = jnp.maximum(ms[h], jnp.max(ah, axis=-1, keepdims=True))
            p_ref[rows, :] = jnp.exp2(ah - m_new).astype(BF16)
            acc_ref[rows, :] = jnp.exp2(ms[h] - m_new) * (acc_ref[rows, :] + pv)
            new_ms.append(m_new)
        return tuple(new_ms)

    first = keys_fn(0)
    for rows in heads:
        a_ref[rows, :] = logits(rows, *first)
    m0 = tuple(jnp.full((tq, 1), NEG, F32) for _ in range(N_HEADS))
    ms = lax.fori_loop(0, n_kt - 1, lambda kt, ms: step(kt, ms, True), m0)
    step(n_kt - 1, ms, False)
    vv = vals_fn(n_kt - 1)
    for rows in heads:
        acc_ref[rows, :] += _dot(p_ref[rows, :], vv)


def _softmax2_rows(a, mask):
    a = jnp.where(mask, a, NEG)
    m = jnp.max(a, axis=-1, keepdims=True)
    e = jnp.where(mask, jnp.exp2(a - m), 0.0)
    s = jnp.sum(e, axis=-1, keepdims=True)
    return e / jnp.maximum(s, 1e-30)


def _split3(x):
    hi = x.astype(BF16)
    r1 = x - hi.astype(F32)
    mid = r1.astype(BF16)
    lo = (r1 - mid.astype(F32)).astype(BF16)
    return hi, mid, lo


def _nsa_kernel(q_ref, sm_ref, kc_ref, vc_ref, sk_ref, sv_ref, wk_ref, wv_ref, pf_ref, qf_ref, vf_ref,
                o_ref, q4_ref, p4_ref, acc_ref, selm_ref, p_ref, a_ref):
    tq = q_ref.shape[1]
    s_len = sk_ref.shape[1]
    n_cmp = kc_ref.shape[1]
    n_sel = s_len // SEL_BLOCK
    n_top = min(SEL_TOPK, n_sel)
    t0 = pl.program_id(1) * tq
    _stack_queries(q_ref, qf_ref, q4_ref)
    t_col = t0 + lax.broadcasted_iota(I32, (tq, 1), 0)

    a_all = _dot_nt(q4_ref[:, 0:HEAD_DIM], kc_ref[0])
    c_idx = lax.broadcasted_iota(I32, (1, n_cmp), 1)
    cd = t_col - (c_idx * CMP_STRIDE + (CMP_BLOCK - 1))
    cmask = (cd >= 0) & (c_idx < n_cmp - 1)
    cdf = cd.astype(F32)
    p_sum = jnp.zeros((tq, n_cmp), F32)
    for h in range(N_HEADS):
        p = _softmax2_rows(a_all[h * tq:(h + 1) * tq] - (SLOPES_NSA[h] * LOG2E) * cdf, cmask)
        p_sum = p_sum + p
        p4_ref[h * tq:(h + 1) * tq, 0:n_cmp] = p.astype(BF16)
    o_cmp = _dot(p4_ref[:, 0:n_cmp], vc_ref[0])

    n_selp = selm_ref.shape[0]
    jj = lax.broadcasted_iota(I32, (n_selp, n_cmp), 0)
    cc = lax.broadcasted_iota(I32, (n_selp, n_cmp), 1)
    c_start = cc * CMP_STRIDE
    overlap = ((c_start < (jj + 1) * SEL_BLOCK) & (c_start + (CMP_BLOCK - 1) >= jj * SEL_BLOCK)
               & (cc < n_cmp - 1))
    ov = jnp.where(overlap, 1.0, 0.0).astype(BF16)
    hi, mid, lo = _split3(p_sum)
    imp = _dot_nt(ov, hi) + _dot_nt(ov, mid) + _dot_nt(ov, lo)
    j_col = lax.broadcasted_iota(I32, (n_selp, 1), 0)
    t_blk = (t0 + lax.broadcasted_iota(I32, (1, tq), 1)) // SEL_BLOCK
    forced = (j_col == 0) | (j_col == t_blk) | (j_col == t_blk - 1)
    imp = jnp.where(forced, jnp.inf, imp)
    imp = jnp.where(j_col <= t_blk, imp, -jnp.inf)
    rank = jnp.zeros((n_selp, tq), F32)
    for i2 in range(n_sel):
        ci = imp[i2:i2 + 1, :]
        tie_first = jnp.where(j_col > i2, 1.0, 0.0)
        rank = rank + jnp.where(ci > imp, 1.0, jnp.where(ci == imp, tie_first, 0.0))
    selm_ref[...] = jnp.where((rank < n_top) & (j_col < n_sel), 1.0, 0.0)
    selm = selm_ref[...].T.astype(BF16)

    tk = min(ATT_TK, s_len)
    n_kt = (t0 + tq - 1) // tk + 1
    e_row = lax.broadcasted_iota(I32, (n_selp, tk), 0)
    e_col = lax.broadcasted_iota(I32, (n_selp, tk), 1)

    def sel_keys(kt):
        s0 = pl.multiple_of(kt * tk, tk)
        kk = jnp.concatenate([sk_ref[0, pl.ds(s0, tk), :], pf_ref[pl.ds(s0, tk), :]], axis=1)
        s_pos = s0 + lax.broadcasted_iota(I32, (1, tk), 1)
        expand = jnp.where(e_row == (s0 + e_col) // SEL_BLOCK, 1.0, 0.0).astype(BF16)
        return kk, (s_pos <= t_col) & (_dot(selm, expand) > 0.5)

    def sel_vals(kt):
        s0 = pl.multiple_of(kt * tk, tk)
        return jnp.concatenate([sv_ref[0, pl.ds(s0, tk), :], vf_ref[0:tk, :]], axis=1)

    _flash_attention(q4_ref, acc_ref, p_ref, a_ref, n_kt, sel_keys, sel_vals, tq)

    wlen = min(NSA_WINDOW + tq, s_len)
    ks = pl.multiple_of(jnp.maximum(t0 + tq - wlen, 0), LANES)
    kw = jnp.concatenate([wk_ref[0, pl.ds(ks, wlen), :], pf_ref[pl.ds(ks, wlen), :]], axis=1)
    vw = jnp.concatenate([wv_ref[0, pl.ds(ks, wlen), :], vf_ref[0:wlen, :]], axis=1)
    wd = t_col - (ks + lax.broadcasted_iota(I32, (1, wlen), 1))
    wmask = (wd >= 0) & (wd < NSA_WINDOW)
    o_win = []
    for h in range(N_HEADS):
        ah = jnp.where(wmask, _dot_nt(q4_ref[h * tq:(h + 1) * tq, :], kw), NEG)
        m = jnp.max(ah, axis=-1, keepdims=True)
        o_win.append(_dot(jnp.exp2(ah - m).astype(BF16), vw))

    gates = jax.nn.sigmoid(sm_ref[0].astype(F32))
    for h in range(N_HEADS):
        rows = slice(h * tq, (h + 1) * tq)
        o_slc = acc_ref[rows, 0:HEAD_DIM] / jnp.maximum(acc_ref[rows, HEAD_DIM:HEAD_DIM + 1], 1e-30)
        o_w = o_win[h][:, 0:HEAD_DIM] / jnp.maximum(o_win[h][:, HEAD_DIM:HEAD_DIM + 1], 1e-30)
        g0 = gates[:, NG_LANE + 3 * h:NG_LANE + 3 * h + 1]
        g1 = gates[:, NG_LANE + 3 * h + 1:NG_LANE + 3 * h + 2]
        g2 = gates[:, NG_LANE + 3 * h + 2:NG_LANE + 3 * h + 3]
        o = g0 * o_cmp[rows] + g1 * o_slc + g2 * o_w
        o_ref[0, :, h * HEAD_DIM:(h + 1) * HEAD_DIM] = o.astype(BF16)


def _nsa_mixer(proj, k_cmp, v_cmp):
    b, s, _ = proj.shape
    tq = min(ATT_TQ, s)
    tk = min(ATT_TK, s)
    n_cmp = k_cmp.shape[1]
    wlen = min(NSA_WINDOW + tq, s)
    col = lambda name: _PACK_OFF[name] // HEAD_DIM
    full = lambda name: pl.BlockSpec((1, s, HEAD_DIM), functools.partial(lambda c, bi, i: (bi, 0, c), col(name)))
    nv = max(tk, wlen)
    return pl.pallas_call(
        _nsa_kernel,
        out_shape=jax.ShapeDtypeStruct((b, s, W_MIX), BF16),
        grid=(b, s // tq),
        in_specs=[
            pl.BlockSpec((1, tq, W_MIX), lambda bi, i: (bi, i, _PACK_OFF['n_q'] // W_MIX)),
            pl.BlockSpec((1, tq, LANES), lambda bi, i: (bi, i, SMALL_OFF // LANES)),
            pl.BlockSpec((1, n_cmp, HEAD_DIM), lambda bi, i: (bi, 0, 0)),
            pl.BlockSpec((1, n_cmp, HEAD_DIM), lambda bi, i: (bi, 0, 0)),
            full('n_sk'), full('n_sv'), full('n_wk'), full('n_wv'),
            _resident((s, LANES)), _resident((N_HEADS * tq, LANES)), _resident((nv, LANES)),
        ],
        out_specs=pl.BlockSpec((1, tq, W_MIX), lambda bi, i: (bi, i, 0)),
        scratch_shapes=[pltpu.VMEM((N_HEADS * tq, 2 * HEAD_DIM), BF16),
                        pltpu.VMEM((N_HEADS * tq, n_cmp), BF16),
                        pltpu.VMEM((N_HEADS * tq, 2 * HEAD_DIM), F32),
                        pltpu.VMEM((((s // SEL_BLOCK + LANES - 1) // LANES) * LANES, tq), F32),
                        pltpu.VMEM((N_HEADS * tq, tk), BF16),
                        pltpu.VMEM((N_HEADS * tq, tk), F32)],
        compiler_params=_cparams(("parallel", "arbitrary")),
        name="nsa_attention",
    )(proj, proj, k_cmp, v_cmp, proj, proj, proj, proj,
      _pos_features(s), _slope_features(SLOPES_NSA, tq), _ones_feature(nv))


I16 = jnp.int16
I16_MIN = -2 ** 15
SEL_ROWS = 64
SCORE_CHAINS = 4


def _dsa_kernel(q_ref, iq_ref, sm_ref, smf_ref, k_ref, v_ref, pf_ref, qf_ref, vf_ref, o_ref,
                q4_ref, acc_ref, hi_ref, lo_ref, lq_ref, selb_ref, p_ref, a_ref):
    tq = q_ref.shape[1]
    s_len = k_ref.shape[1]
    topk = min(DSA_TOPK_MAX, s_len // 4)
    tk = min(ATT_TK, s_len)
    n_ch = tk // SEL_ROWS
    t0 = pl.program_id(1) * tq
    n_kt = (t0 + tq - 1) // tk + 1
    _stack_queries(q_ref, qf_ref, q4_ref)
    t_row = t0 + lax.broadcasted_iota(I32, (1, tq), 1)
    iq = iq_ref[0]
    iw_t = sm_ref[0].astype(F32).T
    one = jnp.ones((), BF16)
    zero = jnp.zeros((), BF16)

    def score_body(kt, _):
        tc = tk // SCORE_CHAINS
        for c in range(SCORE_CHAINS):
            s0 = pl.multiple_of(kt * tk + c * tc, tc)
            ik = smf_ref[0, pl.ds(s0, tc), IK_LANE:IK_LANE + IDX_DIM]
            sc = jnp.zeros((tc, tq), F32)
            for h in range(IDX_HEADS):
                lg = _dot_nt(ik, iq[:, h * IDX_DIM:(h + 1) * IDX_DIM])
                sc = sc + jnp.maximum(lg, 0.0) * iw_t[IW_LANE + h:IW_LANE + h + 1, :]
            sc = jnp.where(sc == 0.0, 0.0, sc)
            bits = lax.bitcast_convert_type(sc, I32)
            key = bits ^ ((bits >> 31) & 0x7FFFFFFF)
            s_pos = s0 + lax.broadcasted_iota(I32, (tc, 1), 0)
            key = jnp.where(s_pos <= t_row, key, INT_MIN)
            hi_ref[pl.ds(s0, tc), :] = (key >> 16).astype(I16)
            lo_ref[pl.ds(s0, tc), :] = ((key & 0xFFFF) + I16_MIN).astype(I16)
        return 0

    lax.fori_loop(0, n_kt, score_body, 0)

    def count_ge(ref, thr_row):
        thr = jnp.broadcast_to(thr_row, (SEL_ROWS, tq))

        def body(kt, cnt):
            s0 = pl.multiple_of(kt * tk, tk)
            for c in range(n_ch):
                cnt = cnt + jnp.where(ref[pl.ds(s0 + c * SEL_ROWS, SEL_ROWS), :] >= thr, one, zero)
            return cnt
        cnt = lax.fori_loop(0, n_kt, body, jnp.zeros((SEL_ROWS, tq), BF16))
        return jnp.sum(cnt.astype(F32), axis=0, keepdims=True)

    def kth_largest(ref, k):
        def bit_body(it, cand):
            trial = cand | jnp.left_shift(jnp.int32(1), 15 - it)
            total = count_ge(ref, (trial + I16_MIN).astype(I16))
            return jnp.where(total >= k, trial, cand)
        return lax.fori_loop(0, 16, bit_body, jnp.zeros((1, tq), I32))

    cand_hi = jnp.maximum(kth_largest(hi_ref, float(topk)), 1)
    p16 = jnp.broadcast_to((cand_hi + I16_MIN).astype(I16), (SEL_ROWS, tq))
    n_above = count_ge(hi_ref, (jnp.minimum(cand_hi + 1, 2 ** 16 - 1) + I16_MIN).astype(I16))
    need = float(topk) - n_above

    def tie_body(kt, _):
        s0 = pl.multiple_of(kt * tk, tk)
        for c in range(n_ch):
            ds = pl.ds(s0 + c * SEL_ROWS, SEL_ROWS)
            lq_ref[ds, :] = jnp.where(hi_ref[ds, :] == p16, lo_ref[ds, :], jnp.full((), I16_MIN, I16))
        return 0

    lax.fori_loop(0, n_kt, tie_body, 0)
    q16 = jnp.broadcast_to((kth_largest(lq_ref, need) + I16_MIN).astype(I16), (SEL_ROWS, tq))

    def sel_body(kt, _):
        s0 = pl.multiple_of(kt * tk, tk)
        for c in range(n_ch):
            ds = pl.ds(s0 + c * SEL_ROWS, SEL_ROWS)
            hi = hi_ref[ds, :]
            tie = jnp.where(hi == p16, jnp.where(lo_ref[ds, :] >= q16, one, zero), zero)
            selb_ref[ds, :] = jnp.where(hi > p16, one, tie)
        return 0

    lax.fori_loop(0, n_kt, sel_body, 0)

    def att_keys(kt):
        s0 = pl.multiple_of(kt * tk, tk)
        kk = jnp.concatenate([k_ref[0, pl.ds(s0, tk), :], pf_ref[pl.ds(s0, tk), :]], axis=1)
        return kk, selb_ref[pl.ds(s0, tk), :].astype(F32).T > 0.5

    def att_vals(kt):
        s0 = pl.multiple_of(kt * tk, tk)
        return jnp.concatenate([v_ref[0, pl.ds(s0, tk), :], vf_ref[...]], axis=1)

    _flash_attention(q4_ref, acc_ref, p_ref, a_ref, n_kt, att_keys, att_vals, tq)
    for h in range(N_HEADS):
        rows = slice(h * tq, (h + 1) * tq)
        o = acc_ref[rows, 0:HEAD_DIM] / jnp.maximum(acc_ref[rows, HEAD_DIM:HEAD_DIM + 1], 1e-30)
        o_ref[0, :, h * HEAD_DIM:(h + 1) * HEAD_DIM] = o.astype(BF16)


def _dsa_mixer(proj):
    b, s, _ = proj.shape
    tq = min(ATT_TQ, s)
    tk = min(ATT_TK, s)
    full = lambda off: pl.BlockSpec((1, s, LANES), functools.partial(lambda c, bi, i: (bi, 0, c), off // LANES))
    return pl.pallas_call(
        _dsa_kernel,
        out_shape=jax.ShapeDtypeStruct((b, s, W_MIX), BF16),
        grid=(b, s // tq),
        in_specs=[
            pl.BlockSpec((1, tq, W_MIX), lambda bi, i: (bi, i, _PACK_OFF['d_q'] // W_MIX)),
            pl.BlockSpec((1, tq, W_MIX), lambda bi, i: (bi, i, _PACK_OFF['i_q'] // W_MIX)),
            pl.BlockSpec((1, tq, LANES), lambda bi, i: (bi, i, SMALL_OFF // LANES)),
            full(SMALL_OFF), full(_PACK_OFF['d_k']), full(_PACK_OFF['d_v']),
            _resident((s, LANES)), _resident((N_HEADS * tq, LANES)), _resident((tk, LANES)),
        ],
        out_specs=pl.BlockSpec((1, tq, W_MIX), lambda bi, i: (bi, i, 0)),
        scratch_shapes=[pltpu.VMEM((N_HEADS * tq, 2 * HEAD_DIM), BF16),
                        pltpu.VMEM((N_HEADS * tq, 2 * HEAD_DIM), F32),
                        pltpu.VMEM((s, tq), I16), pltpu.VMEM((s, tq), I16), pltpu.VMEM((s, tq), I16),
                        pltpu.VMEM((s, tq), BF16),
                        pltpu.VMEM((N_HEADS * tq, tk), BF16),
                        pltpu.VMEM((N_HEADS * tq, tk), F32)],
        compiler_params=_cparams(("parallel", "arbitrary")),
        name="dsa_attention",
    )(proj, proj, proj, proj, proj, proj,
      _pos_features(s), _slope_features(SLOPES_DSA, tq), _ones_feature(tk))


OUTPROJ_TM = 512
LN_ROWS = 256


def _outproj_kernel(y0_ref, y1_ref, y2_ref, y3_ref, w_ref, x_ref, g_ref, b_ref, o_ref, ob_ref):
    for r0 in range(0, x_ref.shape[0], LN_ROWS):
        rows = slice(r0, r0 + LN_ROWS)
        acc = DN_ALPHA * x_ref[rows, :]
        for gi, y_ref in enumerate((y0_ref, y1_ref, y2_ref, y3_ref)):
            acc = acc + _dot(y_ref[rows, :], w_ref[gi * W_MIX:(gi + 1) * W_MIX, :])
        y = _layernorm(acc, g_ref[...], b_ref[...])
        o_ref[rows, :] = y
        ob_ref[rows, :] = y.astype(BF16)


def _outproj_ln(ys, w_out, x, g, b, l):
    n_tok, d = x.shape
    tm = OUTPROJ_TM
    yspec = pl.BlockSpec((tm, W_MIX), lambda i: (i, 0))
    xspec = pl.BlockSpec((tm, d), lambda i: (i, 0))
    return pl.pallas_call(
        _outproj_kernel,
        out_shape=(jax.ShapeDtypeStruct((n_tok, d), F32), jax.ShapeDtypeStruct((n_tok, d), BF16)),
        grid=(n_tok // tm,),
        in_specs=[yspec, yspec, yspec, yspec, _of_layer(w_out, l), xspec, _of_layer(g, l), _of_layer(b, l)],
        out_specs=(xspec, xspec),
        compiler_params=_cparams(("parallel",)),
        name="outproj_ln",
    )(*ys, w_out, x, g, b)


FFN_UP_TM = 1024
FFN_TF = 512
FFN_DOWN_TM = 256


def _ffn_up_kernel(xb_ref, wg_ref, wu_ref, h_ref):
    xb = xb_ref[...]
    gate = _dot(xb, wg_ref[...])
    up = _dot(xb, wu_ref[...])
    h_ref[...] = (gate * jax.nn.sigmoid(gate) * up).astype(BF16)


def _ffn_down_kernel(h_ref, wd_ref, x_ref, g_ref, b_ref, *rest):
    *out_refs, acc_a, acc_b = rest
    i = pl.program_id(0)

    @pl.when(i == 0)
    def _():
        acc_b[...] = jnp.zeros(acc_b.shape, F32)

    def step(acc_prev, acc_next):
        y = _layernorm(acc_prev[...], g_ref[...], b_ref[...])
        out_refs[0][...] = y
        if len(out_refs) > 1:
            out_refs[1][...] = y.astype(BF16)
        acc_next[...] = DN_ALPHA * x_ref[...] + _dot(h_ref[...], wd_ref[...])

    @pl.when(i % 2 == 0)
    def _():
        step(acc_b, acc_a)

    @pl.when(i % 2 == 1)
    def _():
        step(acc_a, acc_b)


def _ffn_ln(xb, w_gate_up, w_down, x, g, b, l, want_bf16):
    n_tok, d = x.shape
    d_ff = w_down.shape[1]
    tm, tf = FFN_UP_TM, FFN_TF
    nf = d_ff // tf
    h = pl.pallas_call(
        _ffn_up_kernel,
        out_shape=jax.ShapeDtypeStruct((n_tok, d_ff), BF16),
        grid=(nf, n_tok // tm),
        in_specs=[
            pl.BlockSpec((tm, d), lambda f, i: (i, 0)),
            pl.BlockSpec((None, d, tf), lambda f, i: (l, 0, f)),
            pl.BlockSpec((None, d, tf), lambda f, i: (l, 0, f + nf)),
        ],
        out_specs=pl.BlockSpec((tm, tf), lambda f, i: (i, f)),
        compiler_params=_cparams(("parallel", "parallel")),
        name="ffn_up",
    )(xb, w_gate_up, w_gate_up)
    tm = FFN_DOWN_TM
    n_tiles = n_tok // tm
    in_row = lambda i: (jnp.minimum(i, n_tiles - 1), 0)
    out_row = lambda i: (jnp.maximum(i - 1, 0), 0)
    out_dtypes = (F32, BF16) if want_bf16 else (F32,)
    outs = pl.pallas_call(
        _ffn_down_kernel,
        out_shape=tuple(jax.ShapeDtypeStruct((n_tok, d), t) for t in out_dtypes),
        grid=(n_tiles + 1,),
        in_specs=[pl.BlockSpec((tm, d_ff), in_row), _of_layer(w_down, l),
                  pl.BlockSpec((tm, d), in_row), _of_layer(g, l), _of_layer(b, l)],
        out_specs=tuple(pl.BlockSpec((tm, d), out_row) for _ in out_dtypes),
        scratch_shapes=[pltpu.VMEM((tm, d), F32), pltpu.VMEM((tm, d), F32)],
        compiler_params=_cparams(("arbitrary",)),
        name="ffn_down_ln",
    )(h, w_down, x, g, b)
    return outs if want_bf16 else (outs[0], None)


PACK_ROWS = 256
_FOLD = {'d_q': ATTN_SCALE * LOG2E, 'n_q': ATTN_SCALE * LOG2E, 'i_w': IDX_SCALE}


def _pack_kernel(w_ref, o_ref):
    for n in _PACK_ORDER:
        src, dst, size = _SRC_OFF[n], _PACK_OFF[n], _SRC_SIZE[n]
        o_ref[:, dst:dst + size] = (w_ref[:, src:src + size] * _FOLD.get(n, 1.0)).astype(BF16)
    used = sum(_SRC_SIZES)
    o_ref[:, used:D_PACK] = jnp.zeros((o_ref.shape[0], D_PACK - used), BF16)


def _pack_w_in(w):
    n_layers, d, d_in = w.shape
    return pl.pallas_call(
        _pack_kernel,
        out_shape=jax.ShapeDtypeStruct((n_layers, d, D_PACK), BF16),
        grid=(n_layers, d // PACK_ROWS),
        in_specs=[pl.BlockSpec((None, PACK_ROWS, d_in), lambda l, i: (l, i, 0))],
        out_specs=pl.BlockSpec((None, PACK_ROWS, D_PACK), lambda l, i: (l, i, 0)),
        compiler_params=_cparams(("parallel", "parallel")),
        name="pack_w_in",
    )(w)


def _prepare(p):
    row = lambda v: v[:, None, :].astype(F32)
    n_layers = p['w_in'].shape[0]
    flat = lambda v: v.reshape(n_layers, 1, -1).astype(BF16)
    return dict(
        w_pack=_pack_w_in(p['w_in']), w_out=p['w_out'].astype(BF16),
        pool_w=p['pool_w'].astype(BF16), pool_scale=row(p['pool_scale']),
        conv_w=jnp.concatenate([p['conv_w'].astype(F32), jnp.zeros((n_layers, 1, W_MIX), F32)], axis=1),
        conv_b=row(p['conv_b']), conv_ln_g=row(p['conv_ln_g']), conv_ln_b=row(p['conv_ln_b']),
        conv_pw_w=p['conv_pw_w'].astype(BF16), conv_pw_b=row(p['conv_pw_b']),
        cmp_pos_k=flat(p['cmp_pos_k']), cmp_pos_v=flat(p['cmp_pos_v']),
        cmp_k_w1=p['cmp_k_w1'].astype(BF16), cmp_k_w2=p['cmp_k_w2'].astype(BF16),
        cmp_v_w1=p['cmp_v_w1'].astype(BF16), cmp_v_w2=p['cmp_v_w2'].astype(BF16),
        ln1_g=row(p['ln1_g']), ln1_b=row(p['ln1_b']), ln2_g=row(p['ln2_g']), ln2_b=row(p['ln2_b']),
        w_gate_up=p['w_gate_up'].astype(BF16), w_down=p['w_down'].astype(BF16))


def _layer(x, xb, p, l, last):
    b, s, d = x.shape
    n_tok = b * s
    x_in = x if xb is None else xb
    proj, ck, cv = _inproj(x_in.reshape(n_tok, d), p['w_pack'], l)
    proj = proj.reshape(b, s, D_PACK)
    y_pool = _pool_mixer(proj, p['pool_w'], p['pool_scale'], l)
    y_conv = _conv_mixer(proj, p['conv_w'], p['conv_b'], p['conv_ln_g'], p['conv_ln_b'],
                         p['conv_pw_w'], p['conv_pw_b'], l)
    y_dsa = _dsa_mixer(proj)
    k_cmp = _compress(ck.reshape(b, s, HEAD_DIM), p['cmp_pos_k'], p['cmp_k_w1'], p['cmp_k_w2'], l)
    v_cmp = _compress(cv.reshape(b, s, HEAD_DIM), p['cmp_pos_v'], p['cmp_v_w1'], p['cmp_v_w2'], l)
    y_nsa = _nsa_mixer(proj, k_cmp, v_cmp)
    ys = [y.reshape(n_tok, W_MIX) for y in (y_pool, y_conv, y_dsa, y_nsa)]
    x1, x1b = _outproj_ln(ys, p['w_out'], x.reshape(n_tok, d), p['ln1_g'], p['ln1_b'], l)
    x2, x2b = _ffn_ln(x1b, p['w_gate_up'], p['w_down'], x1, p['ln2_g'], p['ln2_b'], l,
                      want_bf16=not last)
    return x2.reshape(b, s, d), None if last else x2b.reshape(b, s, d)


def kernel(x, w_in, w_out, pool_w, pool_scale, conv_w, conv_b, conv_ln_g, conv_ln_b, conv_pw_w, conv_pw_b,
           cmp_pos_k, cmp_pos_v, cmp_k_w1, cmp_k_w2, cmp_v_w1, cmp_v_w2, ln1_g, ln1_b, ln2_g, ln2_b,
           w_gate_up, w_down):
    params = dict(w_in=w_in, w_out=w_out, pool_w=pool_w, pool_scale=pool_scale, conv_w=conv_w, conv_b=conv_b,
                  conv_ln_g=conv_ln_g, conv_ln_b=conv_ln_b, conv_pw_w=conv_pw_w, conv_pw_b=conv_pw_b,
                  cmp_pos_k=cmp_pos_k, cmp_pos_v=cmp_pos_v, cmp_k_w1=cmp_k_w1, cmp_k_w2=cmp_k_w2,
                  cmp_v_w1=cmp_v_w1, cmp_v_w2=cmp_v_w2, ln1_g=ln1_g, ln1_b=ln1_b, ln2_g=ln2_g, ln2_b=ln2_b,
                  w_gate_up=w_gate_up, w_down=w_down)
    prepared = _prepare(params)
    xb = None
    for l in range(w_in.shape[0]):
        x, xb = _layer(x, xb, prepared, l, last=l == w_in.shape[0] - 1)
    return x
```

```python
import functools

import numpy as np
import jax
import jax.numpy as jnp
from jax import lax
from jax.experimental import pallas as pl
from jax.experimental.pallas import tpu as pltpu

F32 = jnp.float32
BF16 = jnp.bfloat16
I32 = jnp.int32

D_MODEL = 2048
DEPTH = 2
W_MIX = D_MODEL // 4
HEAD_DIM = 128
N_HEADS = W_MIX // HEAD_DIM
POOL_WINDOWS = (2, 4, 8, 16)
POOL_GROUP = W_MIX // len(POOL_WINDOWS)
CONV_WIDTH = 31
IDX_HEADS = 8
IDX_DIM = 64
DSA_TOPK_MAX = 256
CMP_BLOCK = 32
CMP_STRIDE = 16
SEL_BLOCK = 64
SEL_TOPK = 16
NSA_WINDOW = 512
D_FF = ((8 * D_MODEL + 3 * 256 - 1) // (3 * 256)) * 256
DN_ALPHA = (2 * DEPTH) ** 0.25
ATTN_SCALE = HEAD_DIM ** -0.5
IDX_SCALE = (IDX_HEADS * IDX_DIM) ** -0.5
LN_EPS = 1e-5
LOG2E = 1.4426950408889634

LANES = 128
SUBLANES = 8
VMEM_LIMIT = 48 * 1024 * 1024

NEG = -1e30
INT_MIN = -2 ** 31

_SRC_SIZES = (W_MIX, W_MIX, W_MIX, W_MIX, HEAD_DIM, HEAD_DIM, IDX_HEADS * IDX_DIM, IDX_DIM, IDX_HEADS,
              W_MIX, HEAD_DIM, HEAD_DIM, HEAD_DIM, HEAD_DIM, HEAD_DIM, HEAD_DIM, N_HEADS * 3)
_SRC_NAMES = ('pool', 'c_a', 'c_g', 'd_q', 'd_k', 'd_v', 'i_q', 'i_k', 'i_w',
              'n_q', 'n_ck', 'n_cv', 'n_sk', 'n_sv', 'n_wk', 'n_wv', 'n_g')
_SRC_OFF = dict(zip(_SRC_NAMES, np.concatenate([[0], np.cumsum(_SRC_SIZES)[:-1]]).tolist()))
_SRC_SIZE = dict(zip(_SRC_NAMES, _SRC_SIZES))
_PACK_ORDER = ('pool', 'c_a', 'c_g', 'd_q', 'i_q', 'n_q', 'd_k', 'd_v',
               'n_ck', 'n_cv', 'n_sk', 'n_sv', 'n_wk', 'n_wv', 'i_k', 'i_w', 'n_g')
_PACK_OFF = {}
_o = 0
for _n in _PACK_ORDER:
    _PACK_OFF[_n] = _o
    _o += _SRC_SIZE[_n]
D_PACK = ((_o + LANES - 1) // LANES) * LANES
SMALL_OFF = _PACK_OFF['i_k']
IK_LANE = 0
IW_LANE = _PACK_OFF['i_w'] - SMALL_OFF
NG_LANE = _PACK_OFF['n_g'] - SMALL_OFF


def _alibi_slopes():
    n = 2 * N_HEADS
    s = np.power(2.0, -8.0 * np.arange(1, n + 1) / n).astype(np.float32)
    return [float(v) for v in s[0::2]], [float(v) for v in s[1::2]]


SLOPES_DSA, SLOPES_NSA = _alibi_slopes()


def _cparams(sem):
    return pltpu.CompilerParams(dimension_semantics=sem, vmem_limit_bytes=VMEM_LIMIT)


def _resident(shape):
    nd = len(shape)
    return pl.BlockSpec(shape, lambda *_: (0,) * nd, pipeline_mode=pl.Buffered(1))


def _of_layer(arr, l):
    nd = arr.ndim
    return pl.BlockSpec((None,) + arr.shape[1:], lambda *_: (l,) + (0,) * (nd - 1),
                        pipeline_mode=pl.Buffered(1))


def _dot(a, b):
    return jnp.dot(a, b, preferred_element_type=F32)


def _dot_nt(a, b):
    return lax.dot_general(a, b, (((1,), (1,)), ((), ())), preferred_element_type=F32)


def _layernorm(x, g, b):
    mu = jnp.mean(x, axis=-1, keepdims=True)
    xc = x - mu
    var = jnp.mean(xc * xc, axis=-1, keepdims=True)
    return xc * lax.rsqrt(var + LN_EPS) * g + b


INPROJ_TM = 512
INPROJ_CHUNK = 512


def _inproj_kernel(x_ref, w_ref, o_ref, ck_ref, cv_ref):
    x = x_ref[...].astype(BF16)
    n = o_ref.shape[1]
    for c0 in range(0, n, INPROJ_CHUNK):
        c1 = min(c0 + INPROJ_CHUNK, n)
        o_ref[:, c0:c1] = _dot(x, w_ref[:, c0:c1]).astype(BF16)
    ck_ref[...] = o_ref[:, _PACK_OFF['n_ck']:_PACK_OFF['n_ck'] + HEAD_DIM]
    cv_ref[...] = o_ref[:, _PACK_OFF['n_cv']:_PACK_OFF['n_cv'] + HEAD_DIM]


def _inproj(x, w_pack, l):
    n_tok, d = x.shape
    tm = INPROJ_TM
    col = pl.BlockSpec((tm, HEAD_DIM), lambda i: (i, 0))
    return pl.pallas_call(
        _inproj_kernel,
        out_shape=(jax.ShapeDtypeStruct((n_tok, D_PACK), BF16),
                   jax.ShapeDtypeStruct((n_tok, HEAD_DIM), BF16), jax.ShapeDtypeStruct((n_tok, HEAD_DIM), BF16)),
        grid=(n_tok // tm,),
        in_specs=[pl.BlockSpec((tm, d), lambda i: (i, 0)), _of_layer(w_pack, l)],
        out_specs=(pl.BlockSpec((tm, D_PACK), lambda i: (i, 0)), col, col),
        compiler_params=_cparams(("parallel",)),
        name="inproj",
    )(x, w_pack)


POOL_TS = 512
POOL_HALO = 16


def _pool_kernel(u_ref, halo_ref, w_ref, sc_ref, o_ref, xs_ref):
    i = pl.program_id(1)
    ts = u_ref.shape[1]
    xs_ref[POOL_HALO:POOL_HALO + ts, :] = u_ref[0].astype(F32)
    xs_ref[0:POOL_HALO, :] = jnp.where(i > 0, halo_ref[0].astype(F32), 0.0)
    pos = i * ts + lax.broadcasted_iota(I32, (ts, 1), 0)
    for g, win in enumerate(POOL_WINDOWS):
        c = slice(g * POOL_GROUP, (g + 1) * POOL_GROUP)
        x = xs_ref[POOL_HALO:POOL_HALO + ts, c]
        acc = x
        for k in range(1, win):
            acc = acc + xs_ref[POOL_HALO - k:POOL_HALO - k + ts, c]
        cnt = jnp.minimum(pos + 1, win).astype(F32)
        d = acc / cnt - x
        y = _dot(d.astype(BF16), w_ref[g])
        o_ref[0, :, c] = (y * sc_ref[:, c]).astype(BF16)


def _pool_mixer(proj, pool_w, pool_scale, l):
    b, s, _ = proj.shape
    ts = min(POOL_TS, s)
    hb = ts // POOL_HALO
    blk = _PACK_OFF['pool'] // W_MIX
    return pl.pallas_call(
        _pool_kernel,
        out_shape=jax.ShapeDtypeStruct((b, s, W_MIX), BF16),
        grid=(b, s // ts),
        in_specs=[
            pl.BlockSpec((1, ts, W_MIX), lambda bi, i: (bi, i, blk)),
            pl.BlockSpec((1, POOL_HALO, W_MIX), lambda bi, i: (bi, jnp.maximum(i * hb - 1, 0), blk)),
            _of_layer(pool_w, l),
            _of_layer(pool_scale, l),
        ],
        out_specs=pl.BlockSpec((1, ts, W_MIX), lambda bi, i: (bi, i, 0)),
        scratch_shapes=[pltpu.VMEM((ts + POOL_HALO, W_MIX), F32)],
        compiler_params=_cparams(("parallel", "parallel")),
        name="pool_mixer",
    )(proj, proj, pool_w, pool_scale)


CONV_TS = 512
CONV_HALO = 32
CONV_ROWS = 32


def _conv_kernel(a_ref, g_ref, ha_ref, hg_ref, cw_ref, cb_ref, lg_ref, lb_ref, pw_ref, pb_ref,
                 o_ref, hs_ref, sh_ref, y_ref):
    i = pl.program_id(1)
    ts = a_ref.shape[1]
    hs_ref[CONV_HALO:CONV_HALO + ts, :] = a_ref[0].astype(F32) * jax.nn.sigmoid(g_ref[0].astype(F32))
    halo = ha_ref[0].astype(F32) * jax.nn.sigmoid(hg_ref[0].astype(F32))
    hs_ref[0:CONV_HALO, :] = jnp.where(i > 0, halo, 0.0)
    n_sh = ts + CONV_HALO - SUBLANES
    for b in range(1, SUBLANES):
        sh_ref[b - 1, 0:n_sh, :] = hs_ref[b:b + n_sh, :]
    base = CONV_HALO - (CONV_WIDTH - 1)
    for r0 in range(0, ts, CONV_ROWS):
        acc = jnp.broadcast_to(cb_ref[...], (CONV_ROWS, W_MIX))
        for j in range(CONV_WIDTH):
            a8, b = divmod(base + j, SUBLANES)
            r = r0 + a8 * SUBLANES
            src = hs_ref[r:r + CONV_ROWS, :] if b == 0 else sh_ref[b - 1, r:r + CONV_ROWS, :]
            acc = acc + src * cw_ref[j:j + 1, :]
        y = _layernorm(acc, lg_ref[...], lb_ref[...])
        y_ref[r0:r0 + CONV_ROWS, :] = (y * jax.nn.sigmoid(y)).astype(BF16)
    o_ref[0] = (_dot(y_ref[...], pw_ref[...]) + pb_ref[...]).astype(BF16)


def _conv_mixer(proj, cw, conv_b, ln_g, ln_b, pw_w, pw_b, l):
    b, s, _ = proj.shape
    ts = min(CONV_TS, s)
    hb = ts // CONV_HALO
    ba = _PACK_OFF['c_a'] // W_MIX
    bg = _PACK_OFF['c_g'] // W_MIX
    halo_map = lambda blk: (lambda bi, i: (bi, jnp.maximum(i * hb - 1, 0), blk))
    return pl.pallas_call(
        _conv_kernel,
        out_shape=jax.ShapeDtypeStruct((b, s, W_MIX), BF16),
        grid=(b, s // ts),
        in_specs=[
            pl.BlockSpec((1, ts, W_MIX), lambda bi, i: (bi, i, ba)),
            pl.BlockSpec((1, ts, W_MIX), lambda bi, i: (bi, i, bg)),
            pl.BlockSpec((1, CONV_HALO, W_MIX), halo_map(ba)),
            pl.BlockSpec((1, CONV_HALO, W_MIX), halo_map(bg)),
            _of_layer(cw, l), _of_layer(conv_b, l), _of_layer(ln_g, l), _of_layer(ln_b, l),
            _of_layer(pw_w, l), _of_layer(pw_b, l),
        ],
        out_specs=pl.BlockSpec((1, ts, W_MIX), lambda bi, i: (bi, i, 0)),
        scratch_shapes=[pltpu.VMEM((ts + CONV_HALO, W_MIX), F32),
                        pltpu.VMEM((SUBLANES - 1, ts + CONV_HALO - SUBLANES, W_MIX), F32),
                        pltpu.VMEM((ts, W_MIX), BF16)],
        compiler_params=_cparams(("parallel", "parallel")),
        name="conv_mixer",
    )(proj, proj, proj, proj, cw, conv_b, ln_g, ln_b, pw_w, pw_b)


def _compress_kernel(r_ref, pos_ref, w1_ref, w2_ref, o_ref):
    r = r_ref[0]
    half = r.shape[1]
    n = r.shape[0]
    top = _dot(r, w1_ref[0:half, :])
    bot = _dot(r, w1_ref[half:2 * half, :])
    bot_next = pltpu.roll(bot, n - 1, 0)
    posb = _dot(jnp.broadcast_to(pos_ref[...], (8, 2 * half)), w1_ref[...])[0:1, :]
    h = jax.nn.gelu(top + bot_next + posb)
    out = _dot(h.astype(BF16), w2_ref[...])
    row = lax.broadcasted_iota(I32, out.shape, 0)
    o_ref[0] = jnp.where(row < n - 1, out, 0.0).astype(BF16)


def _compress(raw, pos, w1, w2, l):
    b, s, d = raw.shape
    n = s // CMP_STRIDE
    r = raw.reshape(b, n, CMP_STRIDE * d)
    return pl.pallas_call(
        _compress_kernel,
        out_shape=jax.ShapeDtypeStruct((b, n, d), BF16),
        grid=(b,),
        in_specs=[pl.BlockSpec((1, n, CMP_STRIDE * d), lambda bi: (bi, 0, 0)),
                  _of_layer(pos, l), _of_layer(w1, l), _of_layer(w2, l)],
        out_specs=pl.BlockSpec((1, n, d), lambda bi: (bi, 0, 0)),
        compiler_params=_cparams(("parallel",)),
        name="nsa_compress",
    )(r, pos, w1, w2)


ATT_TQ = 256
ATT_TK = 512
POS_RADIX = 64
BLOCK_COL = 64


def _pos_features(s_len, block_onehot=False):
    assert s_len <= POS_RADIX * 256
    s = np.arange(s_len)
    f = np.zeros((s_len, LANES), np.float32)
    f[:, 0] = f[:, 1] = s // POS_RADIX
    f[:, 2] = f[:, 3] = s % POS_RADIX
    if block_onehot:
        assert s_len // SEL_BLOCK <= LANES - BLOCK_COL
        f[s, BLOCK_COL + s // SEL_BLOCK] = 1.0
    return jnp.asarray(f, BF16)


def _slope_features(slopes, tq):
    f = np.zeros((len(slopes) * tq, LANES), np.float32)
    for h, sl in enumerate(slopes):
        c = np.float32(sl * LOG2E)
        ca = np.float32(np.asarray(c, dtype=BF16))
        cb = np.float32(np.asarray(c - ca, dtype=BF16))
        f[h * tq:(h + 1) * tq, 0:4] = [POS_RADIX * ca, POS_RADIX * cb, ca, cb]
    return jnp.asarray(f, BF16)


def _ones_feature(n):
    f = np.zeros((n, LANES), np.float32)
    f[:, 0] = 1.0
    return jnp.asarray(f, BF16)


def _stack_queries(q_ref, qf_ref, q4_ref):
    tq = q_ref.shape[1]
    for h in range(N_HEADS):
        q4_ref[h * tq:(h + 1) * tq, 0:HEAD_DIM] = q_ref[0, :, h * HEAD_DIM:(h + 1) * HEAD_DIM]
    q4_ref[:, HEAD_DIM:2 * HEAD_DIM] = qf_ref[...]


def _flash_attention(q4_ref, acc_ref, p_ref, a_ref, n_kt, keys_fn, vals_fn, tq, mask_fn=None,
                     last_mask_fn=None):
    acc_ref[...] = jnp.zeros(acc_ref.shape, F32)
    p_ref[...] = jnp.zeros(p_ref.shape, BF16)
    heads = [slice(h * tq, (h + 1) * tq) for h in range(N_HEADS)]

    def tile_logits(kt):
        kk = keys_fn(kt)
        if mask_fn is None:
            return lambda rows: _dot_nt(q4_ref[rows, :], kk)
        mask = mask_fn(kt)
        return lambda rows: jnp.where(mask, _dot_nt(q4_ref[rows, :], kk), NEG)

    def step(kt, ms, last):
        vv = vals_fn(jnp.maximum(kt - 1, 0))
        next_logits = None if last else tile_logits(kt + 1)
        last_mask = last_mask_fn() if (last and last_mask_fn) else None
        new_ms = []
        for h, rows in enumerate(heads):
            ah = a_ref[rows, :]
            if last_mask is not None:
                ah = jnp.where(last_mask, ah, NEG)
            pv = _dot(p_ref[rows, :], vv)
            if not last:
                a_ref[rows, :] = next_logits(rows)
            m_new = jnp.maximum(ms[h], jnp.max(ah, axis=-1, keepdims=True))
            p_ref[rows, :] = jnp.exp2(ah - m_new).astype(BF16)
            acc_ref[rows, :] = jnp.exp2(ms[h] - m_new) * (acc_ref[rows, :] + pv)
            new_ms.append(m_new)
        return tuple(new_ms)

    first_logits = tile_logits(0)
    for rows in heads:
        a_ref[rows, :] = first_logits(rows)
    m0 = tuple(jnp.full((tq, 1), NEG, F32) for _ in range(N_HEADS))
    ms = lax.fori_loop(0, n_kt - 1, lambda kt, ms: step(kt, ms, False), m0)
    step(n_kt - 1, ms, True)
    vv = vals_fn(n_kt - 1)
    for rows in heads:
        acc_ref[rows, :] += _dot(p_ref[rows, :], vv)


def _softmax2_rows(a, mask):
    a = jnp.where(mask, a, NEG)
    m = jnp.max(a, axis=-1, keepdims=True)
    e = jnp.where(mask, jnp.exp2(a - m), 0.0)
    s = jnp.sum(e, axis=-1, keepdims=True)
    return e / jnp.maximum(s, 1e-30)


def _split3(x):
    hi = x.astype(BF16)
    r1 = x - hi.astype(F32)
    mid = r1.astype(BF16)
    lo = (r1 - mid.astype(F32)).astype(BF16)
    return hi, mid, lo


def _nsa_kernel(q_ref, sm_ref, kc_ref, vc_ref, sk_ref, sv_ref, wk_ref, wv_ref, pf_ref, qf_ref, vf_ref,
                o_ref, q4_ref, p4_ref, acc_ref, selm_ref, p_ref, a_ref, ow_ref):
    tq = q_ref.shape[1]
    s_len = sk_ref.shape[1]
    n_cmp = kc_ref.shape[1]
    n_sel = s_len // SEL_BLOCK
    assert n_sel <= LANES - BLOCK_COL
    n_top = min(SEL_TOPK, n_sel)
    t0 = pl.program_id(1) * tq
    _stack_queries(q_ref, qf_ref, q4_ref)
    t_col = t0 + lax.broadcasted_iota(I32, (tq, 1), 0)

    a_all = _dot_nt(q4_ref[:, 0:HEAD_DIM], kc_ref[0])
    c_idx = lax.broadcasted_iota(I32, (1, n_cmp), 1)
    cd = t_col - (c_idx * CMP_STRIDE + (CMP_BLOCK - 1))
    cmask = (cd >= 0) & (c_idx < n_cmp - 1)
    cdf = cd.astype(F32)
    p_sum = jnp.zeros((tq, n_cmp), F32)
    for h in range(N_HEADS):
        p = _softmax2_rows(a_all[h * tq:(h + 1) * tq] - (SLOPES_NSA[h] * LOG2E) * cdf, cmask)
        p_sum = p_sum + p
        p4_ref[h * tq:(h + 1) * tq, 0:n_cmp] = p.astype(BF16)
    o_cmp = _dot(p4_ref[:, 0:n_cmp], vc_ref[0])

    wlen = min(NSA_WINDOW + tq, s_len)
    ks = pl.multiple_of(jnp.maximum(t0 + tq - wlen, 0), LANES)
    kw = jnp.concatenate([wk_ref[0, pl.ds(ks, wlen), :], pf_ref[pl.ds(ks, wlen), :]], axis=1)
    vw = jnp.concatenate([wv_ref[0, pl.ds(ks, wlen), :], vf_ref[0:wlen, :]], axis=1)
    wd = t_col - (ks + lax.broadcasted_iota(I32, (1, wlen), 1))
    wmask = (wd >= 0) & (wd < NSA_WINDOW)
    for h in range(N_HEADS):
        rows = slice(h * tq, (h + 1) * tq)
        ah = jnp.where(wmask, _dot_nt(q4_ref[rows, :], kw), NEG)
        m = jnp.max(ah, axis=-1, keepdims=True)
        ow_ref[rows, :] = _dot(jnp.exp2(ah - m).astype(BF16), vw)

    n_selp = selm_ref.shape[0]
    jj = lax.broadcasted_iota(I32, (n_selp, n_cmp), 0)
    cc = lax.broadcasted_iota(I32, (n_selp, n_cmp), 1)
    c_start = cc * CMP_STRIDE
    overlap = ((c_start < (jj + 1) * SEL_BLOCK) & (c_start + (CMP_BLOCK - 1) >= jj * SEL_BLOCK)
               & (cc < n_cmp - 1))
    ov = jnp.where(overlap, 1.0, 0.0).astype(BF16)
    hi, mid, lo = _split3(p_sum)
    imp = _dot_nt(ov, hi) + _dot_nt(ov, mid) + _dot_nt(ov, lo)
    j_col = lax.broadcasted_iota(I32, (n_selp, 1), 0)
    t_blk = (t0 + lax.broadcasted_iota(I32, (1, tq), 1)) // SEL_BLOCK
    forced = (j_col == 0) | (j_col == t_blk) | (j_col == t_blk - 1)
    imp = jnp.where(forced, jnp.inf, imp)
    imp = jnp.where(j_col <= t_blk, imp, -jnp.inf)
    rank = jnp.zeros((n_selp, tq), F32)
    for i2 in range(n_sel):
        ci = imp[i2:i2 + 1, :]
        tie_first = jnp.where(j_col > i2, 1.0, 0.0)
        rank = rank + jnp.where(ci > imp, 1.0, jnp.where(ci == imp, tie_first, 0.0))
    selm_ref[...] = jnp.where((rank < n_top) & (j_col < n_sel), 0.0, NEG)
    sel_bias = selm_ref[...].T[:, 0:LANES - BLOCK_COL].astype(BF16)
    for h in range(N_HEADS):
        q4_ref[h * tq:(h + 1) * tq, HEAD_DIM + BLOCK_COL:2 * HEAD_DIM] = sel_bias

    tk = min(ATT_TK, s_len)
    n_kt = (t0 + tq - 1) // tk + 1

    def sel_keys(kt):
        s0 = pl.multiple_of(kt * tk, tk)
        return jnp.concatenate([sk_ref[0, pl.ds(s0, tk), :], pf_ref[pl.ds(s0, tk), :]], axis=1)

    def sel_vals(kt):
        s0 = pl.multiple_of(kt * tk, tk)
        return jnp.concatenate([sv_ref[0, pl.ds(s0, tk), :], vf_ref[0:tk, :]], axis=1)

    def causal_last():
        return (n_kt - 1) * tk + lax.broadcasted_iota(I32, (1, tk), 1) <= t_col

    _flash_attention(q4_ref, acc_ref, p_ref, a_ref, n_kt, sel_keys, sel_vals, tq, last_mask_fn=causal_last)

    gates = jax.nn.sigmoid(sm_ref[0].astype(F32))
    for h in range(N_HEADS):
        rows = slice(h * tq, (h + 1) * tq)
        o_slc = acc_ref[rows, 0:HEAD_DIM] / jnp.maximum(acc_ref[rows, HEAD_DIM:HEAD_DIM + 1], 1e-30)
        o_w = ow_ref[rows, 0:HEAD_DIM] / jnp.maximum(ow_ref[rows, HEAD_DIM:HEAD_DIM + 1], 1e-30)
        g0 = gates[:, NG_LANE + 3 * h:NG_LANE + 3 * h + 1]
        g1 = gates[:, NG_LANE + 3 * h + 1:NG_LANE + 3 * h + 2]
        g2 = gates[:, NG_LANE + 3 * h + 2:NG_LANE + 3 * h + 3]
        o = g0 * o_cmp[rows] + g1 * o_slc + g2 * o_w
        o_ref[0, :, h * HEAD_DIM:(h + 1) * HEAD_DIM] = o.astype(BF16)


def _nsa_mixer(proj, k_cmp, v_cmp):
    b, s, _ = proj.shape
    tq = min(ATT_TQ, s)
    tk = min(ATT_TK, s)
    n_cmp = k_cmp.shape[1]
    wlen = min(NSA_WINDOW + tq, s)
    col = lambda name: _PACK_OFF[name] // HEAD_DIM
    full = lambda name: pl.BlockSpec((1, s, HEAD_DIM), functools.partial(lambda c, bi, i: (bi, 0, c), col(name)))
    nv = max(tk, wlen)
    return pl.pallas_call(
        _nsa_kernel,
        out_shape=jax.ShapeDtypeStruct((b, s, W_MIX), BF16),
        grid=(b, s // tq),
        in_specs=[
            pl.BlockSpec((1, tq, W_MIX), lambda bi, i: (bi, i, _PACK_OFF['n_q'] // W_MIX)),
            pl.BlockSpec((1, tq, LANES), lambda bi, i: (bi, i, SMALL_OFF // LANES)),
            pl.BlockSpec((1, n_cmp, HEAD_DIM), lambda bi, i: (bi, 0, 0)),
            pl.BlockSpec((1, n_cmp, HEAD_DIM), lambda bi, i: (bi, 0, 0)),
            full('n_sk'), full('n_sv'), full('n_wk'), full('n_wv'),
            _resident((s, LANES)), _resident((N_HEADS * tq, LANES)), _resident((nv, LANES)),
        ],
        out_specs=pl.BlockSpec((1, tq, W_MIX), lambda bi, i: (bi, i, 0)),
        scratch_shapes=[pltpu.VMEM((N_HEADS * tq, 2 * HEAD_DIM), BF16),
                        pltpu.VMEM((N_HEADS * tq, n_cmp), BF16),
                        pltpu.VMEM((N_HEADS * tq, 2 * HEAD_DIM), F32),
                        pltpu.VMEM((((s // SEL_BLOCK + LANES - 1) // LANES) * LANES, tq), F32),
                        pltpu.VMEM((N_HEADS * tq, tk), BF16),
                        pltpu.VMEM((N_HEADS * tq, tk), F32),
                        pltpu.VMEM((N_HEADS * tq, 2 * HEAD_DIM), F32)],
        compiler_params=_cparams(("parallel", "arbitrary")),
        name="nsa_attention",
    )(proj, proj, k_cmp, v_cmp, proj, proj, proj, proj,
      _pos_features(s, block_onehot=True), _slope_features(SLOPES_NSA, tq), _ones_feature(nv))


I16 = jnp.int16
I16_MIN = -2 ** 15
SEL_ROWS = 64
SCORE_CHAINS = 4


def _dsa_kernel(q_ref, iq_ref, sm_ref, smf_ref, k_ref, v_ref, pf_ref, qf_ref, vf_ref, o_ref,
                q4_ref, acc_ref, hi_ref, lo_ref, lq_ref, selb_ref, p_ref, a_ref):
    tq = q_ref.shape[1]
    s_len = k_ref.shape[1]
    topk = min(DSA_TOPK_MAX, s_len // 4)
    tk = min(ATT_TK, s_len)
    n_ch = tk // SEL_ROWS
    t0 = pl.program_id(1) * tq
    n_kt = (t0 + tq - 1) // tk + 1
    _stack_queries(q_ref, qf_ref, q4_ref)
    t_row = t0 + lax.broadcasted_iota(I32, (1, tq), 1)
    iq = iq_ref[0]
    iw_t = sm_ref[0].astype(F32).T
    one = jnp.ones((), BF16)
    zero = jnp.zeros((), BF16)

    def score_body(kt, _):
        tc = tk // SCORE_CHAINS
        for c in range(SCORE_CHAINS):
            s0 = pl.multiple_of(kt * tk + c * tc, tc)
            ik = smf_ref[0, pl.ds(s0, tc), IK_LANE:IK_LANE + IDX_DIM]
            sc = jnp.zeros((tc, tq), F32)
            for h in range(IDX_HEADS):
                lg = _dot_nt(ik, iq[:, h * IDX_DIM:(h + 1) * IDX_DIM])
                sc = sc + jnp.maximum(lg, 0.0) * iw_t[IW_LANE + h:IW_LANE + h + 1, :]
            sc = jnp.where(sc == 0.0, 0.0, sc)
            bits = lax.bitcast_convert_type(sc, I32)
            key = bits ^ ((bits >> 31) & 0x7FFFFFFF)
            s_pos = s0 + lax.broadcasted_iota(I32, (tc, 1), 0)
            key = jnp.where(s_pos <= t_row, key, INT_MIN)
            hi_ref[pl.ds(s0, tc), :] = (key >> 16).astype(I16)
            lo_ref[pl.ds(s0, tc), :] = ((key & 0xFFFF) + I16_MIN).astype(I16)
        return 0

    lax.fori_loop(0, n_kt, score_body, 0)

    def count_ge(ref, thr_row):
        thr = jnp.broadcast_to(thr_row, (SEL_ROWS, tq))

        def body(kt, cnt):
            s0 = pl.multiple_of(kt * tk, tk)
            for c in range(n_ch):
                cnt = cnt + jnp.where(ref[pl.ds(s0 + c * SEL_ROWS, SEL_ROWS), :] >= thr, one, zero)
            return cnt
        cnt = lax.fori_loop(0, n_kt, body, jnp.zeros((SEL_ROWS, tq), BF16))
        return jnp.sum(cnt.astype(F32), axis=0, keepdims=True)

    def kth_largest(ref, k):
        def bit_body(it, cand):
            trial = cand | jnp.left_shift(jnp.int32(1), 15 - it)
            total = count_ge(ref, (trial + I16_MIN).astype(I16))
            return jnp.where(total >= k, trial, cand)
        return lax.fori_loop(0, 16, bit_body, jnp.zeros((1, tq), I32))

    cand_hi = jnp.maximum(kth_largest(hi_ref, float(topk)), 1)
    p16 = jnp.broadcast_to((cand_hi + I16_MIN).astype(I16), (SEL_ROWS, tq))
    n_above = count_ge(hi_ref, (jnp.minimum(cand_hi + 1, 2 ** 16 - 1) + I16_MIN).astype(I16))
    need = float(topk) - n_above

    def tie_body(kt, _):
        s0 = pl.multiple_of(kt * tk, tk)
        for c in range(n_ch):
            ds = pl.ds(s0 + c * SEL_ROWS, SEL_ROWS)
            lq_ref[ds, :] = jnp.where(hi_ref[ds, :] == p16, lo_ref[ds, :], jnp.full((), I16_MIN, I16))
        return 0

    lax.fori_loop(0, n_kt, tie_body, 0)
    q16 = jnp.broadcast_to((kth_largest(lq_ref, need) + I16_MIN).astype(I16), (SEL_ROWS, tq))

    def sel_body(kt, _):
        s0 = pl.multiple_of(kt * tk, tk)
        for c in range(n_ch):
            ds = pl.ds(s0 + c * SEL_ROWS, SEL_ROWS)
            hi = hi_ref[ds, :]
            tie = jnp.where(hi == p16, jnp.where(lo_ref[ds, :] >= q16, one, zero), zero)
            selb_ref[ds, :] = jnp.where(hi > p16, one, tie)
        return 0

    lax.fori_loop(0, n_kt, sel_body, 0)

    def att_keys(kt):
        s0 = pl.multiple_of(kt * tk, tk)
        return jnp.concatenate([k_ref[0, pl.ds(s0, tk), :], pf_ref[pl.ds(s0, tk), :]], axis=1)

    def att_vals(kt):
        s0 = pl.multiple_of(kt * tk, tk)
        return jnp.concatenate([v_ref[0, pl.ds(s0, tk), :], vf_ref[...]], axis=1)

    def att_mask(kt):
        s0 = pl.multiple_of(kt * tk, tk)
        return selb_ref[pl.ds(s0, tk), :].astype(F32).T > 0.5

    _flash_attention(q4_ref, acc_ref, p_ref, a_ref, n_kt, att_keys, att_vals, tq, mask_fn=att_mask)
    for h in range(N_HEADS):
        rows = slice(h * tq, (h + 1) * tq)
        o = acc_ref[rows, 0:HEAD_DIM] / jnp.maximum(acc_ref[rows, HEAD_DIM:HEAD_DIM + 1], 1e-30)
        o_ref[0, :, h * HEAD_DIM:(h + 1) * HEAD_DIM] = o.astype(BF16)


def _dsa_mixer(proj):
    b, s, _ = proj.shape
    tq = min(ATT_TQ, s)
    tk = min(ATT_TK, s)
    full = lambda off: pl.BlockSpec((1, s, LANES), functools.partial(lambda c, bi, i: (bi, 0, c), off // LANES))
    return pl.pallas_call(
        _dsa_kernel,
        out_shape=jax.ShapeDtypeStruct((b, s, W_MIX), BF16),
        grid=(b, s // tq),
        in_specs=[
            pl.BlockSpec((1, tq, W_MIX), lambda bi, i: (bi, i, _PACK_OFF['d_q'] // W_MIX)),
            pl.BlockSpec((1, tq, W_MIX), lambda bi, i: (bi, i, _PACK_OFF['i_q'] // W_MIX)),
            pl.BlockSpec((1, tq, LANES), lambda bi, i: (bi, i, SMALL_OFF // LANES)),
            full(SMALL_OFF), full(_PACK_OFF['d_k']), full(_PACK_OFF['d_v']),
            _resident((s, LANES)), _resident((N_HEADS * tq, LANES)), _resident((tk, LANES)),
        ],
        out_specs=pl.BlockSpec((1, tq, W_MIX), lambda bi, i: (bi, i, 0)),
        scratch_shapes=[pltpu.VMEM((N_HEADS * tq, 2 * HEAD_DIM), BF16),
                        pltpu.VMEM((N_HEADS * tq, 2 * HEAD_DIM), F32),
                        pltpu.VMEM((s, tq), I16), pltpu.VMEM((s, tq), I16), pltpu.VMEM((s, tq), I16),
                        pltpu.VMEM((s, tq), BF16),
                        pltpu.VMEM((N_HEADS * tq, tk), BF16),
                        pltpu.VMEM((N_HEADS * tq, tk), F32)],
        compiler_params=_cparams(("parallel", "arbitrary")),
        name="dsa_attention",
    )(proj, proj, proj, proj, proj, proj,
      _pos_features(s), _slope_features(SLOPES_DSA, tq), _ones_feature(tk))


OUTPROJ_TM = 512
LN_ROWS = 256


def _outproj_kernel(y0_ref, y1_ref, y2_ref, y3_ref, w_ref, x_ref, g_ref, b_ref, o_ref, ob_ref):
    for r0 in range(0, x_ref.shape[0], LN_ROWS):
        rows = slice(r0, r0 + LN_ROWS)
        acc = DN_ALPHA * x_ref[rows, :]
        for gi, y_ref in enumerate((y0_ref, y1_ref, y2_ref, y3_ref)):
            acc = acc + _dot(y_ref[rows, :], w_ref[gi * W_MIX:(gi + 1) * W_MIX, :])
        y = _layernorm(acc, g_ref[...], b_ref[...])
        o_ref[rows, :] = y
        ob_ref[rows, :] = y.astype(BF16)


def _outproj_ln(ys, w_out, x, g, b, l):
    n_tok, d = x.shape
    tm = OUTPROJ_TM
    yspec = pl.BlockSpec((tm, W_MIX), lambda i: (i, 0))
    xspec = pl.BlockSpec((tm, d), lambda i: (i, 0))
    return pl.pallas_call(
        _outproj_kernel,
        out_shape=(jax.ShapeDtypeStruct((n_tok, d), F32), jax.ShapeDtypeStruct((n_tok, d), BF16)),
        grid=(n_tok // tm,),
        in_specs=[yspec, yspec, yspec, yspec, _of_layer(w_out, l), xspec, _of_layer(g, l), _of_layer(b, l)],
        out_specs=(xspec, xspec),
        compiler_params=_cparams(("parallel",)),
        name="outproj_ln",
    )(*ys, w_out, x, g, b)


FFN_UP_TM = 1024
FFN_TF = 512
FFN_DOWN_TM = 256


def _ffn_up_kernel(xb_ref, wg_ref, wu_ref, h_ref):
    xb = xb_ref[...]
    gate = _dot(xb, wg_ref[...])
    up = _dot(xb, wu_ref[...])
    h_ref[...] = (gate * jax.nn.sigmoid(gate) * up).astype(BF16)


def _ffn_down_kernel(h_ref, wd_ref, x_ref, g_ref, b_ref, *rest):
    *out_refs, acc_a, acc_b = rest
    i = pl.program_id(0)

    @pl.when(i == 0)
    def _():
        acc_b[...] = jnp.zeros(acc_b.shape, F32)

    def step(acc_prev, acc_next):
        y = _layernorm(acc_prev[...], g_ref[...], b_ref[...])
        out_refs[0][...] = y
        if len(out_refs) > 1:
            out_refs[1][...] = y.astype(BF16)
        acc_next[...] = DN_ALPHA * x_ref[...] + _dot(h_ref[...], wd_ref[...])

    @pl.when(i % 2 == 0)
    def _():
        step(acc_b, acc_a)

    @pl.when(i % 2 == 1)
    def _():
        step(acc_a, acc_b)


def _ffn_ln(xb, w_gate_up, w_down, x, g, b, l, want_bf16):
    n_tok, d = x.shape
    d_ff = w_down.shape[1]
    tm, tf = FFN_UP_TM, FFN_TF
    nf = d_ff // tf
    h = pl.pallas_call(
        _ffn_up_kernel,
        out_shape=jax.ShapeDtypeStruct((n_tok, d_ff), BF16),
        grid=(nf, n_tok // tm),
        in_specs=[
            pl.BlockSpec((tm, d), lambda f, i: (i, 0)),
            pl.BlockSpec((None, d, tf), lambda f, i: (l, 0, f)),
            pl.BlockSpec((None, d, tf), lambda f, i: (l, 0, f + nf)),
        ],
        out_specs=pl.BlockSpec((tm, tf), lambda f, i: (i, f)),
        compiler_params=_cparams(("parallel", "parallel")),
        name="ffn_up",
    )(xb, w_gate_up, w_gate_up)
    tm = FFN_DOWN_TM
    n_tiles = n_tok // tm
    in_row = lambda i: (jnp.minimum(i, n_tiles - 1), 0)
    out_row = lambda i: (jnp.maximum(i - 1, 0), 0)
    out_dtypes = (F32, BF16) if want_bf16 else (F32,)
    outs = pl.pallas_call(
        _ffn_down_kernel,
        out_shape=tuple(jax.ShapeDtypeStruct((n_tok, d), t) for t in out_dtypes),
        grid=(n_tiles + 1,),
        in_specs=[pl.BlockSpec((tm, d_ff), in_row), _of_layer(w_down, l),
                  pl.BlockSpec((tm, d), in_row), _of_layer(g, l), _of_layer(b, l)],
        out_specs=tuple(pl.BlockSpec((tm, d), out_row) for _ in out_dtypes),
        scratch_shapes=[pltpu.VMEM((tm, d), F32), pltpu.VMEM((tm, d), F32)],
        compiler_params=_cparams(("arbitrary",)),
        name="ffn_down_ln",
    )(h, w_down, x, g, b)
    return outs if want_bf16 else (outs[0], None)


PACK_ROWS = 256
_FOLD = {'d_q': ATTN_SCALE * LOG2E, 'n_q': ATTN_SCALE * LOG2E, 'i_w': IDX_SCALE}


def _pack_kernel(w_ref, o_ref):
    for n in _PACK_ORDER:
        src, dst, size = _SRC_OFF[n], _PACK_OFF[n], _SRC_SIZE[n]
        o_ref[:, dst:dst + size] = (w_ref[:, src:src + size] * _FOLD.get(n, 1.0)).astype(BF16)
    used = sum(_SRC_SIZES)
    o_ref[:, used:D_PACK] = jnp.zeros((o_ref.shape[0], D_PACK - used), BF16)


def _pack_w_in(w):
    n_layers, d, d_in = w.shape
    return pl.pallas_call(
        _pack_kernel,
        out_shape=jax.ShapeDtypeStruct((n_layers, d, D_PACK), BF16),
        grid=(n_layers, d // PACK_ROWS),
        in_specs=[pl.BlockSpec((None, PACK_ROWS, d_in), lambda l, i: (l, i, 0))],
        out_specs=pl.BlockSpec((None, PACK_ROWS, D_PACK), lambda l, i: (l, i, 0)),
        compiler_params=_cparams(("parallel", "parallel")),
        name="pack_w_in",
    )(w)


def _prepare(p):
    row = lambda v: v[:, None, :].astype(F32)
    n_layers = p['w_in'].shape[0]
    flat = lambda v: v.reshape(n_layers, 1, -1).astype(BF16)
    return dict(
        w_pack=_pack_w_in(p['w_in']), w_out=p['w_out'].astype(BF16),
        pool_w=p['pool_w'].astype(BF16), pool_scale=row(p['pool_scale']),
        conv_w=jnp.concatenate([p['conv_w'].astype(F32), jnp.zeros((n_layers, 1, W_MIX), F32)], axis=1),
        conv_b=row(p['conv_b']), conv_ln_g=row(p['conv_ln_g']), conv_ln_b=row(p['conv_ln_b']),
        conv_pw_w=p['conv_pw_w'].astype(BF16), conv_pw_b=row(p['conv_pw_b']),
        cmp_pos_k=flat(p['cmp_pos_k']), cmp_pos_v=flat(p['cmp_pos_v']),
        cmp_k_w1=p['cmp_k_w1'].astype(BF16), cmp_k_w2=p['cmp_k_w2'].astype(BF16),
        cmp_v_w1=p['cmp_v_w1'].astype(BF16), cmp_v_w2=p['cmp_v_w2'].astype(BF16),
        ln1_g=row(p['ln1_g']), ln1_b=row(p['ln1_b']), ln2_g=row(p['ln2_g']), ln2_b=row(p['ln2_b']),
        w_gate_up=p['w_gate_up'].astype(BF16), w_down=p['w_down'].astype(BF16))


def _layer(x, xb, p, l, last):
    b, s, d = x.shape
    n_tok = b * s
    x_in = x if xb is None else xb
    proj, ck, cv = _inproj(x_in.reshape(n_tok, d), p['w_pack'], l)
    proj = proj.reshape(b, s, D_PACK)
    y_pool = _pool_mixer(proj, p['pool_w'], p['pool_scale'], l)
    y_conv = _conv_mixer(proj, p['conv_w'], p['conv_b'], p['conv_ln_g'], p['conv_ln_b'],
                         p['conv_pw_w'], p['conv_pw_b'], l)
    y_dsa = _dsa_mixer(proj)
    k_cmp = _compress(ck.reshape(b, s, HEAD_DIM), p['cmp_pos_k'], p['cmp_k_w1'], p['cmp_k_w2'], l)
    v_cmp = _compress(cv.reshape(b, s, HEAD_DIM), p['cmp_pos_v'], p['cmp_v_w1'], p['cmp_v_w2'], l)
    y_nsa = _nsa_mixer(proj, k_cmp, v_cmp)
    ys = [y.reshape(n_tok, W_MIX) for y in (y_pool, y_conv, y_dsa, y_nsa)]
    x1, x1b = _outproj_ln(ys, p['w_out'], x.reshape(n_tok, d), p['ln1_g'], p['ln1_b'], l)
    x2, x2b = _ffn_ln(x1b, p['w_gate_up'], p['w_down'], x1, p['ln2_g'], p['ln2_b'], l,
                      want_bf16=not last)
    return x2.reshape(b, s, d), None if last else x2b.reshape(b, s, d)


def kernel(x, w_in, w_out, pool_w, pool_scale, conv_w, conv_b, conv_ln_g, conv_ln_b, conv_pw_w, conv_pw_b,
           cmp_pos_k, cmp_pos_v, cmp_k_w1, cmp_k_w2, cmp_v_w1, cmp_v_w2, ln1_g, ln1_b, ln2_g, ln2_b,
           w_gate_up, w_down):
    params = dict(w_in=w_in, w_out=w_out, pool_w=pool_w, pool_scale=pool_scale, conv_w=conv_w, conv_b=conv_b,
                  conv_ln_g=conv_ln_g, conv_ln_b=conv_ln_b, conv_pw_w=conv_pw_w, conv_pw_b=conv_pw_b,
                  cmp_pos_k=cmp_pos_k, cmp_pos_v=cmp_pos_v, cmp_k_w1=cmp_k_w1, cmp_k_w2=cmp_k_w2,
                  cmp_v_w1=cmp_v_w1, cmp_v_w2=cmp_v_w2, ln1_g=ln1_g, ln1_b=ln1_b, ln2_g=ln2_g, ln2_b=ln2_b,
                  w_gate_up=w_gate_up, w_down=w_down)
    prepared = _prepare(params)
    xb = None
    for l in range(w_in.shape[0]):
        x, xb = _layer(x, xb, prepared, l, last=l == w_in.shape[0] - 1)
    return x
```

```python
import functools

import numpy as np
import jax
import jax.numpy as jnp
from jax import lax
from jax.experimental import pallas as pl
from jax.experimental.pallas import tpu as pltpu

F32 = jnp.float32
BF16 = jnp.bfloat16
I32 = jnp.int32

D_MODEL = 2048
DEPTH = 2
W_MIX = D_MODEL // 4
HEAD_DIM = 128
N_HEADS = W_MIX // HEAD_DIM
POOL_WINDOWS = (2, 4, 8, 16)
POOL_GROUP = W_MIX // len(POOL_WINDOWS)
CONV_WIDTH = 31
IDX_HEADS = 8
IDX_DIM = 64
DSA_TOPK_MAX = 256
CMP_BLOCK = 32
CMP_STRIDE = 16
SEL_BLOCK = 64
SEL_TOPK = 16
NSA_WINDOW = 512
D_FF = ((8 * D_MODEL + 3 * 256 - 1) // (3 * 256)) * 256
DN_ALPHA = (2 * DEPTH) ** 0.25
ATTN_SCALE = HEAD_DIM ** -0.5
IDX_SCALE = (IDX_HEADS * IDX_DIM) ** -0.5
LN_EPS = 1e-5
LOG2E = 1.4426950408889634

LANES = 128
SUBLANES = 8
VMEM_LIMIT = 48 * 1024 * 1024

NEG = -1e30
INT_MIN = -2 ** 31

_SRC_SIZES = (W_MIX, W_MIX, W_MIX, W_MIX, HEAD_DIM, HEAD_DIM, IDX_HEADS * IDX_DIM, IDX_DIM, IDX_HEADS,
              W_MIX, HEAD_DIM, HEAD_DIM, HEAD_DIM, HEAD_DIM, HEAD_DIM, HEAD_DIM, N_HEADS * 3)
_SRC_NAMES = ('pool', 'c_a', 'c_g', 'd_q', 'd_k', 'd_v', 'i_q', 'i_k', 'i_w',
              'n_q', 'n_ck', 'n_cv', 'n_sk', 'n_sv', 'n_wk', 'n_wv', 'n_g')
_SRC_OFF = dict(zip(_SRC_NAMES, np.concatenate([[0], np.cumsum(_SRC_SIZES)[:-1]]).tolist()))
_SRC_SIZE = dict(zip(_SRC_NAMES, _SRC_SIZES))
_PACK_ORDER = ('pool', 'c_a', 'c_g', 'd_q', 'i_q', 'n_q', 'd_k', 'd_v',
               'n_ck', 'n_cv', 'n_sk', 'n_sv', 'n_wk', 'n_wv', 'i_k', 'i_w', 'n_g')
_PACK_OFF = {}
_o = 0
for _n in _PACK_ORDER:
    _PACK_OFF[_n] = _o
    _o += _SRC_SIZE[_n]
D_PACK = ((_o + LANES - 1) // LANES) * LANES
SMALL_OFF = _PACK_OFF['i_k']
IK_LANE = 0
IW_LANE = _PACK_OFF['i_w'] - SMALL_OFF
NG_LANE = _PACK_OFF['n_g'] - SMALL_OFF


def _alibi_slopes():
    n = 2 * N_HEADS
    s = np.power(2.0, -8.0 * np.arange(1, n + 1) / n).astype(np.float32)
    return [float(v) for v in s[0::2]], [float(v) for v in s[1::2]]


SLOPES_DSA, SLOPES_NSA = _alibi_slopes()


def _cparams(sem):
    return pltpu.CompilerParams(dimension_semantics=sem, vmem_limit_bytes=VMEM_LIMIT)


def _resident(shape):
    nd = len(shape)
    return pl.BlockSpec(shape, lambda *_: (0,) * nd, pipeline_mode=pl.Buffered(1))


def _of_layer(arr, l):
    nd = arr.ndim
    return pl.BlockSpec((None,) + arr.shape[1:], lambda *_: (l,) + (0,) * (nd - 1),
                        pipeline_mode=pl.Buffered(1))


def _dot(a, b):
    return jnp.dot(a, b, preferred_element_type=F32)


def _dot_nt(a, b):
    return lax.dot_general(a, b, (((1,), (1,)), ((), ())), preferred_element_type=F32)


def _layernorm(x, g, b):
    mu = jnp.mean(x, axis=-1, keepdims=True)
    xc = x - mu
    var = jnp.mean(xc * xc, axis=-1, keepdims=True)
    return xc * lax.rsqrt(var + LN_EPS) * g + b


INPROJ_TM = 512
INPROJ_CHUNK = 512


def _inproj_kernel(x_ref, w_ref, o_ref, ck_ref, cv_ref):
    x = x_ref[...].astype(BF16)
    n = o_ref.shape[1]
    for c0 in range(0, n, INPROJ_CHUNK):
        c1 = min(c0 + INPROJ_CHUNK, n)
        o_ref[:, c0:c1] = _dot(x, w_ref[:, c0:c1]).astype(BF16)
    ck_ref[...] = o_ref[:, _PACK_OFF['n_ck']:_PACK_OFF['n_ck'] + HEAD_DIM]
    cv_ref[...] = o_ref[:, _PACK_OFF['n_cv']:_PACK_OFF['n_cv'] + HEAD_DIM]


def _inproj(x, w_pack, l):
    n_tok, d = x.shape
    tm = INPROJ_TM
    col = pl.BlockSpec((tm, HEAD_DIM), lambda i: (i, 0))
    return pl.pallas_call(
        _inproj_kernel,
        out_shape=(jax.ShapeDtypeStruct((n_tok, D_PACK), BF16),
                   jax.ShapeDtypeStruct((n_tok, HEAD_DIM), BF16), jax.ShapeDtypeStruct((n_tok, HEAD_DIM), BF16)),
        grid=(n_tok // tm,),
        in_specs=[pl.BlockSpec((tm, d), lambda i: (i, 0)), _of_layer(w_pack, l)],
        out_specs=(pl.BlockSpec((tm, D_PACK), lambda i: (i, 0)), col, col),
        compiler_params=_cparams(("parallel",)),
        name="inproj",
    )(x, w_pack)


POOL_TS = 512
POOL_HALO = 16


def _pool_kernel(u_ref, halo_ref, w_ref, sc_ref, o_ref, xs_ref):
    i = pl.program_id(1)
    ts = u_ref.shape[1]
    xs_ref[POOL_HALO:POOL_HALO + ts, :] = u_ref[0].astype(F32)
    xs_ref[0:POOL_HALO, :] = jnp.where(i > 0, halo_ref[0].astype(F32), 0.0)
    pos = i * ts + lax.broadcasted_iota(I32, (ts, 1), 0)
    for g, win in enumerate(POOL_WINDOWS):
        c = slice(g * POOL_GROUP, (g + 1) * POOL_GROUP)
        x = xs_ref[POOL_HALO:POOL_HALO + ts, c]
        acc = x
        for k in range(1, win):
            acc = acc + xs_ref[POOL_HALO - k:POOL_HALO - k + ts, c]
        cnt = jnp.minimum(pos + 1, win).astype(F32)
        d = acc / cnt - x
        y = _dot(d.astype(BF16), w_ref[g])
        o_ref[0, :, c] = (y * sc_ref[:, c]).astype(BF16)


def _pool_mixer(proj, pool_w, pool_scale, l):
    b, s, _ = proj.shape
    ts = min(POOL_TS, s)
    hb = ts // POOL_HALO
    blk = _PACK_OFF['pool'] // W_MIX
    return pl.pallas_call(
        _pool_kernel,
        out_shape=jax.ShapeDtypeStruct((b, s, W_MIX), BF16),
        grid=(b, s // ts),
        in_specs=[
            pl.BlockSpec((1, ts, W_MIX), lambda bi, i: (bi, i, blk)),
            pl.BlockSpec((1, POOL_HALO, W_MIX), lambda bi, i: (bi, jnp.maximum(i * hb - 1, 0), blk)),
            _of_layer(pool_w, l),
            _of_layer(pool_scale, l),
        ],
        out_specs=pl.BlockSpec((1, ts, W_MIX), lambda bi, i: (bi, i, 0)),
        scratch_shapes=[pltpu.VMEM((ts + POOL_HALO, W_MIX), F32)],
        compiler_params=_cparams(("parallel", "parallel")),
        name="pool_mixer",
    )(proj, proj, pool_w, pool_scale)


CONV_TS = 512
CONV_HALO = 32
CONV_ROWS = 32


def _conv_kernel(a_ref, g_ref, ha_ref, hg_ref, cw_ref, cb_ref, lg_ref, lb_ref, pw_ref, pb_ref,
                 o_ref, hs_ref, sh_ref, y_ref):
    i = pl.program_id(1)
    ts = a_ref.shape[1]
    hs_ref[CONV_HALO:CONV_HALO + ts, :] = a_ref[0].astype(F32) * jax.nn.sigmoid(g_ref[0].astype(F32))
    halo = ha_ref[0].astype(F32) * jax.nn.sigmoid(hg_ref[0].astype(F32))
    hs_ref[0:CONV_HALO, :] = jnp.where(i > 0, halo, 0.0)
    n_sh = ts + CONV_HALO - SUBLANES
    for b in range(1, SUBLANES):
        sh_ref[b - 1, 0:n_sh, :] = hs_ref[b:b + n_sh, :]
    base = CONV_HALO - (CONV_WIDTH - 1)
    for r0 in range(0, ts, CONV_ROWS):
        acc = jnp.broadcast_to(cb_ref[...], (CONV_ROWS, W_MIX))
        for j in range(CONV_WIDTH):
            a8, b = divmod(base + j, SUBLANES)
            r = r0 + a8 * SUBLANES
            src = hs_ref[r:r + CONV_ROWS, :] if b == 0 else sh_ref[b - 1, r:r + CONV_ROWS, :]
            acc = acc + src * cw_ref[j:j + 1, :]
        y = _layernorm(acc, lg_ref[...], lb_ref[...])
        y_ref[r0:r0 + CONV_ROWS, :] = (y * jax.nn.sigmoid(y)).astype(BF16)
    o_ref[0] = (_dot(y_ref[...], pw_ref[...]) + pb_ref[...]).astype(BF16)


def _conv_mixer(proj, cw, conv_b, ln_g, ln_b, pw_w, pw_b, l):
    b, s, _ = proj.shape
    ts = min(CONV_TS, s)
    hb = ts // CONV_HALO
    ba = _PACK_OFF['c_a'] // W_MIX
    bg = _PACK_OFF['c_g'] // W_MIX
    halo_map = lambda blk: (lambda bi, i: (bi, jnp.maximum(i * hb - 1, 0), blk))
    return pl.pallas_call(
        _conv_kernel,
        out_shape=jax.ShapeDtypeStruct((b, s, W_MIX), BF16),
        grid=(b, s // ts),
        in_specs=[
            pl.BlockSpec((1, ts, W_MIX), lambda bi, i: (bi, i, ba)),
            pl.BlockSpec((1, ts, W_MIX), lambda bi, i: (bi, i, bg)),
            pl.BlockSpec((1, CONV_HALO, W_MIX), halo_map(ba)),
            pl.BlockSpec((1, CONV_HALO, W_MIX), halo_map(bg)),
            _of_layer(cw, l), _of_layer(conv_b, l), _of_layer(ln_g, l), _of_layer(ln_b, l),
            _of_layer(pw_w, l), _of_layer(pw_b, l),
        ],
        out_specs=pl.BlockSpec((1, ts, W_MIX), lambda bi, i: (bi, i, 0)),
        scratch_shapes=[pltpu.VMEM((ts + CONV_HALO, W_MIX), F32),
                        pltpu.VMEM((SUBLANES - 1, ts + CONV_HALO - SUBLANES, W_MIX), F32),
                        pltpu.VMEM((ts, W_MIX), BF16)],
        compiler_params=_cparams(("parallel", "parallel")),
        name="conv_mixer",
    )(proj, proj, proj, proj, cw, conv_b, ln_g, ln_b, pw_w, pw_b)


def _compress_kernel(r_ref, pos_ref, w1_ref, w2_ref, o_ref):
    r = r_ref[0]
    half = r.shape[1]
    n = r.shape[0]
    top = _dot(r, w1_ref[0:half, :])
    bot = _dot(r, w1_ref[half:2 * half, :])
    bot_next = pltpu.roll(bot, n - 1, 0)
    posb = _dot(jnp.broadcast_to(pos_ref[...], (8, 2 * half)), w1_ref[...])[0:1, :]
    h = jax.nn.gelu(top + bot_next + posb)
    out = _dot(h.astype(BF16), w2_ref[...])
    row = lax.broadcasted_iota(I32, out.shape, 0)
    o_ref[0] = jnp.where(row < n - 1, out, 0.0).astype(BF16)


def _compress(raw, pos, w1, w2, l):
    b, s, d = raw.shape
    n = s // CMP_STRIDE
    r = raw.reshape(b, n, CMP_STRIDE * d)
    return pl.pallas_call(
        _compress_kernel,
        out_shape=jax.ShapeDtypeStruct((b, n, d), BF16),
        grid=(b,),
        in_specs=[pl.BlockSpec((1, n, CMP_STRIDE * d), lambda bi: (bi, 0, 0)),
                  _of_layer(pos, l), _of_layer(w1, l), _of_layer(w2, l)],
        out_specs=pl.BlockSpec((1, n, d), lambda bi: (bi, 0, 0)),
        compiler_params=_cparams(("parallel",)),
        name="nsa_compress",
    )(r, pos, w1, w2)


ATT_TQ = 256
ATT_TK = 512
POS_RADIX = 64
BLOCK_COL = 64


def _pos_features(s_len, block_onehot=False):
    assert s_len <= POS_RADIX * 256
    s = np.arange(s_len)
    f = np.zeros((s_len, LANES), np.float32)
    f[:, 0] = f[:, 1] = s // POS_RADIX
    f[:, 2] = f[:, 3] = s % POS_RADIX
    if block_onehot:
        assert s_len // SEL_BLOCK <= LANES - BLOCK_COL
        f[s, BLOCK_COL + s // SEL_BLOCK] = 1.0
    return jnp.asarray(f, BF16)


def _slope_features(slopes, tq):
    f = np.zeros((len(slopes) * tq, LANES), np.float32)
    for h, sl in enumerate(slopes):
        c = np.float32(sl * LOG2E)
        ca = np.float32(np.asarray(c, dtype=BF16))
        cb = np.float32(np.asarray(c - ca, dtype=BF16))
        f[h * tq:(h + 1) * tq, 0:4] = [POS_RADIX * ca, POS_RADIX * cb, ca, cb]
    return jnp.asarray(f, BF16)


def _ones_feature(n):
    f = np.zeros((n, LANES), np.float32)
    f[:, 0] = 1.0
    return jnp.asarray(f, BF16)


def _stack_queries(q_ref, qf_ref, q4_ref):
    tq = q_ref.shape[1]
    for h in range(N_HEADS):
        q4_ref[h * tq:(h + 1) * tq, 0:HEAD_DIM] = q_ref[0, :, h * HEAD_DIM:(h + 1) * HEAD_DIM]
    q4_ref[:, HEAD_DIM:2 * HEAD_DIM] = qf_ref[...]


def _flash_attention(q4_ref, acc_ref, p_ref, a_ref, n_kt, keys_fn, vals_fn, tq, mask_fn=None,
                     last_mask_fn=None):
    acc_ref[...] = jnp.zeros(acc_ref.shape, F32)
    p_ref[...] = jnp.zeros(p_ref.shape, BF16)
    heads = [slice(h * tq, (h + 1) * tq) for h in range(N_HEADS)]

    def tile_logits(kt):
        kk = keys_fn(kt)
        if mask_fn is None:
            return lambda rows: _dot_nt(q4_ref[rows, :], kk)
        mask = mask_fn(kt)
        return lambda rows: jnp.where(mask, _dot_nt(q4_ref[rows, :], kk), NEG)

    def step(kt, ms, last):
        vv = vals_fn(jnp.maximum(kt - 1, 0))
        next_logits = None if last else tile_logits(kt + 1)
        last_mask = last_mask_fn() if (last and last_mask_fn) else None
        new_ms = []
        for h, rows in enumerate(heads):
            ah = a_ref[rows, :]
            if last_mask is not None:
                ah = jnp.where(last_mask, ah, NEG)
            pv = _dot(p_ref[rows, :], vv)
            if not last:
                a_ref[rows, :] = next_logits(rows)
            m_new = jnp.maximum(ms[h], jnp.max(ah, axis=-1, keepdims=True))
            p_ref[rows, :] = jnp.exp2(ah - m_new).astype(BF16)
            acc_ref[rows, :] = jnp.exp2(ms[h] - m_new) * (acc_ref[rows, :] + pv)
            new_ms.append(m_new)
        return tuple(new_ms)

    first_logits = tile_logits(0)
    for rows in heads:
        a_ref[rows, :] = first_logits(rows)
    m0 = tuple(jnp.full((tq, 1), NEG, F32) for _ in range(N_HEADS))
    ms = lax.fori_loop(0, n_kt - 1, lambda kt, ms: step(kt, ms, False), m0)
    step(n_kt - 1, ms, True)
    vv = vals_fn(n_kt - 1)
    for rows in heads:
        acc_ref[rows, :] += _dot(p_ref[rows, :], vv)


def _softmax2_rows(a, mask):
    a = jnp.where(mask, a, NEG)
    m = jnp.max(a, axis=-1, keepdims=True)
    e = jnp.where(mask, jnp.exp2(a - m), 0.0)
    s = jnp.sum(e, axis=-1, keepdims=True)
    return e / jnp.maximum(s, 1e-30)


def _split3(x):
    hi = x.astype(BF16)
    r1 = x - hi.astype(F32)
    mid = r1.astype(BF16)
    lo = (r1 - mid.astype(F32)).astype(BF16)
    return hi, mid, lo


def _nsa_kernel(q_ref, sm_ref, kc_ref, vc_ref, sk_ref, sv_ref, wk_ref, wv_ref, pf_ref, qf_ref, vf_ref,
                o_ref, q4_ref, p4_ref, acc_ref, selm_ref, p_ref, a_ref, ow_ref):
    tq = q_ref.shape[1]
    s_len = sk_ref.shape[1]
    n_cmp = kc_ref.shape[1]
    n_sel = s_len // SEL_BLOCK
    assert n_sel <= LANES - BLOCK_COL
    n_top = min(SEL_TOPK, n_sel)
    t0 = pl.program_id(1) * tq
    _stack_queries(q_ref, qf_ref, q4_ref)
    t_col = t0 + lax.broadcasted_iota(I32, (tq, 1), 0)

    a_all = _dot_nt(q4_ref[:, 0:HEAD_DIM], kc_ref[0])
    c_idx = lax.broadcasted_iota(I32, (1, n_cmp), 1)
    cd = t_col - (c_idx * CMP_STRIDE + (CMP_BLOCK - 1))
    cmask = (cd >= 0) & (c_idx < n_cmp - 1)
    cdf = cd.astype(F32)
    p_sum = jnp.zeros((tq, n_cmp), F32)
    for h in range(N_HEADS):
        p = _softmax2_rows(a_all[h * tq:(h + 1) * tq] - (SLOPES_NSA[h] * LOG2E) * cdf, cmask)
        p_sum = p_sum + p
        p4_ref[h * tq:(h + 1) * tq, 0:n_cmp] = p.astype(BF16)
    o_cmp = _dot(p4_ref[:, 0:n_cmp], vc_ref[0])

    wlen = min(NSA_WINDOW + tq, s_len)
    ks = pl.multiple_of(jnp.maximum(t0 + tq - wlen, 0), LANES)
    kw = jnp.concatenate([wk_ref[0, pl.ds(ks, wlen), :], pf_ref[pl.ds(ks, wlen), :]], axis=1)
    vw = jnp.concatenate([wv_ref[0, pl.ds(ks, wlen), :], vf_ref[0:wlen, :]], axis=1)
    wd = t_col - (ks + lax.broadcasted_iota(I32, (1, wlen), 1))
    wmask = (wd >= 0) & (wd < NSA_WINDOW)
    for h in range(N_HEADS):
        rows = slice(h * tq, (h + 1) * tq)
        ah = jnp.where(wmask, _dot_nt(q4_ref[rows, :], kw), NEG)
        m = jnp.max(ah, axis=-1, keepdims=True)
        ow_ref[rows, :] = _dot(jnp.exp2(ah - m).astype(BF16), vw)

    n_selp = selm_ref.shape[0]
    jj = lax.broadcasted_iota(I32, (n_selp, n_cmp), 0)
    cc = lax.broadcasted_iota(I32, (n_selp, n_cmp), 1)
    c_start = cc * CMP_STRIDE
    overlap = ((c_start < (jj + 1) * SEL_BLOCK) & (c_start + (CMP_BLOCK - 1) >= jj * SEL_BLOCK)
               & (cc < n_cmp - 1))
    ov = jnp.where(overlap, 1.0, 0.0).astype(BF16)
    hi, mid, lo = _split3(p_sum)
    imp = _dot_nt(ov, hi) + _dot_nt(ov, mid) + _dot_nt(ov, lo)
    j_col = lax.broadcasted_iota(I32, (n_selp, 1), 0)
    t_blk = (t0 + lax.broadcasted_iota(I32, (1, tq), 1)) // SEL_BLOCK
    forced = (j_col == 0) | (j_col == t_blk) | (j_col == t_blk - 1)
    imp = jnp.where(forced, jnp.inf, imp)
    imp = jnp.where(j_col <= t_blk, imp, -jnp.inf)
    rank = jnp.zeros((n_selp, tq), F32)
    for i2 in range(n_sel):
        ci = imp[i2:i2 + 1, :]
        tie_first = jnp.where(j_col > i2, 1.0, 0.0)
        rank = rank + jnp.where(ci > imp, 1.0, jnp.where(ci == imp, tie_first, 0.0))
    selm_ref[...] = jnp.where((rank < n_top) & (j_col < n_sel), 0.0, NEG)
    sel_bias = selm_ref[...].T[:, 0:LANES - BLOCK_COL].astype(BF16)
    for h in range(N_HEADS):
        q4_ref[h * tq:(h + 1) * tq, HEAD_DIM + BLOCK_COL:2 * HEAD_DIM] = sel_bias

    tk = min(ATT_TK, s_len)
    n_kt = (t0 + tq - 1) // tk + 1

    def sel_keys(kt):
        s0 = pl.multiple_of(kt * tk, tk)
        return jnp.concatenate([sk_ref[0, pl.ds(s0, tk), :], pf_ref[pl.ds(s0, tk), :]], axis=1)

    def sel_vals(kt):
        s0 = pl.multiple_of(kt * tk, tk)
        return jnp.concatenate([sv_ref[0, pl.ds(s0, tk), :], vf_ref[0:tk, :]], axis=1)

    def causal_last():
        return (n_kt - 1) * tk + lax.broadcasted_iota(I32, (1, tk), 1) <= t_col

    _flash_attention(q4_ref, acc_ref, p_ref, a_ref, n_kt, sel_keys, sel_vals, tq, last_mask_fn=causal_last)

    gates = jax.nn.sigmoid(sm_ref[0].astype(F32))
    for h in range(N_HEADS):
        rows = slice(h * tq, (h + 1) * tq)
        o_slc = acc_ref[rows, 0:HEAD_DIM] / jnp.maximum(acc_ref[rows, HEAD_DIM:HEAD_DIM + 1], 1e-30)
        o_w = ow_ref[rows, 0:HEAD_DIM] / jnp.maximum(ow_ref[rows, HEAD_DIM:HEAD_DIM + 1], 1e-30)
        g0 = gates[:, NG_LANE + 3 * h:NG_LANE + 3 * h + 1]
        g1 = gates[:, NG_LANE + 3 * h + 1:NG_LANE + 3 * h + 2]
        g2 = gates[:, NG_LANE + 3 * h + 2:NG_LANE + 3 * h + 3]
        o = g0 * o_cmp[rows] + g1 * o_slc + g2 * o_w
        o_ref[0, :, h * HEAD_DIM:(h + 1) * HEAD_DIM] = o.astype(BF16)


def _nsa_mixer(proj, k_cmp, v_cmp):
    b, s, _ = proj.shape
    tq = min(ATT_TQ, s)
    tk = min(ATT_TK, s)
    n_cmp = k_cmp.shape[1]
    wlen = min(NSA_WINDOW + tq, s)
    col = lambda name: _PACK_OFF[name] // HEAD_DIM
    full = lambda name: pl.BlockSpec((1, s, HEAD_DIM), functools.partial(lambda c, bi, i: (bi, 0, c), col(name)))
    nv = max(tk, wlen)
    return pl.pallas_call(
        _nsa_kernel,
        out_shape=jax.ShapeDtypeStruct((b, s, W_MIX), BF16),
        grid=(b, s // tq),
        in_specs=[
            pl.BlockSpec((1, tq, W_MIX), lambda bi, i: (bi, i, _PACK_OFF['n_q'] // W_MIX)),
            pl.BlockSpec((1, tq, LANES), lambda bi, i: (bi, i, SMALL_OFF // LANES)),
            pl.BlockSpec((1, n_cmp, HEAD_DIM), lambda bi, i: (bi, 0, 0)),
            pl.BlockSpec((1, n_cmp, HEAD_DIM), lambda bi, i: (bi, 0, 0)),
            full('n_sk'), full('n_sv'), full('n_wk'), full('n_wv'),
            _resident((s, LANES)), _resident((N_HEADS * tq, LANES)), _resident((nv, LANES)),
        ],
        out_specs=pl.BlockSpec((1, tq, W_MIX), lambda bi, i: (bi, i, 0)),
        scratch_shapes=[pltpu.VMEM((N_HEADS * tq, 2 * HEAD_DIM), BF16),
                        pltpu.VMEM((N_HEADS * tq, n_cmp), BF16),
                        pltpu.VMEM((N_HEADS * tq, 2 * HEAD_DIM), F32),
                        pltpu.VMEM((((s // SEL_BLOCK + LANES - 1) // LANES) * LANES, tq), F32),
                        pltpu.VMEM((N_HEADS * tq, tk), BF16),
                        pltpu.VMEM((N_HEADS * tq, tk), F32),
                        pltpu.VMEM((N_HEADS * tq, 2 * HEAD_DIM), F32)],
        compiler_params=_cparams(("parallel", "arbitrary")),
        name="nsa_attention",
    )(proj, proj, k_cmp, v_cmp, proj, proj, proj, proj,
      _pos_features(s, block_onehot=True), _slope_features(SLOPES_NSA, tq), _ones_feature(nv))


I16 = jnp.int16
I16_MIN = -2 ** 15
SEL_ROWS = 64
SCORE_CHAINS = 4


def _dsa_kernel(q_ref, iq_ref, sm_ref, smf_ref, k_ref, v_ref, pf_ref, qf_ref, vf_ref, o_ref,
                q4_ref, acc_ref, hi_ref, lo_ref, lq_ref, selb_ref, p_ref, a_ref):
    tq = q_ref.shape[1]
    s_len = k_ref.shape[1]
    topk = min(DSA_TOPK_MAX, s_len // 4)
    tk = min(ATT_TK, s_len)
    n_ch = tk // SEL_ROWS
    t0 = pl.program_id(1) * tq
    n_kt = (t0 + tq - 1) // tk + 1
    _stack_queries(q_ref, qf_ref, q4_ref)
    t_row = t0 + lax.broadcasted_iota(I32, (1, tq), 1)
    iq = iq_ref[0]
    iw_t = sm_ref[0].astype(F32).T
    one = jnp.ones((), BF16)
    zero = jnp.zeros((), BF16)

    def score_body(kt, _):
        tc = tk // SCORE_CHAINS
        for c in range(SCORE_CHAINS):
            s0 = pl.multiple_of(kt * tk + c * tc, tc)
            ik = smf_ref[0, pl.ds(s0, tc), IK_LANE:IK_LANE + IDX_DIM]
            sc = jnp.zeros((tc, tq), F32)
            for h in range(IDX_HEADS):
                lg = _dot_nt(ik, iq[:, h * IDX_DIM:(h + 1) * IDX_DIM])
                sc = sc + jnp.maximum(lg, 0.0) * iw_t[IW_LANE + h:IW_LANE + h + 1, :]
            sc = jnp.where(sc == 0.0, 0.0, sc)
            bits = lax.bitcast_convert_type(sc, I32)
            key = bits ^ ((bits >> 31) & 0x7FFFFFFF)
            s_pos = s0 + lax.broadcasted_iota(I32, (tc, 1), 0)
            key = jnp.where(s_pos <= t_row, key, INT_MIN)
            hi_ref[pl.ds(s0, tc), :] = (key >> 16).astype(I16)
            lo_ref[pl.ds(s0, tc), :] = ((key & 0xFFFF) + I16_MIN).astype(I16)
        return 0

    lax.fori_loop(0, n_kt, score_body, 0)

    def count_ge(ref, thr_row):
        thr = jnp.broadcast_to(thr_row, (SEL_ROWS, tq))

        def body(kt, cnt):
            s0 = pl.multiple_of(kt * tk, tk)
            for c in range(n_ch):
                cnt = cnt + jnp.where(ref[pl.ds(s0 + c * SEL_ROWS, SEL_ROWS), :] >= thr, one, zero)
            return cnt
        cnt = lax.fori_loop(0, n_kt, body, jnp.zeros((SEL_ROWS, tq), BF16))
        return jnp.sum(cnt.astype(F32), axis=0, keepdims=True)

    def kth_largest(ref, k):
        def bit_body(it, cand):
            trial = cand | jnp.left_shift(jnp.int32(1), 15 - it)
            total = count_ge(ref, (trial + I16_MIN).astype(I16))
            return jnp.where(total >= k, trial, cand)
        return lax.fori_loop(0, 16, bit_body, jnp.zeros((1, tq), I32))

    cand_hi = jnp.maximum(kth_largest(hi_ref, float(topk)), 1)
    p16 = jnp.broadcast_to((cand_hi + I16_MIN).astype(I16), (SEL_ROWS, tq))
    n_above = count_ge(hi_ref, (jnp.minimum(cand_hi + 1, 2 ** 16 - 1) + I16_MIN).astype(I16))
    need = float(topk) - n_above

    def tie_body(kt, _):
        s0 = pl.multiple_of(kt * tk, tk)
        for c in range(n_ch):
            ds = pl.ds(s0 + c * SEL_ROWS, SEL_ROWS)
            lq_ref[ds, :] = jnp.where(hi_ref[ds, :] == p16, lo_ref[ds, :], jnp.full((), I16_MIN, I16))
        return 0

    lax.fori_loop(0, n_kt, tie_body, 0)
    cand_lo = kth_largest(lq_ref, need)
    q16_row = (cand_lo + I16_MIN).astype(I16)
    q16 = jnp.broadcast_to(q16_row, (SEL_ROWS, tq))

    n_gt = jnp.where(cand_lo >= 2 ** 16 - 1, 0.0,
                     count_ge(lq_ref, (jnp.minimum(cand_lo + 1, 2 ** 16 - 1) + I16_MIN).astype(I16)))
    n_with_hi = count_ge(hi_ref, (cand_hi + I16_MIN).astype(I16)) - n_above
    n_ge = jnp.where(cand_lo > 0, count_ge(lq_ref, q16_row), n_with_hi)
    quota = need - n_gt
    any_tied = jnp.max(jnp.where(n_ge - n_gt > quota, 1.0, 0.0)) > 0.0

    @pl.when(jnp.logical_not(any_tied))
    def _():
        def sel_body(kt, _):
            s0 = pl.multiple_of(kt * tk, tk)
            for c in range(n_ch):
                ds = pl.ds(s0 + c * SEL_ROWS, SEL_ROWS)
                hi = hi_ref[ds, :]
                tie = jnp.where(hi == p16, jnp.where(lo_ref[ds, :] >= q16, one, zero), zero)
                selb_ref[ds, :] = jnp.where(hi > p16, one, tie)
            return 0

        lax.fori_loop(0, n_kt, sel_body, 0)

    @pl.when(any_tied)
    def _():
        p_t = jnp.broadcast_to(p16[0:1, :], (tk, tq))
        q_t = jnp.broadcast_to(q16_row, (tk, tq))
        lower = jnp.where(lax.broadcasted_iota(I32, (tk, tk), 0) >= lax.broadcasted_iota(I32, (tk, tk), 1),
                          1.0, 0.0).astype(BF16)

        def tied_body(kt, seen):
            ds = pl.ds(pl.multiple_of(kt * tk, tk), tk)
            hi, lo = hi_ref[ds, :], lo_ref[ds, :]
            above = jnp.where(hi > p_t, one, jnp.where(hi == p_t, jnp.where(lo > q_t, one, zero), zero))
            exact = jnp.where(hi == p_t, jnp.where(lo == q_t, one, zero), zero)
            rank = _dot(lower, exact) + seen
            kept = jnp.where(rank <= quota, 1.0, 0.0).astype(BF16) * exact
            selb_ref[ds, :] = above + kept
            return seen + jnp.sum(exact.astype(F32), axis=0, keepdims=True)

        lax.fori_loop(0, n_kt, tied_body, jnp.zeros((1, tq), F32))

    def att_keys(kt):
        s0 = pl.multiple_of(kt * tk, tk)
        return jnp.concatenate([k_ref[0, pl.ds(s0, tk), :], pf_ref[pl.ds(s0, tk), :]], axis=1)

    def att_vals(kt):
        s0 = pl.multiple_of(kt * tk, tk)
        return jnp.concatenate([v_ref[0, pl.ds(s0, tk), :], vf_ref[...]], axis=1)

    def att_mask(kt):
        s0 = pl.multiple_of(kt * tk, tk)
        return selb_ref[pl.ds(s0, tk), :].astype(F32).T > 0.5

    _flash_attention(q4_ref, acc_ref, p_ref, a_ref, n_kt, att_keys, att_vals, tq, mask_fn=att_mask)
    for h in range(N_HEADS):
        rows = slice(h * tq, (h + 1) * tq)
        o = acc_ref[rows, 0:HEAD_DIM] / jnp.maximum(acc_ref[rows, HEAD_DIM:HEAD_DIM + 1], 1e-30)
        o_ref[0, :, h * HEAD_DIM:(h + 1) * HEAD_DIM] = o.astype(BF16)


def _dsa_mixer(proj):
    b, s, _ = proj.shape
    tq = min(ATT_TQ, s)
    tk = min(ATT_TK, s)
    full = lambda off: pl.BlockSpec((1, s, LANES), functools.partial(lambda c, bi, i: (bi, 0, c), off // LANES))
    return pl.pallas_call(
        _dsa_kernel,
        out_shape=jax.ShapeDtypeStruct((b, s, W_MIX), BF16),
        grid=(b, s // tq),
        in_specs=[
            pl.BlockSpec((1, tq, W_MIX), lambda bi, i: (bi, i, _PACK_OFF['d_q'] // W_MIX)),
            pl.BlockSpec((1, tq, W_MIX), lambda bi, i: (bi, i, _PACK_OFF['i_q'] // W_MIX)),
            pl.BlockSpec((1, tq, LANES), lambda bi, i: (bi, i, SMALL_OFF // LANES)),
            full(SMALL_OFF), full(_PACK_OFF['d_k']), full(_PACK_OFF['d_v']),
            _resident((s, LANES)), _resident((N_HEADS * tq, LANES)), _resident((tk, LANES)),
        ],
        out_specs=pl.BlockSpec((1, tq, W_MIX), lambda bi, i: (bi, i, 0)),
        scratch_shapes=[pltpu.VMEM((N_HEADS * tq, 2 * HEAD_DIM), BF16),
                        pltpu.VMEM((N_HEADS * tq, 2 * HEAD_DIM), F32),
                        pltpu.VMEM((s, tq), I16), pltpu.VMEM((s, tq), I16), pltpu.VMEM((s, tq), I16),
                        pltpu.VMEM((s, tq), BF16),
                        pltpu.VMEM((N_HEADS * tq, tk), BF16),
                        pltpu.VMEM((N_HEADS * tq, tk), F32)],
        compiler_params=_cparams(("parallel", "arbitrary")),
        name="dsa_attention",
    )(proj, proj, proj, proj, proj, proj,
      _pos_features(s), _slope_features(SLOPES_DSA, tq), _ones_feature(tk))


OUTPROJ_TM = 512
LN_ROWS = 256


def _outproj_kernel(y0_ref, y1_ref, y2_ref, y3_ref, w_ref, x_ref, g_ref, b_ref, o_ref, ob_ref):
    for r0 in range(0, x_ref.shape[0], LN_ROWS):
        rows = slice(r0, r0 + LN_ROWS)
        acc = DN_ALPHA * x_ref[rows, :]
        for gi, y_ref in enumerate((y0_ref, y1_ref, y2_ref, y3_ref)):
            acc = acc + _dot(y_ref[rows, :], w_ref[gi * W_MIX:(gi + 1) * W_MIX, :])
        y = _layernorm(acc, g_ref[...], b_ref[...])
        o_ref[rows, :] = y
        ob_ref[rows, :] = y.astype(BF16)


def _outproj_ln(ys, w_out, x, g, b, l):
    n_tok, d = x.shape
    tm = OUTPROJ_TM
    yspec = pl.BlockSpec((tm, W_MIX), lambda i: (i, 0))
    xspec = pl.BlockSpec((tm, d), lambda i: (i, 0))
    return pl.pallas_call(
        _outproj_kernel,
        out_shape=(jax.ShapeDtypeStruct((n_tok, d), F32), jax.ShapeDtypeStruct((n_tok, d), BF16)),
        grid=(n_tok // tm,),
        in_specs=[yspec, yspec, yspec, yspec, _of_layer(w_out, l), xspec, _of_layer(g, l), _of_layer(b, l)],
        out_specs=(xspec, xspec),
        compiler_params=_cparams(("parallel",)),
        name="outproj_ln",
    )(*ys, w_out, x, g, b)


FFN_UP_TM = 1024
FFN_TF = 512
FFN_DOWN_TM = 256


def _ffn_up_kernel(xb_ref, wg_ref, wu_ref, h_ref):
    xb = xb_ref[...]
    gate = _dot(xb, wg_ref[...])
    up = _dot(xb, wu_ref[...])
    h_ref[...] = (gate * jax.nn.sigmoid(gate) * up).astype(BF16)


def _ffn_down_kernel(h_ref, wd_ref, x_ref, g_ref, b_ref, *rest):
    *out_refs, acc_a, acc_b = rest
    i = pl.program_id(0)

    @pl.when(i == 0)
    def _():
        acc_b[...] = jnp.zeros(acc_b.shape, F32)

    def step(acc_prev, acc_next):
        y = _layernorm(acc_prev[...], g_ref[...], b_ref[...])
        out_refs[0][...] = y
        if len(out_refs) > 1:
            out_refs[1][...] = y.astype(BF16)
        acc_next[...] = DN_ALPHA * x_ref[...] + _dot(h_ref[...], wd_ref[...])

    @pl.when(i % 2 == 0)
    def _():
        step(acc_b, acc_a)

    @pl.when(i % 2 == 1)
    def _():
        step(acc_a, acc_b)


def _ffn_ln(xb, w_gate_up, w_down, x, g, b, l, want_bf16):
    n_tok, d = x.shape
    d_ff = w_down.shape[1]
    tm, tf = FFN_UP_TM, FFN_TF
    nf = d_ff // tf
    h = pl.pallas_call(
        _ffn_up_kernel,
        out_shape=jax.ShapeDtypeStruct((n_tok, d_ff), BF16),
        grid=(nf, n_tok // tm),
        in_specs=[
            pl.BlockSpec((tm, d), lambda f, i: (i, 0)),
            pl.BlockSpec((None, d, tf), lambda f, i: (l, 0, f)),
            pl.BlockSpec((None, d, tf), lambda f, i: (l, 0, f + nf)),
        ],
        out_specs=pl.BlockSpec((tm, tf), lambda f, i: (i, f)),
        compiler_params=_cparams(("parallel", "parallel")),
        name="ffn_up",
    )(xb, w_gate_up, w_gate_up)
    tm = FFN_DOWN_TM
    n_tiles = n_tok // tm
    in_row = lambda i: (jnp.minimum(i, n_tiles - 1), 0)
    out_row = lambda i: (jnp.maximum(i - 1, 0), 0)
    out_dtypes = (F32, BF16) if want_bf16 else (F32,)
    outs = pl.pallas_call(
        _ffn_down_kernel,
        out_shape=tuple(jax.ShapeDtypeStruct((n_tok, d), t) for t in out_dtypes),
        grid=(n_tiles + 1,),
        in_specs=[pl.BlockSpec((tm, d_ff), in_row), _of_layer(w_down, l),
                  pl.BlockSpec((tm, d), in_row), _of_layer(g, l), _of_layer(b, l)],
        out_specs=tuple(pl.BlockSpec((tm, d), out_row) for _ in out_dtypes),
        scratch_shapes=[pltpu.VMEM((tm, d), F32), pltpu.VMEM((tm, d), F32)],
        compiler_params=_cparams(("arbitrary",)),
        name="ffn_down_ln",
    )(h, w_down, x, g, b)
    return outs if want_bf16 else (outs[0], None)


PACK_ROWS = 256
_FOLD = {'d_q': ATTN_SCALE * LOG2E, 'n_q': ATTN_SCALE * LOG2E, 'i_w': IDX_SCALE}


def _pack_kernel(w_ref, o_ref):
    for n in _PACK_ORDER:
        src, dst, size = _SRC_OFF[n], _PACK_OFF[n], _SRC_SIZE[n]
        o_ref[:, dst:dst + size] = (w_ref[:, src:src + size] * _FOLD.get(n, 1.0)).astype(BF16)
    used = sum(_SRC_SIZES)
    o_ref[:, used:D_PACK] = jnp.zeros((o_ref.shape[0], D_PACK - used), BF16)


def _pack_w_in(w):
    n_layers, d, d_in = w.shape
    return pl.pallas_call(
        _pack_kernel,
        out_shape=jax.ShapeDtypeStruct((n_layers, d, D_PACK), BF16),
        grid=(n_layers, d // PACK_ROWS),
        in_specs=[pl.BlockSpec((None, PACK_ROWS, d_in), lambda l, i: (l, i, 0))],
        out_specs=pl.BlockSpec((None, PACK_ROWS, D_PACK), lambda l, i: (l, i, 0)),
        compiler_params=_cparams(("parallel", "parallel")),
        name="pack_w_in",
    )(w)


def _prepare(p):
    row = lambda v: v[:, None, :].astype(F32)
    n_layers = p['w_in'].shape[0]
    flat = lambda v: v.reshape(n_layers, 1, -1).astype(BF16)
    return dict(
        w_pack=_pack_w_in(p['w_in']), w_out=p['w_out'].astype(BF16),
        pool_w=p['pool_w'].astype(BF16), pool_scale=row(p['pool_scale']),
        conv_w=jnp.concatenate([p['conv_w'].astype(F32), jnp.zeros((n_layers, 1, W_MIX), F32)], axis=1),
        conv_b=row(p['conv_b']), conv_ln_g=row(p['conv_ln_g']), conv_ln_b=row(p['conv_ln_b']),
        conv_pw_w=p['conv_pw_w'].astype(BF16), conv_pw_b=row(p['conv_pw_b']),
        cmp_pos_k=flat(p['cmp_pos_k']), cmp_pos_v=flat(p['cmp_pos_v']),
        cmp_k_w1=p['cmp_k_w1'].astype(BF16), cmp_k_w2=p['cmp_k_w2'].astype(BF16),
        cmp_v_w1=p['cmp_v_w1'].astype(BF16), cmp_v_w2=p['cmp_v_w2'].astype(BF16),
        ln1_g=row(p['ln1_g']), ln1_b=row(p['ln1_b']), ln2_g=row(p['ln2_g']), ln2_b=row(p['ln2_b']),
        w_gate_up=p['w_gate_up'].astype(BF16), w_down=p['w_down'].astype(BF16))


def _layer(x, xb, p, l, last):
    b, s, d = x.shape
    n_tok = b * s
    x_in = x if xb is None else xb
    proj, ck, cv = _inproj(x_in.reshape(n_tok, d), p['w_pack'], l)
    proj = proj.reshape(b, s, D_PACK)
    y_pool = _pool_mixer(proj, p['pool_w'], p['pool_scale'], l)
    y_conv = _conv_mixer(proj, p['conv_w'], p['conv_b'], p['conv_ln_g'], p['conv_ln_b'],
                         p['conv_pw_w'], p['conv_pw_b'], l)
    y_dsa = _dsa_mixer(proj)
    k_cmp = _compress(ck.reshape(b, s, HEAD_DIM), p['cmp_pos_k'], p['cmp_k_w1'], p['cmp_k_w2'], l)
    v_cmp = _compress(cv.reshape(b, s, HEAD_DIM), p['cmp_pos_v'], p['cmp_v_w1'], p['cmp_v_w2'], l)
    y_nsa = _nsa_mixer(proj, k_cmp, v_cmp)
    ys = [y.reshape(n_tok, W_MIX) for y in (y_pool, y_conv, y_dsa, y_nsa)]
    x1, x1b = _outproj_ln(ys, p['w_out'], x.reshape(n_tok, d), p['ln1_g'], p['ln1_b'], l)
    x2, x2b = _ffn_ln(x1b, p['w_gate_up'], p['w_down'], x1, p['ln2_g'], p['ln2_b'], l,
                      want_bf16=not last)
    return x2.reshape(b, s, d), None if last else x2b.reshape(b, s, d)


def kernel(x, w_in, w_out, pool_w, pool_scale, conv_w, conv_b, conv_ln_g, conv_ln_b, conv_pw_w, conv_pw_b,
           cmp_pos_k, cmp_pos_v, cmp_k_w1, cmp_k_w2, cmp_v_w1, cmp_v_w2, ln1_g, ln1_b, ln2_g, ln2_b,
           w_gate_up, w_down):
    params = dict(w_in=w_in, w_out=w_out, pool_w=pool_w, pool_scale=pool_scale, conv_w=conv_w, conv_b=conv_b,
                  conv_ln_g=conv_ln_g, conv_ln_b=conv_ln_b, conv_pw_w=conv_pw_w, conv_pw_b=conv_pw_b,
                  cmp_pos_k=cmp_pos_k, cmp_pos_v=cmp_pos_v, cmp_k_w1=cmp_k_w1, cmp_k_w2=cmp_k_w2,
                  cmp_v_w1=cmp_v_w1, cmp_v_w2=cmp_v_w2, ln1_g=ln1_g, ln1_b=ln1_b, ln2_g=ln2_g, ln2_b=ln2_b,
                  w_gate_up=w_gate_up, w_down=w_down)
    prepared = _prepare(params)
    xb = None
    for l in range(w_in.shape[0]):
        x, xb = _layer(x, xb, prepared, l, last=l == w_in.shape[0] - 1)
    return x
```

```python
import functools

import numpy as np
import jax
import jax.numpy as jnp
from jax import lax
from jax.experimental import pallas as pl
from jax.experimental.pallas import tpu as pltpu

F32 = jnp.float32
BF16 = jnp.bfloat16
I32 = jnp.int32

D_MODEL = 2048
DEPTH = 2
W_MIX = D_MODEL // 4
HEAD_DIM = 128
N_HEADS = W_MIX // HEAD_DIM
POOL_WINDOWS = (2, 4, 8, 16)
POOL_GROUP = W_MIX // len(POOL_WINDOWS)
CONV_WIDTH = 31
IDX_HEADS = 8
IDX_DIM = 64
DSA_TOPK_MAX = 256
CMP_BLOCK = 32
CMP_STRIDE = 16
SEL_BLOCK = 64
SEL_TOPK = 16
NSA_WINDOW = 512
D_FF = ((8 * D_MODEL + 3 * 256 - 1) // (3 * 256)) * 256
DN_ALPHA = (2 * DEPTH) ** 0.25
ATTN_SCALE = HEAD_DIM ** -0.5
IDX_SCALE = (IDX_HEADS * IDX_DIM) ** -0.5
LN_EPS = 1e-5
LOG2E = 1.4426950408889634

LANES = 128
SUBLANES = 8
VMEM_LIMIT = 48 * 1024 * 1024

NEG = -1e30
INT_MIN = -2 ** 31

_SRC_SIZES = (W_MIX, W_MIX, W_MIX, W_MIX, HEAD_DIM, HEAD_DIM, IDX_HEADS * IDX_DIM, IDX_DIM, IDX_HEADS,
              W_MIX, HEAD_DIM, HEAD_DIM, HEAD_DIM, HEAD_DIM, HEAD_DIM, HEAD_DIM, N_HEADS * 3)
_SRC_NAMES = ('pool', 'c_a', 'c_g', 'd_q', 'd_k', 'd_v', 'i_q', 'i_k', 'i_w',
              'n_q', 'n_ck', 'n_cv', 'n_sk', 'n_sv', 'n_wk', 'n_wv', 'n_g')
_SRC_OFF = dict(zip(_SRC_NAMES, np.concatenate([[0], np.cumsum(_SRC_SIZES)[:-1]]).tolist()))
_SRC_SIZE = dict(zip(_SRC_NAMES, _SRC_SIZES))
_PACK_ORDER = ('pool', 'c_a', 'c_g', 'd_q', 'i_q', 'n_q', 'd_k', 'd_v',
               'n_ck', 'n_cv', 'n_sk', 'n_sv', 'n_wk', 'n_wv', 'i_k', 'i_w', 'n_g')
_PACK_OFF = {}
_o = 0
for _n in _PACK_ORDER:
    _PACK_OFF[_n] = _o
    _o += _SRC_SIZE[_n]
D_PACK = ((_o + LANES - 1) // LANES) * LANES
SMALL_OFF = _PACK_OFF['i_k']
IK_LANE = 0
IW_LANE = _PACK_OFF['i_w'] - SMALL_OFF
NG_LANE = _PACK_OFF['n_g'] - SMALL_OFF


def _alibi_slopes():
    n = 2 * N_HEADS
    s = np.power(2.0, -8.0 * np.arange(1, n + 1) / n).astype(np.float32)
    return [float(v) for v in s[0::2]], [float(v) for v in s[1::2]]


SLOPES_DSA, SLOPES_NSA = _alibi_slopes()


def _cparams(sem):
    return pltpu.CompilerParams(dimension_semantics=sem, vmem_limit_bytes=VMEM_LIMIT)


def _resident(shape):
    nd = len(shape)
    return pl.BlockSpec(shape, lambda *_: (0,) * nd, pipeline_mode=pl.Buffered(1))


def _of_layer(arr, l):
    nd = arr.ndim
    return pl.BlockSpec((None,) + arr.shape[1:], lambda *_: (l,) + (0,) * (nd - 1),
                        pipeline_mode=pl.Buffered(1))


def _dot(a, b):
    return jnp.dot(a, b, preferred_element_type=F32)


def _dot_nt(a, b):
    return lax.dot_general(a, b, (((1,), (1,)), ((), ())), preferred_element_type=F32)


def _layernorm(x, g, b):
    mu = jnp.mean(x, axis=-1, keepdims=True)
    xc = x - mu
    var = jnp.mean(xc * xc, axis=-1, keepdims=True)
    return xc * lax.rsqrt(var + LN_EPS) * g + b


INPROJ_TM = 512
INPROJ_CHUNK = 512


def _inproj_kernel(x_ref, w_ref, o_ref, ck_ref, cv_ref):
    x = x_ref[...].astype(BF16)
    n = o_ref.shape[1]
    for c0 in range(0, n, INPROJ_CHUNK):
        c1 = min(c0 + INPROJ_CHUNK, n)
        o_ref[:, c0:c1] = _dot(x, w_ref[:, c0:c1]).astype(BF16)
    ck_ref[...] = o_ref[:, _PACK_OFF['n_ck']:_PACK_OFF['n_ck'] + HEAD_DIM]
    cv_ref[...] = o_ref[:, _PACK_OFF['n_cv']:_PACK_OFF['n_cv'] + HEAD_DIM]


def _inproj(x, w_pack, l):
    n_tok, d = x.shape
    tm = INPROJ_TM
    col = pl.BlockSpec((tm, HEAD_DIM), lambda i: (i, 0))
    return pl.pallas_call(
        _inproj_kernel,
        out_shape=(jax.ShapeDtypeStruct((n_tok, D_PACK), BF16),
                   jax.ShapeDtypeStruct((n_tok, HEAD_DIM), BF16), jax.ShapeDtypeStruct((n_tok, HEAD_DIM), BF16)),
        grid=(n_tok // tm,),
        in_specs=[pl.BlockSpec((tm, d), lambda i: (i, 0)), _of_layer(w_pack, l)],
        out_specs=(pl.BlockSpec((tm, D_PACK), lambda i: (i, 0)), col, col),
        compiler_params=_cparams(("parallel",)),
        name="inproj",
    )(x, w_pack)


POOL_TS = 512
POOL_HALO = 16


def _pool_kernel(u_ref, halo_ref, w_ref, sc_ref, o_ref, xs_ref):
    i = pl.program_id(1)
    ts = u_ref.shape[1]
    xs_ref[POOL_HALO:POOL_HALO + ts, :] = u_ref[0].astype(F32)
    xs_ref[0:POOL_HALO, :] = jnp.where(i > 0, halo_ref[0].astype(F32), 0.0)
    pos = i * ts + lax.broadcasted_iota(I32, (ts, 1), 0)
    for g, win in enumerate(POOL_WINDOWS):
        c = slice(g * POOL_GROUP, (g + 1) * POOL_GROUP)
        x = xs_ref[POOL_HALO:POOL_HALO + ts, c]
        acc = x
        for k in range(1, win):
            acc = acc + xs_ref[POOL_HALO - k:POOL_HALO - k + ts, c]
        cnt = jnp.minimum(pos + 1, win).astype(F32)
        d = acc / cnt - x
        y = _dot(d.astype(BF16), w_ref[g])
        o_ref[0, :, c] = (y * sc_ref[:, c]).astype(BF16)


def _pool_mixer(proj, pool_w, pool_scale, l):
    b, s, _ = proj.shape
    ts = min(POOL_TS, s)
    hb = ts // POOL_HALO
    blk = _PACK_OFF['pool'] // W_MIX
    return pl.pallas_call(
        _pool_kernel,
        out_shape=jax.ShapeDtypeStruct((b, s, W_MIX), BF16),
        grid=(b, s // ts),
        in_specs=[
            pl.BlockSpec((1, ts, W_MIX), lambda bi, i: (bi, i, blk)),
            pl.BlockSpec((1, POOL_HALO, W_MIX), lambda bi, i: (bi, jnp.maximum(i * hb - 1, 0), blk)),
            _of_layer(pool_w, l),
            _of_layer(pool_scale, l),
        ],
        out_specs=pl.BlockSpec((1, ts, W_MIX), lambda bi, i: (bi, i, 0)),
        scratch_shapes=[pltpu.VMEM((ts + POOL_HALO, W_MIX), F32)],
        compiler_params=_cparams(("parallel", "parallel")),
        name="pool_mixer",
    )(proj, proj, pool_w, pool_scale)


CONV_TS = 512
CONV_HALO = 32
CONV_ROWS = 32


def _conv_kernel(a_ref, g_ref, ha_ref, hg_ref, cw_ref, cb_ref, lg_ref, lb_ref, pw_ref, pb_ref,
                 o_ref, hs_ref, sh_ref, y_ref):
    i = pl.program_id(1)
    ts = a_ref.shape[1]
    hs_ref[CONV_HALO:CONV_HALO + ts, :] = a_ref[0].astype(F32) * jax.nn.sigmoid(g_ref[0].astype(F32))
    halo = ha_ref[0].astype(F32) * jax.nn.sigmoid(hg_ref[0].astype(F32))
    hs_ref[0:CONV_HALO, :] = jnp.where(i > 0, halo, 0.0)
    n_sh = ts + CONV_HALO - SUBLANES
    for b in range(1, SUBLANES):
        sh_ref[b - 1, 0:n_sh, :] = hs_ref[b:b + n_sh, :]
    base = CONV_HALO - (CONV_WIDTH - 1)
    for r0 in range(0, ts, CONV_ROWS):
        acc = jnp.broadcast_to(cb_ref[...], (CONV_ROWS, W_MIX))
        for j in range(CONV_WIDTH):
            a8, b = divmod(base + j, SUBLANES)
            r = r0 + a8 * SUBLANES
            src = hs_ref[r:r + CONV_ROWS, :] if b == 0 else sh_ref[b - 1, r:r + CONV_ROWS, :]
            acc = acc + src * cw_ref[j:j + 1, :]
        y = _layernorm(acc, lg_ref[...], lb_ref[...])
        y_ref[r0:r0 + CONV_ROWS, :] = (y * jax.nn.sigmoid(y)).astype(BF16)
    o_ref[0] = (_dot(y_ref[...], pw_ref[...]) + pb_ref[...]).astype(BF16)


def _conv_mixer(proj, cw, conv_b, ln_g, ln_b, pw_w, pw_b, l):
    b, s, _ = proj.shape
    ts = min(CONV_TS, s)
    hb = ts // CONV_HALO
    ba = _PACK_OFF['c_a'] // W_MIX
    bg = _PACK_OFF['c_g'] // W_MIX
    halo_map = lambda blk: (lambda bi, i: (bi, jnp.maximum(i * hb - 1, 0), blk))
    return pl.pallas_call(
        _conv_kernel,
        out_shape=jax.ShapeDtypeStruct((b, s, W_MIX), BF16),
        grid=(b, s // ts),
        in_specs=[
            pl.BlockSpec((1, ts, W_MIX), lambda bi, i: (bi, i, ba)),
            pl.BlockSpec((1, ts, W_MIX), lambda bi, i: (bi, i, bg)),
            pl.BlockSpec((1, CONV_HALO, W_MIX), halo_map(ba)),
            pl.BlockSpec((1, CONV_HALO, W_MIX), halo_map(bg)),
            _of_layer(cw, l), _of_layer(conv_b, l), _of_layer(ln_g, l), _of_layer(ln_b, l),
            _of_layer(pw_w, l), _of_layer(pw_b, l),
        ],
        out_specs=pl.BlockSpec((1, ts, W_MIX), lambda bi, i: (bi, i, 0)),
        scratch_shapes=[pltpu.VMEM((ts + CONV_HALO, W_MIX), F32),
                        pltpu.VMEM((SUBLANES - 1, ts + CONV_HALO - SUBLANES, W_MIX), F32),
                        pltpu.VMEM((ts, W_MIX), BF16)],
        compiler_params=_cparams(("parallel", "parallel")),
        name="conv_mixer",
    )(proj, proj, proj, proj, cw, conv_b, ln_g, ln_b, pw_w, pw_b)


def _compress_kernel(r_ref, pos_ref, w1_ref, w2_ref, o_ref):
    r = r_ref[0]
    half = r.shape[1]
    n = r.shape[0]
    top = _dot(r, w1_ref[0:half, :])
    bot = _dot(r, w1_ref[half:2 * half, :])
    bot_next = pltpu.roll(bot, n - 1, 0)
    posb = _dot(jnp.broadcast_to(pos_ref[...], (8, 2 * half)), w1_ref[...])[0:1, :]
    h = jax.nn.gelu(top + bot_next + posb)
    out = _dot(h.astype(BF16), w2_ref[...])
    row = lax.broadcasted_iota(I32, out.shape, 0)
    o_ref[0] = jnp.where(row < n - 1, out, 0.0).astype(BF16)


def _compress(raw, pos, w1, w2, l):
    b, s, d = raw.shape
    n = s // CMP_STRIDE
    r = raw.reshape(b, n, CMP_STRIDE * d)
    return pl.pallas_call(
        _compress_kernel,
        out_shape=jax.ShapeDtypeStruct((b, n, d), BF16),
        grid=(b,),
        in_specs=[pl.BlockSpec((1, n, CMP_STRIDE * d), lambda bi: (bi, 0, 0)),
                  _of_layer(pos, l), _of_layer(w1, l), _of_layer(w2, l)],
        out_specs=pl.BlockSpec((1, n, d), lambda bi: (bi, 0, 0)),
        compiler_params=_cparams(("parallel",)),
        name="nsa_compress",
    )(r, pos, w1, w2)


ATT_TQ = 256
ATT_TK = 512
POS_RADIX = 64
BLOCK_COL = 64


def _pos_features(s_len, block_onehot=False):
    assert s_len <= POS_RADIX * 256
    s = np.arange(s_len)
    f = np.zeros((s_len, LANES), np.float32)
    f[:, 0] = f[:, 1] = s // POS_RADIX
    f[:, 2] = f[:, 3] = s % POS_RADIX
    if block_onehot:
        assert s_len // SEL_BLOCK <= LANES - BLOCK_COL
        f[s, BLOCK_COL + s // SEL_BLOCK] = 1.0
    return jnp.asarray(f, BF16)


def _slope_features(slopes, tq):
    f = np.zeros((len(slopes) * tq, LANES), np.float32)
    for h, sl in enumerate(slopes):
        c = np.float32(sl * LOG2E)
        ca = np.float32(np.asarray(c, dtype=BF16))
        cb = np.float32(np.asarray(c - ca, dtype=BF16))
        f[h * tq:(h + 1) * tq, 0:4] = [POS_RADIX * ca, POS_RADIX * cb, ca, cb]
    return jnp.asarray(f, BF16)


def _ones_feature(n):
    f = np.zeros((n, LANES), np.float32)
    f[:, 0] = 1.0
    return jnp.asarray(f, BF16)


def _stack_queries(q_ref, qf_ref, q4_ref):
    tq = q_ref.shape[1]
    for h in range(N_HEADS):
        q4_ref[h * tq:(h + 1) * tq, 0:HEAD_DIM] = q_ref[0, :, h * HEAD_DIM:(h + 1) * HEAD_DIM]
    q4_ref[:, HEAD_DIM:2 * HEAD_DIM] = qf_ref[...]


def _flash_attention(q4_ref, acc_ref, p_ref, a_ref, n_kt, keys_fn, vals_fn, tq, mask_fn=None,
                     last_mask_fn=None):
    acc_ref[...] = jnp.zeros(acc_ref.shape, F32)
    p_ref[...] = jnp.zeros(p_ref.shape, BF16)
    heads = [slice(h * tq, (h + 1) * tq) for h in range(N_HEADS)]

    def tile_logits(kt):
        kk = keys_fn(kt)
        if mask_fn is None:
            return lambda rows: _dot_nt(q4_ref[rows, :], kk)
        mask = mask_fn(kt)
        return lambda rows: jnp.where(mask, _dot_nt(q4_ref[rows, :], kk), NEG)

    def step(kt, ms, last):
        vv = vals_fn(jnp.maximum(kt - 1, 0))
        next_logits = None if last else tile_logits(kt + 1)
        last_mask = last_mask_fn() if (last and last_mask_fn) else None
        new_ms = []
        for h, rows in enumerate(heads):
            ah = a_ref[rows, :]
            if last_mask is not None:
                ah = jnp.where(last_mask, ah, NEG)
            pv = _dot(p_ref[rows, :], vv)
            if not last:
                a_ref[rows, :] = next_logits(rows)
            m_new = jnp.maximum(ms[h], jnp.max(ah, axis=-1, keepdims=True))
            p_ref[rows, :] = jnp.exp2(ah - m_new).astype(BF16)
            acc_ref[rows, :] = jnp.exp2(ms[h] - m_new) * (acc_ref[rows, :] + pv)
            new_ms.append(m_new)
        return tuple(new_ms)

    first_logits = tile_logits(0)
    for rows in heads:
        a_ref[rows, :] = first_logits(rows)
    m0 = tuple(jnp.full((tq, 1), NEG, F32) for _ in range(N_HEADS))
    ms = lax.fori_loop(0, n_kt - 1, lambda kt, ms: step(kt, ms, False), m0)
    step(n_kt - 1, ms, True)
    vv = vals_fn(n_kt - 1)
    for rows in heads:
        acc_ref[rows, :] += _dot(p_ref[rows, :], vv)


def _softmax2_rows(a, mask):
    a = jnp.where(mask, a, NEG)
    m = jnp.max(a, axis=-1, keepdims=True)
    e = jnp.where(mask, jnp.exp2(a - m), 0.0)
    s = jnp.sum(e, axis=-1, keepdims=True)
    return e / jnp.maximum(s, 1e-30)


def _split3(x):
    hi = x.astype(BF16)
    r1 = x - hi.astype(F32)
    mid = r1.astype(BF16)
    lo = (r1 - mid.astype(F32)).astype(BF16)
    return hi, mid, lo


def _nsa_kernel(q_ref, sm_ref, kc_ref, vc_ref, sk_ref, sv_ref, wk_ref, wv_ref, pf_ref, qf_ref, vf_ref,
                o_ref, q4_ref, p4_ref, acc_ref, selm_ref, p_ref, a_ref, ow_ref):
    tq = q_ref.shape[1]
    s_len = sk_ref.shape[1]
    n_cmp = kc_ref.shape[1]
    n_sel = s_len // SEL_BLOCK
    assert n_sel <= LANES - BLOCK_COL
    n_top = min(SEL_TOPK, n_sel)
    t0 = pl.program_id(1) * tq
    _stack_queries(q_ref, qf_ref, q4_ref)
    t_col = t0 + lax.broadcasted_iota(I32, (tq, 1), 0)

    a_all = _dot_nt(q4_ref[:, 0:HEAD_DIM], kc_ref[0])
    c_idx = lax.broadcasted_iota(I32, (1, n_cmp), 1)
    cd = t_col - (c_idx * CMP_STRIDE + (CMP_BLOCK - 1))
    cmask = (cd >= 0) & (c_idx < n_cmp - 1)
    cdf = cd.astype(F32)
    p_sum = jnp.zeros((tq, n_cmp), F32)
    for h in range(N_HEADS):
        p = _softmax2_rows(a_all[h * tq:(h + 1) * tq] - (SLOPES_NSA[h] * LOG2E) * cdf, cmask)
        p_sum = p_sum + p
        p4_ref[h * tq:(h + 1) * tq, 0:n_cmp] = p.astype(BF16)
    o_cmp = _dot(p4_ref[:, 0:n_cmp], vc_ref[0])

    wlen = min(NSA_WINDOW + tq, s_len)
    ks = pl.multiple_of(jnp.maximum(t0 + tq - wlen, 0), LANES)
    kw = jnp.concatenate([wk_ref[0, pl.ds(ks, wlen), :], pf_ref[pl.ds(ks, wlen), :]], axis=1)
    vw = jnp.concatenate([wv_ref[0, pl.ds(ks, wlen), :], vf_ref[0:wlen, :]], axis=1)
    wd = t_col - (ks + lax.broadcasted_iota(I32, (1, wlen), 1))
    wmask = (wd >= 0) & (wd < NSA_WINDOW)
    for h in range(N_HEADS):
        rows = slice(h * tq, (h + 1) * tq)
        ah = jnp.where(wmask, _dot_nt(q4_ref[rows, :], kw), NEG)
        m = jnp.max(ah, axis=-1, keepdims=True)
        ow_ref[rows, :] = _dot(jnp.exp2(ah - m).astype(BF16), vw)

    n_selp = selm_ref.shape[0]
    jj = lax.broadcasted_iota(I32, (n_selp, n_cmp), 0)
    cc = lax.broadcasted_iota(I32, (n_selp, n_cmp), 1)
    c_start = cc * CMP_STRIDE
    overlap = ((c_start < (jj + 1) * SEL_BLOCK) & (c_start + (CMP_BLOCK - 1) >= jj * SEL_BLOCK)
               & (cc < n_cmp - 1))
    ov = jnp.where(overlap, 1.0, 0.0).astype(BF16)
    hi, mid, lo = _split3(p_sum)
    imp = _dot_nt(ov, hi) + _dot_nt(ov, mid) + _dot_nt(ov, lo)
    j_col = lax.broadcasted_iota(I32, (n_selp, 1), 0)
    t_blk = (t0 + lax.broadcasted_iota(I32, (1, tq), 1)) // SEL_BLOCK
    forced = (j_col == 0) | (j_col == t_blk) | (j_col == t_blk - 1)
    imp = jnp.where(forced, jnp.inf, imp)
    imp = jnp.where(j_col <= t_blk, imp, -jnp.inf)
    rank = jnp.zeros((n_selp, tq), F32)
    for i2 in range(n_sel):
        ci = imp[i2:i2 + 1, :]
        tie_first = jnp.where(j_col > i2, 1.0, 0.0)
        rank = rank + jnp.where(ci > imp, 1.0, jnp.where(ci == imp, tie_first, 0.0))
    selm_ref[...] = jnp.where((rank < n_top) & (j_col < n_sel), 0.0, NEG)
    sel_bias = selm_ref[...].T[:, 0:LANES - BLOCK_COL].astype(BF16)
    for h in range(N_HEADS):
        q4_ref[h * tq:(h + 1) * tq, HEAD_DIM + BLOCK_COL:2 * HEAD_DIM] = sel_bias

    tk = min(ATT_TK, s_len)
    n_kt = (t0 + tq - 1) // tk + 1

    def sel_keys(kt):
        s0 = pl.multiple_of(kt * tk, tk)
        return jnp.concatenate([sk_ref[0, pl.ds(s0, tk), :], pf_ref[pl.ds(s0, tk), :]], axis=1)

    def sel_vals(kt):
        s0 = pl.multiple_of(kt * tk, tk)
        return jnp.concatenate([sv_ref[0, pl.ds(s0, tk), :], vf_ref[0:tk, :]], axis=1)

    def causal_last():
        return (n_kt - 1) * tk + lax.broadcasted_iota(I32, (1, tk), 1) <= t_col

    _flash_attention(q4_ref, acc_ref, p_ref, a_ref, n_kt, sel_keys, sel_vals, tq, last_mask_fn=causal_last)

    gates = jax.nn.sigmoid(sm_ref[0].astype(F32))
    for h in range(N_HEADS):
        rows = slice(h * tq, (h + 1) * tq)
        o_slc = acc_ref[rows, 0:HEAD_DIM] / jnp.maximum(acc_ref[rows, HEAD_DIM:HEAD_DIM + 1], 1e-30)
        o_w = ow_ref[rows, 0:HEAD_DIM] / jnp.maximum(ow_ref[rows, HEAD_DIM:HEAD_DIM + 1], 1e-30)
        g0 = gates[:, NG_LANE + 3 * h:NG_LANE + 3 * h + 1]
        g1 = gates[:, NG_LANE + 3 * h + 1:NG_LANE + 3 * h + 2]
        g2 = gates[:, NG_LANE + 3 * h + 2:NG_LANE + 3 * h + 3]
        o = g0 * o_cmp[rows] + g1 * o_slc + g2 * o_w
        o_ref[0, :, h * HEAD_DIM:(h + 1) * HEAD_DIM] = o.astype(BF16)


def _nsa_mixer(proj, k_cmp, v_cmp):
    b, s, _ = proj.shape
    tq = min(ATT_TQ, s)
    tk = min(ATT_TK, s)
    n_cmp = k_cmp.shape[1]
    wlen = min(NSA_WINDOW + tq, s)
    col = lambda name: _PACK_OFF[name] // HEAD_DIM
    full = lambda name: pl.BlockSpec((1, s, HEAD_DIM), functools.partial(lambda c, bi, i: (bi, 0, c), col(name)))
    nv = max(tk, wlen)
    return pl.pallas_call(
        _nsa_kernel,
        out_shape=jax.ShapeDtypeStruct((b, s, W_MIX), BF16),
        grid=(b, s // tq),
        in_specs=[
            pl.BlockSpec((1, tq, W_MIX), lambda bi, i: (bi, i, _PACK_OFF['n_q'] // W_MIX)),
            pl.BlockSpec((1, tq, LANES), lambda bi, i: (bi, i, SMALL_OFF // LANES)),
            pl.BlockSpec((1, n_cmp, HEAD_DIM), lambda bi, i: (bi, 0, 0)),
            pl.BlockSpec((1, n_cmp, HEAD_DIM), lambda bi, i: (bi, 0, 0)),
            full('n_sk'), full('n_sv'), full('n_wk'), full('n_wv'),
            _resident((s, LANES)), _resident((N_HEADS * tq, LANES)), _resident((nv, LANES)),
        ],
        out_specs=pl.BlockSpec((1, tq, W_MIX), lambda bi, i: (bi, i, 0)),
        scratch_shapes=[pltpu.VMEM((N_HEADS * tq, 2 * HEAD_DIM), BF16),
                        pltpu.VMEM((N_HEADS * tq, n_cmp), BF16),
                        pltpu.VMEM((N_HEADS * tq, 2 * HEAD_DIM), F32),
                        pltpu.VMEM((((s // SEL_BLOCK + LANES - 1) // LANES) * LANES, tq), F32),
                        pltpu.VMEM((N_HEADS * tq, tk), BF16),
                        pltpu.VMEM((N_HEADS * tq, tk), F32),
                        pltpu.VMEM((N_HEADS * tq, 2 * HEAD_DIM), F32)],
        compiler_params=_cparams(("parallel", "arbitrary")),
        name="nsa_attention",
    )(proj, proj, k_cmp, v_cmp, proj, proj, proj, proj,
      _pos_features(s, block_onehot=True), _slope_features(SLOPES_NSA, tq), _ones_feature(nv))


I16 = jnp.int16
I16_MIN = -2 ** 15
SEL_ROWS = 64
SCORE_CHAINS = 4


def _dsa_kernel(q_ref, iq_ref, sm_ref, smf_ref, k_ref, v_ref, pf_ref, qf_ref, vf_ref, o_ref,
                q4_ref, acc_ref, hi_ref, lo_ref, lq_ref, selb_ref, p_ref, a_ref):
    tq = q_ref.shape[1]
    s_len = k_ref.shape[1]
    topk = min(DSA_TOPK_MAX, s_len // 4)
    tk = min(ATT_TK, s_len)
    n_ch = tk // SEL_ROWS
    t0 = pl.program_id(1) * tq
    n_kt = (t0 + tq - 1) // tk + 1
    _stack_queries(q_ref, qf_ref, q4_ref)
    t_row = t0 + lax.broadcasted_iota(I32, (1, tq), 1)
    iq = iq_ref[0]
    iw_t = sm_ref[0].astype(F32).T
    one = jnp.ones((), BF16)
    zero = jnp.zeros((), BF16)

    def score_body(kt, _):
        tc = tk // SCORE_CHAINS
        for c in range(SCORE_CHAINS):
            s0 = pl.multiple_of(kt * tk + c * tc, tc)
            ik = smf_ref[0, pl.ds(s0, tc), IK_LANE:IK_LANE + IDX_DIM]
            sc = jnp.zeros((tc, tq), F32)
            for h in range(IDX_HEADS):
                lg = _dot_nt(ik, iq[:, h * IDX_DIM:(h + 1) * IDX_DIM])
                sc = sc + jnp.maximum(lg, 0.0) * iw_t[IW_LANE + h:IW_LANE + h + 1, :]
            sc = jnp.where(sc == 0.0, 0.0, sc)
            bits = lax.bitcast_convert_type(sc, I32)
            key = bits ^ ((bits >> 31) & 0x7FFFFFFF)
            s_pos = s0 + lax.broadcasted_iota(I32, (tc, 1), 0)
            key = jnp.where(s_pos <= t_row, key, INT_MIN)
            hi_ref[pl.ds(s0, tc), :] = (key >> 16).astype(I16)
            lo_ref[pl.ds(s0, tc), :] = ((key & 0xFFFF) + I16_MIN).astype(I16)
        return 0

    lax.fori_loop(0, n_kt, score_body, 0)

    def count_ge(ref, thr_row):
        thr = jnp.broadcast_to(thr_row, (SEL_ROWS, tq))

        def body(kt, cnt):
            s0 = pl.multiple_of(kt * tk, tk)
            for c in range(n_ch):
                cnt = cnt + jnp.where(ref[pl.ds(s0 + c * SEL_ROWS, SEL_ROWS), :] >= thr, one, zero)
            return cnt
        cnt = lax.fori_loop(0, n_kt, body, jnp.zeros((SEL_ROWS, tq), BF16))
        return jnp.sum(cnt.astype(F32), axis=0, keepdims=True)

    def kth_largest(ref, k):
        def bit_body(it, carry):
            cand, cnt = carry
            trial = cand | jnp.left_shift(jnp.int32(1), 15 - it)
            total = count_ge(ref, (trial + I16_MIN).astype(I16))
            ok = total >= k
            return jnp.where(ok, trial, cand), jnp.where(ok, total, cnt)
        return lax.fori_loop(0, 16, bit_body, (jnp.zeros((1, tq), I32), jnp.zeros((1, tq), F32)))

    raw_hi, n_ge_hi = kth_largest(hi_ref, float(topk))
    cand_hi = jnp.maximum(raw_hi, 1)
    p16 = jnp.broadcast_to((cand_hi + I16_MIN).astype(I16), (SEL_ROWS, tq))
    n_above = count_ge(hi_ref, (jnp.minimum(cand_hi + 1, 2 ** 16 - 1) + I16_MIN).astype(I16))
    need = float(topk) - n_above

    def tie_body(kt, _):
        s0 = pl.multiple_of(kt * tk, tk)
        for c in range(n_ch):
            ds = pl.ds(s0 + c * SEL_ROWS, SEL_ROWS)
            lq_ref[ds, :] = jnp.where(hi_ref[ds, :] == p16, lo_ref[ds, :], jnp.full((), I16_MIN, I16))
        return 0

    lax.fori_loop(0, n_kt, tie_body, 0)
    cand_lo, n_ge_lo = kth_largest(lq_ref, need)
    q16_row = (cand_lo + I16_MIN).astype(I16)
    q16 = jnp.broadcast_to(q16_row, (SEL_ROWS, tq))

    def sel_body(kt, _):
        s0 = pl.multiple_of(kt * tk, tk)
        for c in range(n_ch):
            ds = pl.ds(s0 + c * SEL_ROWS, SEL_ROWS)
            hi = hi_ref[ds, :]
            tie = jnp.where(hi == p16, jnp.where(lo_ref[ds, :] >= q16, one, zero), zero)
            selb_ref[ds, :] = jnp.where(hi > p16, one, tie)
        return 0

    lax.fori_loop(0, n_kt, sel_body, 0)

    n_gt = jnp.where(cand_lo >= 2 ** 16 - 1, 0.0,
                     count_ge(lq_ref, (jnp.minimum(cand_lo + 1, 2 ** 16 - 1) + I16_MIN).astype(I16)))
    n_with_hi = jnp.where(raw_hi > 0, n_ge_hi - n_above, 0.0)
    n_ge = jnp.where(cand_lo > 0, n_ge_lo, n_with_hi)
    quota = need - n_gt
    any_tied = jnp.max(jnp.where(n_ge - n_gt > quota, 1.0, 0.0)) > 0.0

    @pl.when(any_tied)
    def _():
        p_t = jnp.broadcast_to(p16[0:1, :], (tk, tq))
        q_t = jnp.broadcast_to(q16_row, (tk, tq))
        lower = jnp.where(lax.broadcasted_iota(I32, (tk, tk), 0) >= lax.broadcasted_iota(I32, (tk, tk), 1),
                          1.0, 0.0).astype(BF16)

        def exact_ties(kt):
            ds = pl.ds(pl.multiple_of(kt * tk, tk), tk)
            exact = jnp.where(hi_ref[ds, :] == p_t, jnp.where(lo_ref[ds, :] == q_t, one, zero), zero)
            return ds, exact, _dot(lower, exact)

        def drop(ds, exact, rank):
            selb_ref[ds, :] = selb_ref[ds, :] - jnp.where(rank > quota, 1.0, 0.0).astype(BF16) * exact

        def drop_pair(j, seen):
            ds_a, exact_a, in_a = exact_ties(2 * j)
            ds_b, exact_b, in_b = exact_ties(2 * j + 1)
            rank_a = in_a + seen
            rank_b = in_b + rank_a[tk - 1:tk, :]
            drop(ds_a, exact_a, rank_a)
            drop(ds_b, exact_b, rank_b)
            return rank_b[tk - 1:tk, :]

        seen = lax.fori_loop(0, n_kt // 2, drop_pair, jnp.zeros((1, tq), F32))

        @pl.when(n_kt % 2 == 1)
        def _():
            ds, exact, in_tile = exact_ties(n_kt - 1)
            drop(ds, exact, in_tile + seen)

    def att_keys(kt):
        s0 = pl.multiple_of(kt * tk, tk)
        return jnp.concatenate([k_ref[0, pl.ds(s0, tk), :], pf_ref[pl.ds(s0, tk), :]], axis=1)

    def att_vals(kt):
        s0 = pl.multiple_of(kt * tk, tk)
        return jnp.concatenate([v_ref[0, pl.ds(s0, tk), :], vf_ref[...]], axis=1)

    def att_mask(kt):
        s0 = pl.multiple_of(kt * tk, tk)
        return selb_ref[pl.ds(s0, tk), :].astype(F32).T > 0.5

    _flash_attention(q4_ref, acc_ref, p_ref, a_ref, n_kt, att_keys, att_vals, tq, mask_fn=att_mask)
    for h in range(N_HEADS):
        rows = slice(h * tq, (h + 1) * tq)
        o = acc_ref[rows, 0:HEAD_DIM] / jnp.maximum(acc_ref[rows, HEAD_DIM:HEAD_DIM + 1], 1e-30)
        o_ref[0, :, h * HEAD_DIM:(h + 1) * HEAD_DIM] = o.astype(BF16)


def _dsa_mixer(proj):
    b, s, _ = proj.shape
    tq = min(ATT_TQ, s)
    tk = min(ATT_TK, s)
    full = lambda off: pl.BlockSpec((1, s, LANES), functools.partial(lambda c, bi, i: (bi, 0, c), off // LANES))
    return pl.pallas_call(
        _dsa_kernel,
        out_shape=jax.ShapeDtypeStruct((b, s, W_MIX), BF16),
        grid=(b, s // tq),
        in_specs=[
            pl.BlockSpec((1, tq, W_MIX), lambda bi, i: (bi, i, _PACK_OFF['d_q'] // W_MIX)),
            pl.BlockSpec((1, tq, W_MIX), lambda bi, i: (bi, i, _PACK_OFF['i_q'] // W_MIX)),
            pl.BlockSpec((1, tq, LANES), lambda bi, i: (bi, i, SMALL_OFF // LANES)),
            full(SMALL_OFF), full(_PACK_OFF['d_k']), full(_PACK_OFF['d_v']),
            _resident((s, LANES)), _resident((N_HEADS * tq, LANES)), _resident((tk, LANES)),
        ],
        out_specs=pl.BlockSpec((1, tq, W_MIX), lambda bi, i: (bi, i, 0)),
        scratch_shapes=[pltpu.VMEM((N_HEADS * tq, 2 * HEAD_DIM), BF16),
                        pltpu.VMEM((N_HEADS * tq, 2 * HEAD_DIM), F32),
                        pltpu.VMEM((s, tq), I16), pltpu.VMEM((s, tq), I16), pltpu.VMEM((s, tq), I16),
                        pltpu.VMEM((s, tq), BF16),
                        pltpu.VMEM((N_HEADS * tq, tk), BF16),
                        pltpu.VMEM((N_HEADS * tq, tk), F32)],
        compiler_params=_cparams(("parallel", "arbitrary")),
        name="dsa_attention",
    )(proj, proj, proj, proj, proj, proj,
      _pos_features(s), _slope_features(SLOPES_DSA, tq), _ones_feature(tk))


OUTPROJ_TM = 512
LN_ROWS = 256


def _outproj_kernel(y0_ref, y1_ref, y2_ref, y3_ref, w_ref, x_ref, g_ref, b_ref, o_ref, ob_ref):
    for r0 in range(0, x_ref.shape[0], LN_ROWS):
        rows = slice(r0, r0 + LN_ROWS)
        acc = DN_ALPHA * x_ref[rows, :]
        for gi, y_ref in enumerate((y0_ref, y1_ref, y2_ref, y3_ref)):
            acc = acc + _dot(y_ref[rows, :], w_ref[gi * W_MIX:(gi + 1) * W_MIX, :])
        y = _layernorm(acc, g_ref[...], b_ref[...])
        o_ref[rows, :] = y
        ob_ref[rows, :] = y.astype(BF16)


def _outproj_ln(ys, w_out, x, g, b, l):
    n_tok, d = x.shape
    tm = OUTPROJ_TM
    yspec = pl.BlockSpec((tm, W_MIX), lambda i: (i, 0))
    xspec = pl.BlockSpec((tm, d), lambda i: (i, 0))
    return pl.pallas_call(
        _outproj_kernel,
        out_shape=(jax.ShapeDtypeStruct((n_tok, d), F32), jax.ShapeDtypeStruct((n_tok, d), BF16)),
        grid=(n_tok // tm,),
        in_specs=[yspec, yspec, yspec, yspec, _of_layer(w_out, l), xspec, _of_layer(g, l), _of_layer(b, l)],
        out_specs=(xspec, xspec),
        compiler_params=_cparams(("parallel",)),
        name="outproj_ln",
    )(*ys, w_out, x, g, b)


FFN_UP_TM = 1024
FFN_TF = 512
FFN_DOWN_TM = 256


def _ffn_up_kernel(xb_ref, wg_ref, wu_ref, h_ref):
    xb = xb_ref[...]
    gate = _dot(xb, wg_ref[...])
    up = _dot(xb, wu_ref[...])
    h_ref[...] = (gate * jax.nn.sigmoid(gate) * up).astype(BF16)


def _ffn_down_kernel(h_ref, wd_ref, x_ref, g_ref, b_ref, *rest):
    *out_refs, acc_a, acc_b = rest
    i = pl.program_id(0)

    @pl.when(i == 0)
    def _():
        acc_b[...] = jnp.zeros(acc_b.shape, F32)

    def step(acc_prev, acc_next):
        y = _layernorm(acc_prev[...], g_ref[...], b_ref[...])
        out_refs[0][...] = y
        if len(out_refs) > 1:
            out_refs[1][...] = y.astype(BF16)
        acc_next[...] = DN_ALPHA * x_ref[...] + _dot(h_ref[...], wd_ref[...])

    @pl.when(i % 2 == 0)
    def _():
        step(acc_b, acc_a)

    @pl.when(i % 2 == 1)
    def _():
        step(acc_a, acc_b)


def _ffn_ln(xb, w_gate_up, w_down, x, g, b, l, want_bf16):
    n_tok, d = x.shape
    d_ff = w_down.shape[1]
    tm, tf = FFN_UP_TM, FFN_TF
    nf = d_ff // tf
    h = pl.pallas_call(
        _ffn_up_kernel,
        out_shape=jax.ShapeDtypeStruct((n_tok, d_ff), BF16),
        grid=(nf, n_tok // tm),
        in_specs=[
            pl.BlockSpec((tm, d), lambda f, i: (i, 0)),
            pl.BlockSpec((None, d, tf), lambda f, i: (l, 0, f)),
            pl.BlockSpec((None, d, tf), lambda f, i: (l, 0, f + nf)),
        ],
        out_specs=pl.BlockSpec((tm, tf), lambda f, i: (i, f)),
        compiler_params=_cparams(("parallel", "parallel")),
        name="ffn_up",
    )(xb, w_gate_up, w_gate_up)
    tm = FFN_DOWN_TM
    n_tiles = n_tok // tm
    in_row = lambda i: (jnp.minimum(i, n_tiles - 1), 0)
    out_row = lambda i: (jnp.maximum(i - 1, 0), 0)
    out_dtypes = (F32, BF16) if want_bf16 else (F32,)
    outs = pl.pallas_call(
        _ffn_down_kernel,
        out_shape=tuple(jax.ShapeDtypeStruct((n_tok, d), t) for t in out_dtypes),
        grid=(n_tiles + 1,),
        in_specs=[pl.BlockSpec((tm, d_ff), in_row), _of_layer(w_down, l),
                  pl.BlockSpec((tm, d), in_row), _of_layer(g, l), _of_layer(b, l)],
        out_specs=tuple(pl.BlockSpec((tm, d), out_row) for _ in out_dtypes),
        scratch_shapes=[pltpu.VMEM((tm, d), F32), pltpu.VMEM((tm, d), F32)],
        compiler_params=_cparams(("arbitrary",)),
        name="ffn_down_ln",
    )(h, w_down, x, g, b)
    return outs if want_bf16 else (outs[0], None)


PACK_ROWS = 256
_FOLD = {'d_q': ATTN_SCALE * LOG2E, 'n_q': ATTN_SCALE * LOG2E, 'i_w': IDX_SCALE}


def _pack_kernel(w_ref, o_ref):
    for n in _PACK_ORDER:
        src, dst, size = _SRC_OFF[n], _PACK_OFF[n], _SRC_SIZE[n]
        o_ref[:, dst:dst + size] = (w_ref[:, src:src + size] * _FOLD.get(n, 1.0)).astype(BF16)
    used = sum(_SRC_SIZES)
    o_ref[:, used:D_PACK] = jnp.zeros((o_ref.shape[0], D_PACK - used), BF16)


def _pack_w_in(w):
    n_layers, d, d_in = w.shape
    return pl.pallas_call(
        _pack_kernel,
        out_shape=jax.ShapeDtypeStruct((n_layers, d, D_PACK), BF16),
        grid=(n_layers, d // PACK_ROWS),
        in_specs=[pl.BlockSpec((None, PACK_ROWS, d_in), lambda l, i: (l, i, 0))],
        out_specs=pl.BlockSpec((None, PACK_ROWS, D_PACK), lambda l, i: (l, i, 0)),
        compiler_params=_cparams(("parallel", "parallel")),
        name="pack_w_in",
    )(w)


def _prepare(p):
    row = lambda v: v[:, None, :].astype(F32)
    n_layers = p['w_in'].shape[0]
    flat = lambda v: v.reshape(n_layers, 1, -1).astype(BF16)
    return dict(
        w_pack=_pack_w_in(p['w_in']), w_out=p['w_out'].astype(BF16),
        pool_w=p['pool_w'].astype(BF16), pool_scale=row(p['pool_scale']),
        conv_w=jnp.concatenate([p['conv_w'].astype(F32), jnp.zeros((n_layers, 1, W_MIX), F32)], axis=1),
        conv_b=row(p['conv_b']), conv_ln_g=row(p['conv_ln_g']), conv_ln_b=row(p['conv_ln_b']),
        conv_pw_w=p['conv_pw_w'].astype(BF16), conv_pw_b=row(p['conv_pw_b']),
        cmp_pos_k=flat(p['cmp_pos_k']), cmp_pos_v=flat(p['cmp_pos_v']),
        cmp_k_w1=p['cmp_k_w1'].astype(BF16), cmp_k_w2=p['cmp_k_w2'].astype(BF16),
        cmp_v_w1=p['cmp_v_w1'].astype(BF16), cmp_v_w2=p['cmp_v_w2'].astype(BF16),
        ln1_g=row(p['ln1_g']), ln1_b=row(p['ln1_b']), ln2_g=row(p['ln2_g']), ln2_b=row(p['ln2_b']),
        w_gate_up=p['w_gate_up'].astype(BF16), w_down=p['w_down'].astype(BF16))


def _layer(x, xb, p, l, last):
    b, s, d = x.shape
    n_tok = b * s
    x_in = x if xb is None else xb
    proj, ck, cv = _inproj(x_in.reshape(n_tok, d), p['w_pack'], l)
    proj = proj.reshape(b, s, D_PACK)
    y_pool = _pool_mixer(proj, p['pool_w'], p['pool_scale'], l)
    y_conv = _conv_mixer(proj, p['conv_w'], p['conv_b'], p['conv_ln_g'], p['conv_ln_b'],
                         p['conv_pw_w'], p['conv_pw_b'], l)
    y_dsa = _dsa_mixer(proj)
    k_cmp = _compress(ck.reshape(b, s, HEAD_DIM), p['cmp_pos_k'], p['cmp_k_w1'], p['cmp_k_w2'], l)
    v_cmp = _compress(cv.reshape(b, s, HEAD_DIM), p['cmp_pos_v'], p['cmp_v_w1'], p['cmp_v_w2'], l)
    y_nsa = _nsa_mixer(proj, k_cmp, v_cmp)
    ys = [y.reshape(n_tok, W_MIX) for y in (y_pool, y_conv, y_dsa, y_nsa)]
    x1, x1b = _outproj_ln(ys, p['w_out'], x.reshape(n_tok, d), p['ln1_g'], p['ln1_b'], l)
    x2, x2b = _ffn_ln(x1b, p['w_gate_up'], p['w_down'], x1, p['ln2_g'], p['ln2_b'], l,
                      want_bf16=not last)
    return x2.reshape(b, s, d), None if last else x2b.reshape(b, s, d)


def kernel(x, w_in, w_out, pool_w, pool_scale, conv_w, conv_b, conv_ln_g, conv_ln_b, conv_pw_w, conv_pw_b,
           cmp_pos_k, cmp_pos_v, cmp_k_w1, cmp_k_w2, cmp_v_w1, cmp_v_w2, ln1_g, ln1_b, ln2_g, ln2_b,
           w_gate_up, w_down):
    params = dict(w_in=w_in, w_out=w_out, pool_w=pool_w, pool_scale=pool_scale, conv_w=conv_w, conv_b=conv_b,
                  conv_ln_g=conv_ln_g, conv_ln_b=conv_ln_b, conv_pw_w=conv_pw_w, conv_pw_b=conv_pw_b,
                  cmp_pos_k=cmp_pos_k, cmp_pos_v=cmp_pos_v, cmp_k_w1=cmp_k_w1, cmp_k_w2=cmp_k_w2,
                  cmp_v_w1=cmp_v_w1, cmp_v_w2=cmp_v_w2, ln1_g=ln1_g, ln1_b=ln1_b, ln2_g=ln2_g, ln2_b=ln2_b,
                  w_gate_up=w_gate_up, w_down=w_down)
    prepared = _prepare(params)
    xb = None
    for l in range(w_in.shape[0]):
        x, xb = _layer(x, xb, prepared, l, last=l == w_in.shape[0] - 1)
    return x
```

```python
import functools

import numpy as np
import jax
import jax.numpy as jnp
from jax import lax
from jax.experimental import pallas as pl
from jax.experimental.pallas import tpu as pltpu

F32 = jnp.float32
BF16 = jnp.bfloat16
I32 = jnp.int32

D_MODEL = 2048
DEPTH = 2
W_MIX = D_MODEL // 4
HEAD_DIM = 128
N_HEADS = W_MIX // HEAD_DIM
POOL_WINDOWS = (2, 4, 8, 16)
POOL_GROUP = W_MIX // len(POOL_WINDOWS)
CONV_WIDTH = 31
IDX_HEADS = 8
IDX_DIM = 64
DSA_TOPK_MAX = 256
CMP_BLOCK = 32
CMP_STRIDE = 16
SEL_BLOCK = 64
SEL_TOPK = 16
NSA_WINDOW = 512
D_FF = ((8 * D_MODEL + 3 * 256 - 1) // (3 * 256)) * 256
DN_ALPHA = (2 * DEPTH) ** 0.25
ATTN_SCALE = HEAD_DIM ** -0.5
IDX_SCALE = (IDX_HEADS * IDX_DIM) ** -0.5
LN_EPS = 1e-5
LOG2E = 1.4426950408889634

LANES = 128
SUBLANES = 8
VMEM_LIMIT = 48 * 1024 * 1024

NEG = -1e30
INT_MIN = -2 ** 31

_SRC_SIZES = (W_MIX, W_MIX, W_MIX, W_MIX, HEAD_DIM, HEAD_DIM, IDX_HEADS * IDX_DIM, IDX_DIM, IDX_HEADS,
              W_MIX, HEAD_DIM, HEAD_DIM, HEAD_DIM, HEAD_DIM, HEAD_DIM, HEAD_DIM, N_HEADS * 3)
_SRC_NAMES = ('pool', 'c_a', 'c_g', 'd_q', 'd_k', 'd_v', 'i_q', 'i_k', 'i_w',
              'n_q', 'n_ck', 'n_cv', 'n_sk', 'n_sv', 'n_wk', 'n_wv', 'n_g')
_SRC_OFF = dict(zip(_SRC_NAMES, np.concatenate([[0], np.cumsum(_SRC_SIZES)[:-1]]).tolist()))
_SRC_SIZE = dict(zip(_SRC_NAMES, _SRC_SIZES))
_PACK_ORDER = ('pool', 'c_a', 'c_g', 'd_q', 'i_q', 'n_q', 'd_k', 'd_v',
               'n_ck', 'n_cv', 'n_sk', 'n_sv', 'n_wk', 'n_wv', 'i_k', 'i_w', 'n_g')
_PACK_OFF = {}
_o = 0
for _n in _PACK_ORDER:
    _PACK_OFF[_n] = _o
    _o += _SRC_SIZE[_n]
D_PACK = ((_o + LANES - 1) // LANES) * LANES
SMALL_OFF = _PACK_OFF['i_k']
IK_LANE = 0
IW_LANE = _PACK_OFF['i_w'] - SMALL_OFF
NG_LANE = _PACK_OFF['n_g'] - SMALL_OFF


def _alibi_slopes():
    n = 2 * N_HEADS
    s = np.power(2.0, -8.0 * np.arange(1, n + 1) / n).astype(np.float32)
    return [float(v) for v in s[0::2]], [float(v) for v in s[1::2]]


SLOPES_DSA, SLOPES_NSA = _alibi_slopes()


def _cparams(sem):
    return pltpu.CompilerParams(dimension_semantics=sem, vmem_limit_bytes=VMEM_LIMIT)


def _resident(shape):
    nd = len(shape)
    return pl.BlockSpec(shape, lambda *_: (0,) * nd, pipeline_mode=pl.Buffered(1))


def _of_layer(arr, l):
    nd = arr.ndim
    return pl.BlockSpec((None,) + arr.shape[1:], lambda *_: (l,) + (0,) * (nd - 1),
                        pipeline_mode=pl.Buffered(1))


def _dot(a, b):
    return jnp.dot(a, b, preferred_element_type=F32)


def _dot_nt(a, b):
    return lax.dot_general(a, b, (((1,), (1,)), ((), ())), preferred_element_type=F32)


def _layernorm(x, g, b):
    mu = jnp.mean(x, axis=-1, keepdims=True)
    xc = x - mu
    var = jnp.mean(xc * xc, axis=-1, keepdims=True)
    return xc * lax.rsqrt(var + LN_EPS) * g + b


INPROJ_TM = 512
INPROJ_CHUNK = 512


def _inproj_kernel(x_ref, w_ref, o_ref, ck_ref, cv_ref):
    x = x_ref[...].astype(BF16)
    n = o_ref.shape[1]
    for c0 in range(0, n, INPROJ_CHUNK):
        c1 = min(c0 + INPROJ_CHUNK, n)
        o_ref[:, c0:c1] = _dot(x, w_ref[:, c0:c1]).astype(BF16)
    ck_ref[...] = o_ref[:, _PACK_OFF['n_ck']:_PACK_OFF['n_ck'] + HEAD_DIM]
    cv_ref[...] = o_ref[:, _PACK_OFF['n_cv']:_PACK_OFF['n_cv'] + HEAD_DIM]


def _inproj(x, w_pack, l):
    n_tok, d = x.shape
    tm = INPROJ_TM
    col = pl.BlockSpec((tm, HEAD_DIM), lambda i: (i, 0))
    return pl.pallas_call(
        _inproj_kernel,
        out_shape=(jax.ShapeDtypeStruct((n_tok, D_PACK), BF16),
                   jax.ShapeDtypeStruct((n_tok, HEAD_DIM), BF16), jax.ShapeDtypeStruct((n_tok, HEAD_DIM), BF16)),
        grid=(n_tok // tm,),
        in_specs=[pl.BlockSpec((tm, d), lambda i: (i, 0)), _of_layer(w_pack, l)],
        out_specs=(pl.BlockSpec((tm, D_PACK), lambda i: (i, 0)), col, col),
        compiler_params=_cparams(("parallel",)),
        name="inproj",
    )(x, w_pack)


POOL_TS = 512
POOL_HALO = 16


def _pool_kernel(u_ref, halo_ref, w_ref, sc_ref, o_ref, xs_ref):
    i = pl.program_id(1)
    ts = u_ref.shape[1]
    xs_ref[POOL_HALO:POOL_HALO + ts, :] = u_ref[0].astype(F32)
    xs_ref[0:POOL_HALO, :] = jnp.where(i > 0, halo_ref[0].astype(F32), 0.0)
    pos = i * ts + lax.broadcasted_iota(I32, (ts, 1), 0)
    for g, win in enumerate(POOL_WINDOWS):
        c = slice(g * POOL_GROUP, (g + 1) * POOL_GROUP)
        x = xs_ref[POOL_HALO:POOL_HALO + ts, c]
        acc = x
        for k in range(1, win):
            acc = acc + xs_ref[POOL_HALO - k:POOL_HALO - k + ts, c]
        cnt = jnp.minimum(pos + 1, win).astype(F32)
        d = acc / cnt - x
        y = _dot(d.astype(BF16), w_ref[g])
        o_ref[0, :, c] = (y * sc_ref[:, c]).astype(BF16)


def _pool_mixer(proj, pool_w, pool_scale, l):
    b, s, _ = proj.shape
    ts = min(POOL_TS, s)
    hb = ts // POOL_HALO
    blk = _PACK_OFF['pool'] // W_MIX
    return pl.pallas_call(
        _pool_kernel,
        out_shape=jax.ShapeDtypeStruct((b, s, W_MIX), BF16),
        grid=(b, s // ts),
        in_specs=[
            pl.BlockSpec((1, ts, W_MIX), lambda bi, i: (bi, i, blk)),
            pl.BlockSpec((1, POOL_HALO, W_MIX), lambda bi, i: (bi, jnp.maximum(i * hb - 1, 0), blk)),
            _of_layer(pool_w, l),
            _of_layer(pool_scale, l),
        ],
        out_specs=pl.BlockSpec((1, ts, W_MIX), lambda bi, i: (bi, i, 0)),
        scratch_shapes=[pltpu.VMEM((ts + POOL_HALO, W_MIX), F32)],
        compiler_params=_cparams(("parallel", "parallel")),
        name="pool_mixer",
    )(proj, proj, pool_w, pool_scale)


CONV_TS = 512
CONV_HALO = 32
CONV_ROWS = 32


def _conv_kernel(a_ref, g_ref, ha_ref, hg_ref, cw_ref, cb_ref, lg_ref, lb_ref, pw_ref, pb_ref,
                 o_ref, hs_ref, sh_ref, y_ref):
    i = pl.program_id(1)
    ts = a_ref.shape[1]
    hs_ref[CONV_HALO:CONV_HALO + ts, :] = a_ref[0].astype(F32) * jax.nn.sigmoid(g_ref[0].astype(F32))
    halo = ha_ref[0].astype(F32) * jax.nn.sigmoid(hg_ref[0].astype(F32))
    hs_ref[0:CONV_HALO, :] = jnp.where(i > 0, halo, 0.0)
    n_sh = ts + CONV_HALO - SUBLANES
    for b in range(1, SUBLANES):
        sh_ref[b - 1, 0:n_sh, :] = hs_ref[b:b + n_sh, :]
    base = CONV_HALO - (CONV_WIDTH - 1)
    for r0 in range(0, ts, CONV_ROWS):
        acc = jnp.broadcast_to(cb_ref[...], (CONV_ROWS, W_MIX))
        for j in range(CONV_WIDTH):
            a8, b = divmod(base + j, SUBLANES)
            r = r0 + a8 * SUBLANES
            src = hs_ref[r:r + CONV_ROWS, :] if b == 0 else sh_ref[b - 1, r:r + CONV_ROWS, :]
            acc = acc + src * cw_ref[j:j + 1, :]
        y = _layernorm(acc, lg_ref[...], lb_ref[...])
        y_ref[r0:r0 + CONV_ROWS, :] = (y * jax.nn.sigmoid(y)).astype(BF16)
    o_ref[0] = (_dot(y_ref[...], pw_ref[...]) + pb_ref[...]).astype(BF16)


def _conv_mixer(proj, cw, conv_b, ln_g, ln_b, pw_w, pw_b, l):
    b, s, _ = proj.shape
    ts = min(CONV_TS, s)
    hb = ts // CONV_HALO
    ba = _PACK_OFF['c_a'] // W_MIX
    bg = _PACK_OFF['c_g'] // W_MIX
    halo_map = lambda blk: (lambda bi, i: (bi, jnp.maximum(i * hb - 1, 0), blk))
    return pl.pallas_call(
        _conv_kernel,
        out_shape=jax.ShapeDtypeStruct((b, s, W_MIX), BF16),
        grid=(b, s // ts),
        in_specs=[
            pl.BlockSpec((1, ts, W_MIX), lambda bi, i: (bi, i, ba)),
            pl.BlockSpec((1, ts, W_MIX), lambda bi, i: (bi, i, bg)),
            pl.BlockSpec((1, CONV_HALO, W_MIX), halo_map(ba)),
            pl.BlockSpec((1, CONV_HALO, W_MIX), halo_map(bg)),
            _of_layer(cw, l), _of_layer(conv_b, l), _of_layer(ln_g, l), _of_layer(ln_b, l),
            _of_layer(pw_w, l), _of_layer(pw_b, l),
        ],
        out_specs=pl.BlockSpec((1, ts, W_MIX), lambda bi, i: (bi, i, 0)),
        scratch_shapes=[pltpu.VMEM((ts + CONV_HALO, W_MIX), F32),
                        pltpu.VMEM((SUBLANES - 1, ts + CONV_HALO - SUBLANES, W_MIX), F32),
                        pltpu.VMEM((ts, W_MIX), BF16)],
        compiler_params=_cparams(("parallel", "parallel")),
        name="conv_mixer",
    )(proj, proj, proj, proj, cw, conv_b, ln_g, ln_b, pw_w, pw_b)


def _compress_kernel(r_ref, pos_ref, w1_ref, w2_ref, o_ref):
    r = r_ref[0]
    half = r.shape[1]
    n = r.shape[0]
    top = _dot(r, w1_ref[0:half, :])
    bot = _dot(r, w1_ref[half:2 * half, :])
    bot_next = pltpu.roll(bot, n - 1, 0)
    posb = _dot(jnp.broadcast_to(pos_ref[...], (8, 2 * half)), w1_ref[...])[0:1, :]
    h = jax.nn.gelu(top + bot_next + posb)
    out = _dot(h.astype(BF16), w2_ref[...])
    row = lax.broadcasted_iota(I32, out.shape, 0)
    o_ref[0] = jnp.where(row < n - 1, out, 0.0).astype(BF16)


def _compress(raw, pos, w1, w2, l):
    b, s, d = raw.shape
    n = s // CMP_STRIDE
    r = raw.reshape(b, n, CMP_STRIDE * d)
    return pl.pallas_call(
        _compress_kernel,
        out_shape=jax.ShapeDtypeStruct((b, n, d), BF16),
        grid=(b,),
        in_specs=[pl.BlockSpec((1, n, CMP_STRIDE * d), lambda bi: (bi, 0, 0)),
                  _of_layer(pos, l), _of_layer(w1, l), _of_layer(w2, l)],
        out_specs=pl.BlockSpec((1, n, d), lambda bi: (bi, 0, 0)),
        compiler_params=_cparams(("parallel",)),
        name="nsa_compress",
    )(r, pos, w1, w2)


ATT_TQ = 256
ATT_TK = 512
POS_RADIX = 64
BLOCK_COL = 64


def _pos_features(s_len, block_onehot=False):
    assert s_len <= POS_RADIX * 256
    s = np.arange(s_len)
    f = np.zeros((s_len, LANES), np.float32)
    f[:, 0] = f[:, 1] = s // POS_RADIX
    f[:, 2] = f[:, 3] = s % POS_RADIX
    if block_onehot:
        assert s_len // SEL_BLOCK <= LANES - BLOCK_COL
        f[s, BLOCK_COL + s // SEL_BLOCK] = 1.0
    return jnp.asarray(f, BF16)


def _slope_features(slopes, tq):
    f = np.zeros((len(slopes) * tq, LANES), np.float32)
    for h, sl in enumerate(slopes):
        c = np.float32(sl * LOG2E)
        ca = np.float32(np.asarray(c, dtype=BF16))
        cb = np.float32(np.asarray(c - ca, dtype=BF16))
        f[h * tq:(h + 1) * tq, 0:4] = [POS_RADIX * ca, POS_RADIX * cb, ca, cb]
    return jnp.asarray(f, BF16)


def _ones_feature(n):
    f = np.zeros((n, LANES), np.float32)
    f[:, 0] = 1.0
    return jnp.asarray(f, BF16)


def _stack_queries(q_ref, qf_ref, q4_ref):
    tq = q_ref.shape[1]
    for h in range(N_HEADS):
        q4_ref[h * tq:(h + 1) * tq, 0:HEAD_DIM] = q_ref[0, :, h * HEAD_DIM:(h + 1) * HEAD_DIM]
    q4_ref[:, HEAD_DIM:2 * HEAD_DIM] = qf_ref[...]


def _flash_attention(q4_ref, acc_ref, p_ref, a_ref, n_kt, keys_fn, vals_fn, tq, mask_fn=None,
                     last_mask_fn=None):
    acc_ref[...] = jnp.zeros(acc_ref.shape, F32)
    p_ref[...] = jnp.zeros(p_ref.shape, BF16)
    heads = [slice(h * tq, (h + 1) * tq) for h in range(N_HEADS)]

    def tile_logits(kt):
        kk = keys_fn(kt)
        if mask_fn is None:
            return lambda rows: _dot_nt(q4_ref[rows, :], kk)
        mask = mask_fn(kt)
        return lambda rows: jnp.where(mask, _dot_nt(q4_ref[rows, :], kk), NEG)

    def step(kt, ms, last):
        vv = vals_fn(jnp.maximum(kt - 1, 0))
        next_logits = None if last else tile_logits(kt + 1)
        last_mask = last_mask_fn() if (last and last_mask_fn) else None
        new_ms = []
        for h, rows in enumerate(heads):
            ah = a_ref[rows, :]
            if last_mask is not None:
                ah = jnp.where(last_mask, ah, NEG)
            pv = _dot(p_ref[rows, :], vv)
            if not last:
                a_ref[rows, :] = next_logits(rows)
            m_new = jnp.maximum(ms[h], jnp.max(ah, axis=-1, keepdims=True))
            p_ref[rows, :] = jnp.exp2(ah - m_new).astype(BF16)
            acc_ref[rows, :] = jnp.exp2(ms[h] - m_new) * (acc_ref[rows, :] + pv)
            new_ms.append(m_new)
        return tuple(new_ms)

    first_logits = tile_logits(0)
    for rows in heads:
        a_ref[rows, :] = first_logits(rows)
    m0 = tuple(jnp.full((tq, 1), NEG, F32) for _ in range(N_HEADS))
    ms = lax.fori_loop(0, n_kt - 1, lambda kt, ms: step(kt, ms, False), m0)
    step(n_kt - 1, ms, True)
    vv = vals_fn(n_kt - 1)
    for rows in heads:
        acc_ref[rows, :] += _dot(p_ref[rows, :], vv)


def _softmax2_rows(a, mask):
    a = jnp.where(mask, a, NEG)
    m = jnp.max(a, axis=-1, keepdims=True)
    e = jnp.where(mask, jnp.exp2(a - m), 0.0)
    s = jnp.sum(e, axis=-1, keepdims=True)
    return e / jnp.maximum(s, 1e-30)


def _split3(x):
    hi = x.astype(BF16)
    r1 = x - hi.astype(F32)
    mid = r1.astype(BF16)
    lo = (r1 - mid.astype(F32)).astype(BF16)
    return hi, mid, lo


def _nsa_kernel(q_ref, sm_ref, kc_ref, vc_ref, sk_ref, sv_ref, wk_ref, wv_ref, pf_ref, qf_ref, vf_ref,
                o_ref, q4_ref, p4_ref, acc_ref, selm_ref, p_ref, a_ref, ow_ref):
    tq = q_ref.shape[1]
    s_len = sk_ref.shape[1]
    n_cmp = kc_ref.shape[1]
    n_sel = s_len // SEL_BLOCK
    assert n_sel <= LANES - BLOCK_COL
    n_top = min(SEL_TOPK, n_sel)
    t0 = pl.program_id(1) * tq
    _stack_queries(q_ref, qf_ref, q4_ref)
    t_col = t0 + lax.broadcasted_iota(I32, (tq, 1), 0)

    a_all = _dot_nt(q4_ref[:, 0:HEAD_DIM], kc_ref[0])
    c_idx = lax.broadcasted_iota(I32, (1, n_cmp), 1)
    cd = t_col - (c_idx * CMP_STRIDE + (CMP_BLOCK - 1))
    cmask = (cd >= 0) & (c_idx < n_cmp - 1)
    cdf = cd.astype(F32)
    p_sum = jnp.zeros((tq, n_cmp), F32)
    for h in range(N_HEADS):
        p = _softmax2_rows(a_all[h * tq:(h + 1) * tq] - (SLOPES_NSA[h] * LOG2E) * cdf, cmask)
        p_sum = p_sum + p
        p4_ref[h * tq:(h + 1) * tq, 0:n_cmp] = p.astype(BF16)
    o_cmp = _dot(p4_ref[:, 0:n_cmp], vc_ref[0])

    wlen = min(NSA_WINDOW + tq, s_len)
    ks = pl.multiple_of(jnp.maximum(t0 + tq - wlen, 0), LANES)
    kw = jnp.concatenate([wk_ref[0, pl.ds(ks, wlen), :], pf_ref[pl.ds(ks, wlen), :]], axis=1)
    vw = jnp.concatenate([wv_ref[0, pl.ds(ks, wlen), :], vf_ref[0:wlen, :]], axis=1)
    wd = t_col - (ks + lax.broadcasted_iota(I32, (1, wlen), 1))
    wmask = (wd >= 0) & (wd < NSA_WINDOW)
    for h in range(N_HEADS):
        rows = slice(h * tq, (h + 1) * tq)
        ah = jnp.where(wmask, _dot_nt(q4_ref[rows, :], kw), NEG)
        m = jnp.max(ah, axis=-1, keepdims=True)
        ow_ref[rows, :] = _dot(jnp.exp2(ah - m).astype(BF16), vw)

    n_selp = selm_ref.shape[0]
    jj = lax.broadcasted_iota(I32, (n_selp, n_cmp), 0)
    cc = lax.broadcasted_iota(I32, (n_selp, n_cmp), 1)
    c_start = cc * CMP_STRIDE
    overlap = ((c_start < (jj + 1) * SEL_BLOCK) & (c_start + (CMP_BLOCK - 1) >= jj * SEL_BLOCK)
               & (cc < n_cmp - 1))
    ov = jnp.where(overlap, 1.0, 0.0).astype(BF16)
    hi, mid, lo = _split3(p_sum)
    imp = _dot_nt(ov, hi) + _dot_nt(ov, mid) + _dot_nt(ov, lo)
    j_col = lax.broadcasted_iota(I32, (n_selp, 1), 0)
    t_blk = (t0 + lax.broadcasted_iota(I32, (1, tq), 1)) // SEL_BLOCK
    forced = (j_col == 0) | (j_col == t_blk) | (j_col == t_blk - 1)
    imp = jnp.where(forced, jnp.inf, imp)
    imp = jnp.where(j_col <= t_blk, imp, -jnp.inf)
    rank = jnp.zeros((n_selp, tq), F32)
    for i2 in range(n_sel):
        ci = imp[i2:i2 + 1, :]
        tie_first = jnp.where(j_col > i2, 1.0, 0.0)
        rank = rank + jnp.where(ci > imp, 1.0, jnp.where(ci == imp, tie_first, 0.0))
    selm_ref[...] = jnp.where((rank < n_top) & (j_col < n_sel), 0.0, NEG)
    sel_bias = selm_ref[...].T[:, 0:LANES - BLOCK_COL].astype(BF16)
    for h in range(N_HEADS):
        q4_ref[h * tq:(h + 1) * tq, HEAD_DIM + BLOCK_COL:2 * HEAD_DIM] = sel_bias

    tk = min(ATT_TK, s_len)
    n_kt = (t0 + tq - 1) // tk + 1

    def sel_keys(kt):
        s0 = pl.multiple_of(kt * tk, tk)
        return jnp.concatenate([sk_ref[0, pl.ds(s0, tk), :], pf_ref[pl.ds(s0, tk), :]], axis=1)

    def sel_vals(kt):
        s0 = pl.multiple_of(kt * tk, tk)
        return jnp.concatenate([sv_ref[0, pl.ds(s0, tk), :], vf_ref[0:tk, :]], axis=1)

    def causal_last():
        return (n_kt - 1) * tk + lax.broadcasted_iota(I32, (1, tk), 1) <= t_col

    _flash_attention(q4_ref, acc_ref, p_ref, a_ref, n_kt, sel_keys, sel_vals, tq, last_mask_fn=causal_last)

    gates = jax.nn.sigmoid(sm_ref[0].astype(F32))
    for h in range(N_HEADS):
        rows = slice(h * tq, (h + 1) * tq)
        o_slc = acc_ref[rows, 0:HEAD_DIM] / jnp.maximum(acc_ref[rows, HEAD_DIM:HEAD_DIM + 1], 1e-30)
        o_w = ow_ref[rows, 0:HEAD_DIM] / jnp.maximum(ow_ref[rows, HEAD_DIM:HEAD_DIM + 1], 1e-30)
        g0 = gates[:, NG_LANE + 3 * h:NG_LANE + 3 * h + 1]
        g1 = gates[:, NG_LANE + 3 * h + 1:NG_LANE + 3 * h + 2]
        g2 = gates[:, NG_LANE + 3 * h + 2:NG_LANE + 3 * h + 3]
        o = g0 * o_cmp[rows] + g1 * o_slc + g2 * o_w
        o_ref[0, :, h * HEAD_DIM:(h + 1) * HEAD_DIM] = o.astype(BF16)


def _nsa_mixer(proj, k_cmp, v_cmp):
    b, s, _ = proj.shape
    tq = min(ATT_TQ, s)
    tk = min(ATT_TK, s)
    n_cmp = k_cmp.shape[1]
    wlen = min(NSA_WINDOW + tq, s)
    col = lambda name: _PACK_OFF[name] // HEAD_DIM
    full = lambda name: pl.BlockSpec((1, s, HEAD_DIM), functools.partial(lambda c, bi, i: (bi, 0, c), col(name)))
    nv = max(tk, wlen)
    return pl.pallas_call(
        _nsa_kernel,
        out_shape=jax.ShapeDtypeStruct((b, s, W_MIX), BF16),
        grid=(b, s // tq),
        in_specs=[
            pl.BlockSpec((1, tq, W_MIX), lambda bi, i: (bi, i, _PACK_OFF['n_q'] // W_MIX)),
            pl.BlockSpec((1, tq, LANES), lambda bi, i: (bi, i, SMALL_OFF // LANES)),
            pl.BlockSpec((1, n_cmp, HEAD_DIM), lambda bi, i: (bi, 0, 0)),
            pl.BlockSpec((1, n_cmp, HEAD_DIM), lambda bi, i: (bi, 0, 0)),
            full('n_sk'), full('n_sv'), full('n_wk'), full('n_wv'),
            _resident((s, LANES)), _resident((N_HEADS * tq, LANES)), _resident((nv, LANES)),
        ],
        out_specs=pl.BlockSpec((1, tq, W_MIX), lambda bi, i: (bi, i, 0)),
        scratch_shapes=[pltpu.VMEM((N_HEADS * tq, 2 * HEAD_DIM), BF16),
                        pltpu.VMEM((N_HEADS * tq, n_cmp), BF16),
                        pltpu.VMEM((N_HEADS * tq, 2 * HEAD_DIM), F32),
                        pltpu.VMEM((((s // SEL_BLOCK + LANES - 1) // LANES) * LANES, tq), F32),
                        pltpu.VMEM((N_HEADS * tq, tk), BF16),
                        pltpu.VMEM((N_HEADS * tq, tk), F32),
                        pltpu.VMEM((N_HEADS * tq, 2 * HEAD_DIM), F32)],
        compiler_params=_cparams(("parallel", "arbitrary")),
        name="nsa_attention",
    )(proj, proj, k_cmp, v_cmp, proj, proj, proj, proj,
      _pos_features(s, block_onehot=True), _slope_features(SLOPES_NSA, tq), _ones_feature(nv))


I16 = jnp.int16
I16_MIN = -2 ** 15
SEL_ROWS = 64
SCORE_CHAINS = 4


def _dsa_kernel(q_ref, iq_ref, sm_ref, smf_ref, k_ref, v_ref, pf_ref, qf_ref, vf_ref, o_ref,
                q4_ref, acc_ref, hi_ref, lo_ref, lq_ref, selb_ref, p_ref, a_ref):
    tq = q_ref.shape[1]
    s_len = k_ref.shape[1]
    topk = min(DSA_TOPK_MAX, s_len // 4)
    tk = min(ATT_TK, s_len)
    n_ch = tk // SEL_ROWS
    t0 = pl.program_id(1) * tq
    n_kt = (t0 + tq - 1) // tk + 1
    _stack_queries(q_ref, qf_ref, q4_ref)
    t_row = t0 + lax.broadcasted_iota(I32, (1, tq), 1)
    iq = iq_ref[0]
    iw_t = sm_ref[0].astype(F32).T
    one = jnp.ones((), BF16)
    zero = jnp.zeros((), BF16)

    def score_body(kt, _):
        tc = tk // SCORE_CHAINS
        for c in range(SCORE_CHAINS):
            s0 = pl.multiple_of(kt * tk + c * tc, tc)
            ik = smf_ref[0, pl.ds(s0, tc), IK_LANE:IK_LANE + IDX_DIM]
            sc = jnp.zeros((tc, tq), F32)
            for h in range(IDX_HEADS):
                lg = _dot_nt(ik, iq[:, h * IDX_DIM:(h + 1) * IDX_DIM])
                sc = sc + jnp.maximum(lg, 0.0) * iw_t[IW_LANE + h:IW_LANE + h + 1, :]
            sc = jnp.where(sc == 0.0, 0.0, sc)
            bits = lax.bitcast_convert_type(sc, I32)
            key = bits ^ ((bits >> 31) & 0x7FFFFFFF)
            s_pos = s0 + lax.broadcasted_iota(I32, (tc, 1), 0)
            key = jnp.where(s_pos <= t_row, key, INT_MIN)
            hi_ref[pl.ds(s0, tc), :] = (key >> 16).astype(I16)
            lo_ref[pl.ds(s0, tc), :] = ((key & 0xFFFF) + I16_MIN).astype(I16)
        return 0

    lax.fori_loop(0, n_kt, score_body, 0)

    def count_ge(ref, thr_row):
        thr = jnp.broadcast_to(thr_row, (SEL_ROWS, tq))

        def body(kt, cnt):
            s0 = pl.multiple_of(kt * tk, tk)
            for c in range(n_ch):
                cnt = cnt + jnp.where(ref[pl.ds(s0 + c * SEL_ROWS, SEL_ROWS), :] >= thr, one, zero)
            return cnt
        cnt = lax.fori_loop(0, n_kt, body, jnp.zeros((SEL_ROWS, tq), BF16))
        return jnp.sum(cnt.astype(F32), axis=0, keepdims=True)

    def kth_largest(ref, k):
        def bit_body(it, carry):
            cand, cnt = carry
            trial = cand | jnp.left_shift(jnp.int32(1), 15 - it)
            total = count_ge(ref, (trial + I16_MIN).astype(I16))
            ok = total >= k
            return jnp.where(ok, trial, cand), jnp.where(ok, total, cnt)
        return lax.fori_loop(0, 16, bit_body, (jnp.zeros((1, tq), I32), jnp.zeros((1, tq), F32)))

    raw_hi, n_ge_hi = kth_largest(hi_ref, float(topk))
    cand_hi = jnp.maximum(raw_hi, 1)
    p16 = jnp.broadcast_to((cand_hi + I16_MIN).astype(I16), (SEL_ROWS, tq))
    n_above = count_ge(hi_ref, (jnp.minimum(cand_hi + 1, 2 ** 16 - 1) + I16_MIN).astype(I16))
    need = float(topk) - n_above

    def tie_body(kt, _):
        s0 = pl.multiple_of(kt * tk, tk)
        for c in range(n_ch):
            ds = pl.ds(s0 + c * SEL_ROWS, SEL_ROWS)
            lq_ref[ds, :] = jnp.where(hi_ref[ds, :] == p16, lo_ref[ds, :], jnp.full((), I16_MIN, I16))
        return 0

    lax.fori_loop(0, n_kt, tie_body, 0)
    cand_lo, n_ge_lo = kth_largest(lq_ref, need)
    q16_row = (cand_lo + I16_MIN).astype(I16)
    q16 = jnp.broadcast_to(q16_row, (SEL_ROWS, tq))

    def sel_body(kt, _):
        s0 = pl.multiple_of(kt * tk, tk)
        for c in range(n_ch):
            ds = pl.ds(s0 + c * SEL_ROWS, SEL_ROWS)
            hi = hi_ref[ds, :]
            tie = jnp.where(hi == p16, jnp.where(lo_ref[ds, :] >= q16, one, zero), zero)
            selb_ref[ds, :] = jnp.where(hi > p16, one, tie)
        return 0

    lax.fori_loop(0, n_kt, sel_body, 0)

    n_gt = jnp.where(cand_lo >= 2 ** 16 - 1, 0.0,
                     count_ge(lq_ref, (jnp.minimum(cand_lo + 1, 2 ** 16 - 1) + I16_MIN).astype(I16)))
    n_with_hi = jnp.where(raw_hi > 0, n_ge_hi - n_above, 0.0)
    n_ge = jnp.where(cand_lo > 0, n_ge_lo, n_with_hi)
    quota = need - n_gt
    any_tied = jnp.max(jnp.where(n_ge - n_gt > quota, 1.0, 0.0)) > 0.0

    @pl.when(any_tied)
    def _():
        p_t = jnp.broadcast_to(p16[0:1, :], (tk, tq))
        q_t = jnp.broadcast_to(q16_row, (tk, tq))
        lower = jnp.where(lax.broadcasted_iota(I32, (tk, tk), 0) >= lax.broadcasted_iota(I32, (tk, tk), 1),
                          1.0, 0.0).astype(BF16)

        def exact_ties(kt):
            ds = pl.ds(pl.multiple_of(kt * tk, tk), tk)
            exact = jnp.where(hi_ref[ds, :] == p_t, jnp.where(lo_ref[ds, :] == q_t, one, zero), zero)
            return ds, exact, _dot(lower, exact)

        def drop(ds, exact, rank):
            selb_ref[ds, :] = selb_ref[ds, :] - jnp.where(rank > quota, 1.0, 0.0).astype(BF16) * exact

        def drop_pair(j, seen):
            ds_a, exact_a, in_a = exact_ties(2 * j)
            ds_b, exact_b, in_b = exact_ties(2 * j + 1)
            rank_a = in_a + seen
            rank_b = in_b + rank_a[tk - 1:tk, :]
            drop(ds_a, exact_a, rank_a)
            drop(ds_b, exact_b, rank_b)
            return rank_b[tk - 1:tk, :]

        seen = lax.fori_loop(0, n_kt // 2, drop_pair, jnp.zeros((1, tq), F32))

        @pl.when(n_kt % 2 == 1)
        def _():
            ds, exact, in_tile = exact_ties(n_kt - 1)
            drop(ds, exact, in_tile + seen)

    def att_keys(kt):
        s0 = pl.multiple_of(kt * tk, tk)
        return jnp.concatenate([k_ref[0, pl.ds(s0, tk), :], pf_ref[pl.ds(s0, tk), :]], axis=1)

    def att_vals(kt):
        s0 = pl.multiple_of(kt * tk, tk)
        return jnp.concatenate([v_ref[0, pl.ds(s0, tk), :], vf_ref[...]], axis=1)

    def att_mask(kt):
        s0 = pl.multiple_of(kt * tk, tk)
        return selb_ref[pl.ds(s0, tk), :].astype(F32).T > 0.5

    _flash_attention(q4_ref, acc_ref, p_ref, a_ref, n_kt, att_keys, att_vals, tq, mask_fn=att_mask)
    for h in range(N_HEADS):
        rows = slice(h * tq, (h + 1) * tq)
        o = acc_ref[rows, 0:HEAD_DIM] / jnp.maximum(acc_ref[rows, HEAD_DIM:HEAD_DIM + 1], 1e-30)
        o_ref[0, :, h * HEAD_DIM:(h + 1) * HEAD_DIM] = o.astype(BF16)


def _dsa_mixer(proj):
    b, s, _ = proj.shape
    tq = min(ATT_TQ, s)
    tk = min(ATT_TK, s)
    full = lambda off: pl.BlockSpec((1, s, LANES), functools.partial(lambda c, bi, i: (bi, 0, c), off // LANES))
    return pl.pallas_call(
        _dsa_kernel,
        out_shape=jax.ShapeDtypeStruct((b, s, W_MIX), BF16),
        grid=(b, s // tq),
        in_specs=[
            pl.BlockSpec((1, tq, W_MIX), lambda bi, i: (bi, i, _PACK_OFF['d_q'] // W_MIX)),
            pl.BlockSpec((1, tq, W_MIX), lambda bi, i: (bi, i, _PACK_OFF['i_q'] // W_MIX)),
            pl.BlockSpec((1, tq, LANES), lambda bi, i: (bi, i, SMALL_OFF // LANES)),
            full(SMALL_OFF), full(_PACK_OFF['d_k']), full(_PACK_OFF['d_v']),
            _resident((s, LANES)), _resident((N_HEADS * tq, LANES)), _resident((tk, LANES)),
        ],
        out_specs=pl.BlockSpec((1, tq, W_MIX), lambda bi, i: (bi, i, 0)),
        scratch_shapes=[pltpu.VMEM((N_HEADS * tq, 2 * HEAD_DIM), BF16),
                        pltpu.VMEM((N_HEADS * tq, 2 * HEAD_DIM), F32),
                        pltpu.VMEM((s, tq), I16), pltpu.VMEM((s, tq), I16), pltpu.VMEM((s, tq), I16),
                        pltpu.VMEM((s, tq), BF16),
                        pltpu.VMEM((N_HEADS * tq, tk), BF16),
                        pltpu.VMEM((N_HEADS * tq, tk), F32)],
        compiler_params=_cparams(("parallel", "arbitrary")),
        name="dsa_attention",
    )(proj, proj, proj, proj, proj, proj,
      _pos_features(s), _slope_features(SLOPES_DSA, tq), _ones_feature(tk))


OUTPROJ_TM = 512
LN_ROWS = 256


def _outproj_kernel(y0_ref, y1_ref, y2_ref, y3_ref, w_ref, x_ref, g_ref, b_ref, o_ref, ob_ref, acc_a, acc_b):
    i = pl.program_id(0)

    @pl.when(i == 0)
    def _():
        acc_b[...] = jnp.zeros(acc_b.shape, F32)

    def step(acc_prev, acc_next):
        for r0 in range(0, x_ref.shape[0], LN_ROWS):
            rows = slice(r0, r0 + LN_ROWS)
            y = _layernorm(acc_prev[rows, :], g_ref[...], b_ref[...])
            o_ref[rows, :] = y
            ob_ref[rows, :] = y.astype(BF16)
        for r0 in range(0, x_ref.shape[0], LN_ROWS):
            rows = slice(r0, r0 + LN_ROWS)
            acc = DN_ALPHA * x_ref[rows, :]
            for gi, y_ref in enumerate((y0_ref, y1_ref, y2_ref, y3_ref)):
                acc = acc + _dot(y_ref[rows, :], w_ref[gi * W_MIX:(gi + 1) * W_MIX, :])
            acc_next[rows, :] = acc

    @pl.when(i % 2 == 0)
    def _():
        step(acc_b, acc_a)

    @pl.when(i % 2 == 1)
    def _():
        step(acc_a, acc_b)


def _outproj_ln(ys, w_out, x, g, b, l):
    n_tok, d = x.shape
    tm = OUTPROJ_TM
    n_tiles = n_tok // tm
    in_row = lambda i: (jnp.minimum(i, n_tiles - 1), 0)
    out_row = lambda i: (jnp.maximum(i - 1, 0), 0)
    yspec = pl.BlockSpec((tm, W_MIX), in_row)
    ospec = pl.BlockSpec((tm, d), out_row)
    return pl.pallas_call(
        _outproj_kernel,
        out_shape=(jax.ShapeDtypeStruct((n_tok, d), F32), jax.ShapeDtypeStruct((n_tok, d), BF16)),
        grid=(n_tiles + 1,),
        in_specs=[yspec, yspec, yspec, yspec, _of_layer(w_out, l), pl.BlockSpec((tm, d), in_row),
                  _of_layer(g, l), _of_layer(b, l)],
        out_specs=(ospec, ospec),
        scratch_shapes=[pltpu.VMEM((tm, d), F32), pltpu.VMEM((tm, d), F32)],
        compiler_params=_cparams(("arbitrary",)),
        name="outproj_ln",
    )(*ys, w_out, x, g, b)


FFN_UP_TM = 1024
FFN_TF = 512
FFN_DOWN_TM = 256


def _ffn_up_kernel(xb_ref, wg_ref, wu_ref, h_ref, wgb_ref, wub_ref):
    @pl.when(pl.program_id(1) == 0)
    def _():
        wgb_ref[...] = wg_ref[...].astype(BF16)
        wub_ref[...] = wu_ref[...].astype(BF16)

    xb = xb_ref[...]
    gate = _dot(xb, wgb_ref[...])
    up = _dot(xb, wub_ref[...])
    h_ref[...] = (gate * jax.nn.sigmoid(gate) * up).astype(BF16)


def _ffn_down_kernel(h_ref, wd_ref, x_ref, g_ref, b_ref, *rest):
    *out_refs, acc_a, acc_b = rest
    i = pl.program_id(0)

    @pl.when(i == 0)
    def _():
        acc_b[...] = jnp.zeros(acc_b.shape, F32)

    def step(acc_prev, acc_next):
        y = _layernorm(acc_prev[...], g_ref[...], b_ref[...])
        out_refs[0][...] = y
        if len(out_refs) > 1:
            out_refs[1][...] = y.astype(BF16)
        acc_next[...] = DN_ALPHA * x_ref[...] + _dot(h_ref[...], wd_ref[...])

    @pl.when(i % 2 == 0)
    def _():
        step(acc_b, acc_a)

    @pl.when(i % 2 == 1)
    def _():
        step(acc_a, acc_b)


def _ffn_ln(xb, w_gate_up, w_down, x, g, b, l, want_bf16):
    n_tok, d = x.shape
    d_ff = w_down.shape[1]
    tm, tf = FFN_UP_TM, FFN_TF
    nf = d_ff // tf
    h = pl.pallas_call(
        _ffn_up_kernel,
        out_shape=jax.ShapeDtypeStruct((n_tok, d_ff), BF16),
        grid=(nf, n_tok // tm),
        in_specs=[
            pl.BlockSpec((tm, d), lambda f, i: (i, 0)),
            pl.BlockSpec((None, d, tf), lambda f, i: (l, 0, f)),
            pl.BlockSpec((None, d, tf), lambda f, i: (l, 0, f + nf)),
        ],
        out_specs=pl.BlockSpec((tm, tf), lambda f, i: (i, f)),
        scratch_shapes=[pltpu.VMEM((d, tf), BF16), pltpu.VMEM((d, tf), BF16)],
        compiler_params=_cparams(("parallel", "arbitrary")),
        name="ffn_up",
    )(xb, w_gate_up, w_gate_up)
    tm = FFN_DOWN_TM
    n_tiles = n_tok // tm
    in_row = lambda i: (jnp.minimum(i, n_tiles - 1), 0)
    out_row = lambda i: (jnp.maximum(i - 1, 0), 0)
    out_dtypes = (F32, BF16) if want_bf16 else (F32,)
    outs = pl.pallas_call(
        _ffn_down_kernel,
        out_shape=tuple(jax.ShapeDtypeStruct((n_tok, d), t) for t in out_dtypes),
        grid=(n_tiles + 1,),
        in_specs=[pl.BlockSpec((tm, d_ff), in_row), _of_layer(w_down, l),
                  pl.BlockSpec((tm, d), in_row), _of_layer(g, l), _of_layer(b, l)],
        out_specs=tuple(pl.BlockSpec((tm, d), out_row) for _ in out_dtypes),
        scratch_shapes=[pltpu.VMEM((tm, d), F32), pltpu.VMEM((tm, d), F32)],
        compiler_params=_cparams(("arbitrary",)),
        name="ffn_down_ln",
    )(h, w_down, x, g, b)
    return outs if want_bf16 else (outs[0], None)


PACK_ROWS = 256
_FOLD = {'d_q': ATTN_SCALE * LOG2E, 'n_q': ATTN_SCALE * LOG2E, 'i_w': IDX_SCALE}


def _pack_kernel(w_ref, o_ref):
    for n in _PACK_ORDER:
        src, dst, size = _SRC_OFF[n], _PACK_OFF[n], _SRC_SIZE[n]
        o_ref[:, dst:dst + size] = (w_ref[:, src:src + size] * _FOLD.get(n, 1.0)).astype(BF16)
    used = sum(_SRC_SIZES)
    o_ref[:, used:D_PACK] = jnp.zeros((o_ref.shape[0], D_PACK - used), BF16)


def _pack_w_in(w):
    n_layers, d, d_in = w.shape
    return pl.pallas_call(
        _pack_kernel,
        out_shape=jax.ShapeDtypeStruct((n_layers, d, D_PACK), BF16),
        grid=(n_layers, d // PACK_ROWS),
        in_specs=[pl.BlockSpec((None, PACK_ROWS, d_in), lambda l, i: (l, i, 0))],
        out_specs=pl.BlockSpec((None, PACK_ROWS, D_PACK), lambda l, i: (l, i, 0)),
        compiler_params=_cparams(("parallel", "parallel")),
        name="pack_w_in",
    )(w)


def _prepare(p):
    row = lambda v: v[:, None, :].astype(F32)
    n_layers = p['w_in'].shape[0]
    flat = lambda v: v.reshape(n_layers, 1, -1).astype(BF16)
    return dict(
        w_pack=_pack_w_in(p['w_in']), w_out=p['w_out'].astype(BF16),
        pool_w=p['pool_w'].astype(BF16), pool_scale=row(p['pool_scale']),
        conv_w=jnp.concatenate([p['conv_w'].astype(F32), jnp.zeros((n_layers, 1, W_MIX), F32)], axis=1),
        conv_b=row(p['conv_b']), conv_ln_g=row(p['conv_ln_g']), conv_ln_b=row(p['conv_ln_b']),
        conv_pw_w=p['conv_pw_w'].astype(BF16), conv_pw_b=row(p['conv_pw_b']),
        cmp_pos_k=flat(p['cmp_pos_k']), cmp_pos_v=flat(p['cmp_pos_v']),
        cmp_k_w1=p['cmp_k_w1'].astype(BF16), cmp_k_w2=p['cmp_k_w2'].astype(BF16),
        cmp_v_w1=p['cmp_v_w1'].astype(BF16), cmp_v_w2=p['cmp_v_w2'].astype(BF16),
        ln1_g=row(p['ln1_g']), ln1_b=row(p['ln1_b']), ln2_g=row(p['ln2_g']), ln2_b=row(p['ln2_b']),
        w_gate_up=p['w_gate_up'], w_down=p['w_down'].astype(BF16))


def _layer(x, xb, p, l, last):
    b, s, d = x.shape
    n_tok = b * s
    x_in = x if xb is None else xb
    proj, ck, cv = _inproj(x_in.reshape(n_tok, d), p['w_pack'], l)
    proj = proj.reshape(b, s, D_PACK)
    y_pool = _pool_mixer(proj, p['pool_w'], p['pool_scale'], l)
    y_conv = _conv_mixer(proj, p['conv_w'], p['conv_b'], p['conv_ln_g'], p['conv_ln_b'],
                         p['conv_pw_w'], p['conv_pw_b'], l)
    y_dsa = _dsa_mixer(proj)
    k_cmp = _compress(ck.reshape(b, s, HEAD_DIM), p['cmp_pos_k'], p['cmp_k_w1'], p['cmp_k_w2'], l)
    v_cmp = _compress(cv.reshape(b, s, HEAD_DIM), p['cmp_pos_v'], p['cmp_v_w1'], p['cmp_v_w2'], l)
    y_nsa = _nsa_mixer(proj, k_cmp, v_cmp)
    ys = [y.reshape(n_tok, W_MIX) for y in (y_pool, y_conv, y_dsa, y_nsa)]
    x1, x1b = _outproj_ln(ys, p['w_out'], x.reshape(n_tok, d), p['ln1_g'], p['ln1_b'], l)
    x2, x2b = _ffn_ln(x1b, p['w_gate_up'], p['w_down'], x1, p['ln2_g'], p['ln2_b'], l,
                      want_bf16=not last)
    return x2.reshape(b, s, d), None if last else x2b.reshape(b, s, d)


def kernel(x, w_in, w_out, pool_w, pool_scale, conv_w, conv_b, conv_ln_g, conv_ln_b, conv_pw_w, conv_pw_b,
           cmp_pos_k, cmp_pos_v, cmp_k_w1, cmp_k_w2, cmp_v_w1, cmp_v_w2, ln1_g, ln1_b, ln2_g, ln2_b,
           w_gate_up, w_down):
    params = dict(w_in=w_in, w_out=w_out, pool_w=pool_w, pool_scale=pool_scale, conv_w=conv_w, conv_b=conv_b,
                  conv_ln_g=conv_ln_g, conv_ln_b=conv_ln_b, conv_pw_w=conv_pw_w, conv_pw_b=conv_pw_b,
                  cmp_pos_k=cmp_pos_k, cmp_pos_v=cmp_pos_v, cmp_k_w1=cmp_k_w1, cmp_k_w2=cmp_k_w2,
                  cmp_v_w1=cmp_v_w1, cmp_v_w2=cmp_v_w2, ln1_g=ln1_g, ln1_b=ln1_b, ln2_g=ln2_g, ln2_b=ln2_b,
                  w_gate_up=w_gate_up, w_down=w_down)
    prepared = _prepare(params)
    xb = None
    for l in range(w_in.shape[0]):
        x, xb = _layer(x, xb, prepared, l, last=l == w_in.shape[0] - 1)
    return x
```

```python
import functools

import numpy as np
import jax
import jax.numpy as jnp
from jax import lax
from jax.experimental import pallas as pl
from jax.experimental.pallas import tpu as pltpu

F32 = jnp.float32
BF16 = jnp.bfloat16
I32 = jnp.int32

D_MODEL = 2048
DEPTH = 2
W_MIX = D_MODEL // 4
HEAD_DIM = 128
N_HEADS = W_MIX // HEAD_DIM
POOL_WINDOWS = (2, 4, 8, 16)
POOL_GROUP = W_MIX // len(POOL_WINDOWS)
CONV_WIDTH = 31
IDX_HEADS = 8
IDX_DIM = 64
DSA_TOPK_MAX = 256
CMP_BLOCK = 32
CMP_STRIDE = 16
SEL_BLOCK = 64
SEL_TOPK = 16
NSA_WINDOW = 512
DN_ALPHA = (2 * DEPTH) ** 0.25
ATTN_SCALE = HEAD_DIM ** -0.5
IDX_SCALE = (IDX_HEADS * IDX_DIM) ** -0.5
LN_EPS = 1e-5
LOG2E = 1.4426950408889634

LANES = 128
SUBLANES = 8
VMEM_LIMIT = 48 * 1024 * 1024

NEG = -1e30
INT_MIN = -2 ** 31

_SRC_SIZES = (W_MIX, W_MIX, W_MIX, W_MIX, HEAD_DIM, HEAD_DIM, IDX_HEADS * IDX_DIM, IDX_DIM, IDX_HEADS,
              W_MIX, HEAD_DIM, HEAD_DIM, HEAD_DIM, HEAD_DIM, HEAD_DIM, HEAD_DIM, N_HEADS * 3)
_SRC_NAMES = ('pool', 'c_a', 'c_g', 'd_q', 'd_k', 'd_v', 'i_q', 'i_k', 'i_w',
              'n_q', 'n_ck', 'n_cv', 'n_sk', 'n_sv', 'n_wk', 'n_wv', 'n_g')
_SRC_OFF = dict(zip(_SRC_NAMES, np.concatenate([[0], np.cumsum(_SRC_SIZES)[:-1]]).tolist()))
_SRC_SIZE = dict(zip(_SRC_NAMES, _SRC_SIZES))
_PACK_ORDER = ('pool', 'c_a', 'c_g', 'd_q', 'i_q', 'n_q', 'd_k', 'd_v',
               'n_ck', 'n_cv', 'n_sk', 'n_sv', 'n_wk', 'n_wv', 'i_k', 'i_w', 'n_g')
_PACK_OFF = {}
_o = 0
for _n in _PACK_ORDER:
    _PACK_OFF[_n] = _o
    _o += _SRC_SIZE[_n]
D_PACK = ((_o + LANES - 1) // LANES) * LANES
SMALL_OFF = _PACK_OFF['i_k']
IK_LANE = 0
IW_LANE = _PACK_OFF['i_w'] - SMALL_OFF
NG_LANE = _PACK_OFF['n_g'] - SMALL_OFF


def _alibi_slopes():
    n = 2 * N_HEADS
    s = np.power(2.0, -8.0 * np.arange(1, n + 1) / n).astype(np.float32)
    return [float(v) for v in s[0::2]], [float(v) for v in s[1::2]]


SLOPES_DSA, SLOPES_NSA = _alibi_slopes()


def _cparams(sem):
    return pltpu.CompilerParams(dimension_semantics=sem, vmem_limit_bytes=VMEM_LIMIT)


def _resident(shape):
    nd = len(shape)
    return pl.BlockSpec(shape, lambda *_: (0,) * nd, pipeline_mode=pl.Buffered(1))


def _of_layer(arr, l):
    nd = arr.ndim
    return pl.BlockSpec((None,) + arr.shape[1:], lambda *_: (l,) + (0,) * (nd - 1),
                        pipeline_mode=pl.Buffered(1))


def _dot(a, b):
    return jnp.dot(a, b, preferred_element_type=F32)


def _dot_nt(a, b):
    return lax.dot_general(a, b, (((1,), (1,)), ((), ())), preferred_element_type=F32)


def _layernorm(x, g, b):
    mu = jnp.mean(x, axis=-1, keepdims=True)
    xc = x - mu
    var = jnp.mean(xc * xc, axis=-1, keepdims=True)
    return xc * lax.rsqrt(var + LN_EPS) * g + b


INPROJ_TM = 512
INPROJ_CHUNK = 512


def _inproj_kernel(x_ref, w_ref, o_ref, ck_ref, cv_ref):
    x = x_ref[...].astype(BF16)
    n = o_ref.shape[1]
    for c0 in range(0, n, INPROJ_CHUNK):
        c1 = min(c0 + INPROJ_CHUNK, n)
        o_ref[:, c0:c1] = _dot(x, w_ref[:, c0:c1]).astype(BF16)
    ck_ref[...] = o_ref[:, _PACK_OFF['n_ck']:_PACK_OFF['n_ck'] + HEAD_DIM]
    cv_ref[...] = o_ref[:, _PACK_OFF['n_cv']:_PACK_OFF['n_cv'] + HEAD_DIM]


def _inproj(x, w_pack, l):
    n_tok, d = x.shape
    tm = INPROJ_TM
    col = pl.BlockSpec((tm, HEAD_DIM), lambda i: (i, 0))
    return pl.pallas_call(
        _inproj_kernel,
        out_shape=(jax.ShapeDtypeStruct((n_tok, D_PACK), BF16),
                   jax.ShapeDtypeStruct((n_tok, HEAD_DIM), BF16), jax.ShapeDtypeStruct((n_tok, HEAD_DIM), BF16)),
        grid=(n_tok // tm,),
        in_specs=[pl.BlockSpec((tm, d), lambda i: (i, 0)), _of_layer(w_pack, l)],
        out_specs=(pl.BlockSpec((tm, D_PACK), lambda i: (i, 0)), col, col),
        compiler_params=_cparams(("parallel",)),
        name="inproj",
    )(x, w_pack)


POOL_TS = 512
POOL_HALO = 16


def _pool_kernel(u_ref, halo_ref, w_ref, sc_ref, o_ref, xs_ref):
    i = pl.program_id(1)
    ts = u_ref.shape[1]
    xs_ref[POOL_HALO:POOL_HALO + ts, :] = u_ref[0].astype(F32)
    xs_ref[0:POOL_HALO, :] = jnp.where(i > 0, halo_ref[0].astype(F32), 0.0)
    pos = i * ts + lax.broadcasted_iota(I32, (ts, 1), 0)
    for g, win in enumerate(POOL_WINDOWS):
        c = slice(g * POOL_GROUP, (g + 1) * POOL_GROUP)
        x = xs_ref[POOL_HALO:POOL_HALO + ts, c]
        acc = x
        for k in range(1, win):
            acc = acc + xs_ref[POOL_HALO - k:POOL_HALO - k + ts, c]
        cnt = jnp.minimum(pos + 1, win).astype(F32)
        d = acc / cnt - x
        y = _dot(d.astype(BF16), w_ref[g])
        o_ref[0, :, c] = (y * sc_ref[:, c]).astype(BF16)


def _pool_mixer(proj, pool_w, pool_scale, l):
    b, s, _ = proj.shape
    ts = min(POOL_TS, s)
    hb = ts // POOL_HALO
    blk = _PACK_OFF['pool'] // W_MIX
    return pl.pallas_call(
        _pool_kernel,
        out_shape=jax.ShapeDtypeStruct((b, s, W_MIX), BF16),
        grid=(b, s // ts),
        in_specs=[
            pl.BlockSpec((1, ts, W_MIX), lambda bi, i: (bi, i, blk)),
            pl.BlockSpec((1, POOL_HALO, W_MIX), lambda bi, i: (bi, jnp.maximum(i * hb - 1, 0), blk)),
            _of_layer(pool_w, l),
            _of_layer(pool_scale, l),
        ],
        out_specs=pl.BlockSpec((1, ts, W_MIX), lambda bi, i: (bi, i, 0)),
        scratch_shapes=[pltpu.VMEM((ts + POOL_HALO, W_MIX), F32)],
        compiler_params=_cparams(("parallel", "parallel")),
        name="pool_mixer",
    )(proj, proj, pool_w, pool_scale)


CONV_TS = 512
CONV_HALO = 32
CONV_ROWS = 32


def _conv_kernel(a_ref, g_ref, ha_ref, hg_ref, cw_ref, cb_ref, lg_ref, lb_ref, pw_ref, pb_ref,
                 o_ref, hs_ref, sh_ref, y_ref):
    i = pl.program_id(1)
    ts = a_ref.shape[1]
    hs_ref[CONV_HALO:CONV_HALO + ts, :] = a_ref[0].astype(F32) * jax.nn.sigmoid(g_ref[0].astype(F32))
    halo = ha_ref[0].astype(F32) * jax.nn.sigmoid(hg_ref[0].astype(F32))
    hs_ref[0:CONV_HALO, :] = jnp.where(i > 0, halo, 0.0)
    n_sh = ts + CONV_HALO - SUBLANES
    for b in range(1, SUBLANES):
        sh_ref[b - 1, 0:n_sh, :] = hs_ref[b:b + n_sh, :]
    base = CONV_HALO - (CONV_WIDTH - 1)
    for r0 in range(0, ts, CONV_ROWS):
        acc = jnp.broadcast_to(cb_ref[...], (CONV_ROWS, W_MIX))
        for j in range(CONV_WIDTH):
            a8, b = divmod(base + j, SUBLANES)
            r = r0 + a8 * SUBLANES
            src = hs_ref[r:r + CONV_ROWS, :] if b == 0 else sh_ref[b - 1, r:r + CONV_ROWS, :]
            acc = acc + src * cw_ref[j:j + 1, :]
        y = _layernorm(acc, lg_ref[...], lb_ref[...])
        y_ref[r0:r0 + CONV_ROWS, :] = (y * jax.nn.sigmoid(y)).astype(BF16)
    o_ref[0] = (_dot(y_ref[...], pw_ref[...]) + pb_ref[...]).astype(BF16)


def _conv_mixer(proj, cw, conv_b, ln_g, ln_b, pw_w, pw_b, l):
    b, s, _ = proj.shape
    ts = min(CONV_TS, s)
    hb = ts // CONV_HALO
    ba = _PACK_OFF['c_a'] // W_MIX
    bg = _PACK_OFF['c_g'] // W_MIX
    halo_map = lambda blk: (lambda bi, i: (bi, jnp.maximum(i * hb - 1, 0), blk))
    return pl.pallas_call(
        _conv_kernel,
        out_shape=jax.ShapeDtypeStruct((b, s, W_MIX), BF16),
        grid=(b, s // ts),
        in_specs=[
            pl.BlockSpec((1, ts, W_MIX), lambda bi, i: (bi, i, ba)),
            pl.BlockSpec((1, ts, W_MIX), lambda bi, i: (bi, i, bg)),
            pl.BlockSpec((1, CONV_HALO, W_MIX), halo_map(ba)),
            pl.BlockSpec((1, CONV_HALO, W_MIX), halo_map(bg)),
            _of_layer(cw, l), _of_layer(conv_b, l), _of_layer(ln_g, l), _of_layer(ln_b, l),
            _of_layer(pw_w, l), _of_layer(pw_b, l),
        ],
        out_specs=pl.BlockSpec((1, ts, W_MIX), lambda bi, i: (bi, i, 0)),
        scratch_shapes=[pltpu.VMEM((ts + CONV_HALO, W_MIX), F32),
                        pltpu.VMEM((SUBLANES - 1, ts + CONV_HALO - SUBLANES, W_MIX), F32),
                        pltpu.VMEM((ts, W_MIX), BF16)],
        compiler_params=_cparams(("parallel", "parallel")),
        name="conv_mixer",
    )(proj, proj, proj, proj, cw, conv_b, ln_g, ln_b, pw_w, pw_b)


def _compress_kernel(r_ref, pos_ref, w1_ref, w2_ref, o_ref):
    r = r_ref[0]
    half = r.shape[1]
    n = r.shape[0]
    top = _dot(r, w1_ref[0:half, :])
    bot = _dot(r, w1_ref[half:2 * half, :])
    bot_next = pltpu.roll(bot, n - 1, 0)
    posb = _dot(jnp.broadcast_to(pos_ref[...], (8, 2 * half)), w1_ref[...])[0:1, :]
    h = jax.nn.gelu(top + bot_next + posb)
    out = _dot(h.astype(BF16), w2_ref[...])
    row = lax.broadcasted_iota(I32, out.shape, 0)
    o_ref[0] = jnp.where(row < n - 1, out, 0.0).astype(BF16)


def _compress(raw, pos, w1, w2, l):
    b, s, d = raw.shape
    n = s // CMP_STRIDE
    r = raw.reshape(b, n, CMP_STRIDE * d)
    return pl.pallas_call(
        _compress_kernel,
        out_shape=jax.ShapeDtypeStruct((b, n, d), BF16),
        grid=(b,),
        in_specs=[pl.BlockSpec((1, n, CMP_STRIDE * d), lambda bi: (bi, 0, 0)),
                  _of_layer(pos, l), _of_layer(w1, l), _of_layer(w2, l)],
        out_specs=pl.BlockSpec((1, n, d), lambda bi: (bi, 0, 0)),
        compiler_params=_cparams(("parallel",)),
        name="nsa_compress",
    )(r, pos, w1, w2)


ATT_TQ = 256
ATT_TK = 512
POS_RADIX = 64
BLOCK_COL = 64


def _pos_features(s_len, block_onehot=False):
    assert s_len <= POS_RADIX * 256
    s = np.arange(s_len)
    f = np.zeros((s_len, LANES), np.float32)
    f[:, 0] = f[:, 1] = s // POS_RADIX
    f[:, 2] = f[:, 3] = s % POS_RADIX
    if block_onehot:
        assert s_len // SEL_BLOCK <= LANES - BLOCK_COL
        f[s, BLOCK_COL + s // SEL_BLOCK] = 1.0
    return jnp.asarray(f, BF16)


def _slope_features(slopes, tq):
    f = np.zeros((len(slopes) * tq, LANES), np.float32)
    for h, sl in enumerate(slopes):
        c = np.float32(sl * LOG2E)
        ca = np.float32(np.asarray(c, dtype=BF16))
        cb = np.float32(np.asarray(c - ca, dtype=BF16))
        f[h * tq:(h + 1) * tq, 0:4] = [POS_RADIX * ca, POS_RADIX * cb, ca, cb]
    return jnp.asarray(f, BF16)


def _ones_feature(n):
    f = np.zeros((n, LANES), np.float32)
    f[:, 0] = 1.0
    return jnp.asarray(f, BF16)


def _stack_queries(q_ref, qf_ref, q4_ref):
    tq = q_ref.shape[1]
    for h in range(N_HEADS):
        q4_ref[h * tq:(h + 1) * tq, 0:HEAD_DIM] = q_ref[0, :, h * HEAD_DIM:(h + 1) * HEAD_DIM]
    q4_ref[:, HEAD_DIM:2 * HEAD_DIM] = qf_ref[...]


def _flash_attention(q4_ref, acc_ref, p_ref, a_ref, n_kt, keys_fn, vals_fn, tq, mask_fn=None,
                     last_mask_fn=None):
    acc_ref[...] = jnp.zeros(acc_ref.shape, F32)
    p_ref[...] = jnp.zeros(p_ref.shape, BF16)
    heads = [slice(h * tq, (h + 1) * tq) for h in range(N_HEADS)]

    def tile_logits(kt):
        kk = keys_fn(kt)
        if mask_fn is None:
            return lambda rows: _dot_nt(q4_ref[rows, :], kk)
        mask = mask_fn(kt)
        return lambda rows: jnp.where(mask, _dot_nt(q4_ref[rows, :], kk), NEG)

    def step(kt, ms, last):
        vv = vals_fn(jnp.maximum(kt - 1, 0))
        next_logits = None if last else tile_logits(kt + 1)
        last_mask = last_mask_fn() if (last and last_mask_fn) else None
        new_ms = []
        for h, rows in enumerate(heads):
            ah = a_ref[rows, :]
            if last_mask is not None:
                ah = jnp.where(last_mask, ah, NEG)
            pv = _dot(p_ref[rows, :], vv)
            if not last:
                a_ref[rows, :] = next_logits(rows)
            m_new = jnp.maximum(ms[h], jnp.max(ah, axis=-1, keepdims=True))
            p_ref[rows, :] = jnp.exp2(ah - m_new).astype(BF16)
            acc_ref[rows, :] = jnp.exp2(ms[h] - m_new) * (acc_ref[rows, :] + pv)
            new_ms.append(m_new)
        return tuple(new_ms)

    first_logits = tile_logits(0)
    for rows in heads:
        a_ref[rows, :] = first_logits(rows)
    m0 = tuple(jnp.full((tq, 1), NEG, F32) for _ in range(N_HEADS))
    ms = lax.fori_loop(0, n_kt - 1, lambda kt, ms: step(kt, ms, False), m0)
    step(n_kt - 1, ms, True)
    vv = vals_fn(n_kt - 1)
    for rows in heads:
        acc_ref[rows, :] += _dot(p_ref[rows, :], vv)


def _softmax2_rows(a, mask):
    a = jnp.where(mask, a, NEG)
    m = jnp.max(a, axis=-1, keepdims=True)
    e = jnp.where(mask, jnp.exp2(a - m), 0.0)
    s = jnp.sum(e, axis=-1, keepdims=True)
    return e / jnp.maximum(s, 1e-30)


def _split3(x):
    hi = x.astype(BF16)
    r1 = x - hi.astype(F32)
    mid = r1.astype(BF16)
    lo = (r1 - mid.astype(F32)).astype(BF16)
    return hi, mid, lo


def _nsa_kernel(q_ref, sm_ref, kc_ref, vc_ref, sk_ref, sv_ref, wk_ref, wv_ref, pf_ref, qf_ref, vf_ref,
                o_ref, q4_ref, p4_ref, acc_ref, selm_ref, p_ref, a_ref, ow_ref):
    tq = q_ref.shape[1]
    s_len = sk_ref.shape[1]
    n_cmp = kc_ref.shape[1]
    n_sel = s_len // SEL_BLOCK
    assert n_sel <= LANES - BLOCK_COL
    n_top = min(SEL_TOPK, n_sel)
    t0 = pl.program_id(1) * tq
    _stack_queries(q_ref, qf_ref, q4_ref)
    t_col = t0 + lax.broadcasted_iota(I32, (tq, 1), 0)

    a_all = _dot_nt(q4_ref[:, 0:HEAD_DIM], kc_ref[0])
    c_idx = lax.broadcasted_iota(I32, (1, n_cmp), 1)
    cd = t_col - (c_idx * CMP_STRIDE + (CMP_BLOCK - 1))
    cmask = (cd >= 0) & (c_idx < n_cmp - 1)
    cdf = cd.astype(F32)
    p_sum = jnp.zeros((tq, n_cmp), F32)
    for h in range(N_HEADS):
        p = _softmax2_rows(a_all[h * tq:(h + 1) * tq] - (SLOPES_NSA[h] * LOG2E) * cdf, cmask)
        p_sum = p_sum + p
        p4_ref[h * tq:(h + 1) * tq, 0:n_cmp] = p.astype(BF16)
    o_cmp = _dot(p4_ref[:, 0:n_cmp], vc_ref[0])

    wlen = min(NSA_WINDOW + tq, s_len)
    ks = pl.multiple_of(jnp.maximum(t0 + tq - wlen, 0), LANES)
    kw = jnp.concatenate([wk_ref[0, pl.ds(ks, wlen), :], pf_ref[pl.ds(ks, wlen), :]], axis=1)
    vw = jnp.concatenate([wv_ref[0, pl.ds(ks, wlen), :], vf_ref[0:wlen, :]], axis=1)
    wd = t_col - (ks + lax.broadcasted_iota(I32, (1, wlen), 1))
    wmask = (wd >= 0) & (wd < NSA_WINDOW)
    for h in range(N_HEADS):
        rows = slice(h * tq, (h + 1) * tq)
        ah = jnp.where(wmask, _dot_nt(q4_ref[rows, :], kw), NEG)
        m = jnp.max(ah, axis=-1, keepdims=True)
        ow_ref[rows, :] = _dot(jnp.exp2(ah - m).astype(BF16), vw)

    n_selp = selm_ref.shape[0]
    jj = lax.broadcasted_iota(I32, (n_selp, n_cmp), 0)
    cc = lax.broadcasted_iota(I32, (n_selp, n_cmp), 1)
    c_start = cc * CMP_STRIDE
    overlap = ((c_start < (jj + 1) * SEL_BLOCK) & (c_start + (CMP_BLOCK - 1) >= jj * SEL_BLOCK)
               & (cc < n_cmp - 1))
    ov = jnp.where(overlap, 1.0, 0.0).astype(BF16)
    hi, mid, lo = _split3(p_sum)
    imp = _dot_nt(ov, hi) + _dot_nt(ov, mid) + _dot_nt(ov, lo)
    j_col = lax.broadcasted_iota(I32, (n_selp, 1), 0)
    t_blk = (t0 + lax.broadcasted_iota(I32, (1, tq), 1)) // SEL_BLOCK
    forced = (j_col == 0) | (j_col == t_blk) | (j_col == t_blk - 1)
    imp = jnp.where(forced, jnp.inf, imp)
    imp = jnp.where(j_col <= t_blk, imp, -jnp.inf)
    rank = jnp.zeros((n_selp, tq), F32)
    for i2 in range(n_sel):
        ci = imp[i2:i2 + 1, :]
        tie_first = jnp.where(j_col > i2, 1.0, 0.0)
        rank = rank + jnp.where(ci > imp, 1.0, jnp.where(ci == imp, tie_first, 0.0))
    selm_ref[...] = jnp.where((rank < n_top) & (j_col < n_sel), 0.0, NEG)
    sel_bias = selm_ref[...].T[:, 0:LANES - BLOCK_COL].astype(BF16)
    for h in range(N_HEADS):
        q4_ref[h * tq:(h + 1) * tq, HEAD_DIM + BLOCK_COL:2 * HEAD_DIM] = sel_bias

    tk = min(ATT_TK, s_len)
    n_kt = (t0 + tq - 1) // tk + 1

    def sel_keys(kt):
        s0 = pl.multiple_of(kt * tk, tk)
        return jnp.concatenate([sk_ref[0, pl.ds(s0, tk), :], pf_ref[pl.ds(s0, tk), :]], axis=1)

    def sel_vals(kt):
        s0 = pl.multiple_of(kt * tk, tk)
        return jnp.concatenate([sv_ref[0, pl.ds(s0, tk), :], vf_ref[0:tk, :]], axis=1)

    def causal_last():
        return (n_kt - 1) * tk + lax.broadcasted_iota(I32, (1, tk), 1) <= t_col

    _flash_attention(q4_ref, acc_ref, p_ref, a_ref, n_kt, sel_keys, sel_vals, tq, last_mask_fn=causal_last)

    gates = jax.nn.sigmoid(sm_ref[0].astype(F32))
    for h in range(N_HEADS):
        rows = slice(h * tq, (h + 1) * tq)
        o_slc = acc_ref[rows, 0:HEAD_DIM] / jnp.maximum(acc_ref[rows, HEAD_DIM:HEAD_DIM + 1], 1e-30)
        o_w = ow_ref[rows, 0:HEAD_DIM] / jnp.maximum(ow_ref[rows, HEAD_DIM:HEAD_DIM + 1], 1e-30)
        g0 = gates[:, NG_LANE + 3 * h:NG_LANE + 3 * h + 1]
        g1 = gates[:, NG_LANE + 3 * h + 1:NG_LANE + 3 * h + 2]
        g2 = gates[:, NG_LANE + 3 * h + 2:NG_LANE + 3 * h + 3]
        o = g0 * o_cmp[rows] + g1 * o_slc + g2 * o_w
        o_ref[0, :, h * HEAD_DIM:(h + 1) * HEAD_DIM] = o.astype(BF16)


def _nsa_mixer(proj, k_cmp, v_cmp):
    b, s, _ = proj.shape
    tq = min(ATT_TQ, s)
    tk = min(ATT_TK, s)
    n_cmp = k_cmp.shape[1]
    wlen = min(NSA_WINDOW + tq, s)
    col = lambda name: _PACK_OFF[name] // HEAD_DIM
    full = lambda name: pl.BlockSpec((1, s, HEAD_DIM), functools.partial(lambda c, bi, i: (bi, 0, c), col(name)))
    nv = max(tk, wlen)
    return pl.pallas_call(
        _nsa_kernel,
        out_shape=jax.ShapeDtypeStruct((b, s, W_MIX), BF16),
        grid=(b, s // tq),
        in_specs=[
            pl.BlockSpec((1, tq, W_MIX), lambda bi, i: (bi, i, _PACK_OFF['n_q'] // W_MIX)),
            pl.BlockSpec((1, tq, LANES), lambda bi, i: (bi, i, SMALL_OFF // LANES)),
            pl.BlockSpec((1, n_cmp, HEAD_DIM), lambda bi, i: (bi, 0, 0)),
            pl.BlockSpec((1, n_cmp, HEAD_DIM), lambda bi, i: (bi, 0, 0)),
            full('n_sk'), full('n_sv'), full('n_wk'), full('n_wv'),
            _resident((s, LANES)), _resident((N_HEADS * tq, LANES)), _resident((nv, LANES)),
        ],
        out_specs=pl.BlockSpec((1, tq, W_MIX), lambda bi, i: (bi, i, 0)),
        scratch_shapes=[pltpu.VMEM((N_HEADS * tq, 2 * HEAD_DIM), BF16),
                        pltpu.VMEM((N_HEADS * tq, n_cmp), BF16),
                        pltpu.VMEM((N_HEADS * tq, 2 * HEAD_DIM), F32),
                        pltpu.VMEM((((s // SEL_BLOCK + LANES - 1) // LANES) * LANES, tq), F32),
                        pltpu.VMEM((N_HEADS * tq, tk), BF16),
                        pltpu.VMEM((N_HEADS * tq, tk), F32),
                        pltpu.VMEM((N_HEADS * tq, 2 * HEAD_DIM), F32)],
        compiler_params=_cparams(("parallel", "arbitrary")),
        name="nsa_attention",
    )(proj, proj, k_cmp, v_cmp, proj, proj, proj, proj,
      _pos_features(s, block_onehot=True), _slope_features(SLOPES_NSA, tq), _ones_feature(nv))


I16 = jnp.int16
I16_MIN = -2 ** 15
SEL_ROWS = 64
SCORE_CHAINS = 4


def _dsa_kernel(q_ref, iq_ref, sm_ref, smf_ref, k_ref, v_ref, pf_ref, qf_ref, vf_ref, o_ref,
                q4_ref, acc_ref, hi_ref, lo_ref, lq_ref, selb_ref, p_ref, a_ref):
    tq = q_ref.shape[1]
    s_len = k_ref.shape[1]
    topk = min(DSA_TOPK_MAX, s_len // 4)
    tk = min(ATT_TK, s_len)
    n_ch = tk // SEL_ROWS
    t0 = pl.program_id(1) * tq
    n_kt = (t0 + tq - 1) // tk + 1
    _stack_queries(q_ref, qf_ref, q4_ref)
    t_row = t0 + lax.broadcasted_iota(I32, (1, tq), 1)
    iq = iq_ref[0]
    iw_t = sm_ref[0].astype(F32).T
    one = jnp.ones((), BF16)
    zero = jnp.zeros((), BF16)

    def score_body(kt, _):
        tc = tk // SCORE_CHAINS
        for c in range(SCORE_CHAINS):
            s0 = pl.multiple_of(kt * tk + c * tc, tc)
            ik = smf_ref[0, pl.ds(s0, tc), IK_LANE:IK_LANE + IDX_DIM]
            sc = jnp.zeros((tc, tq), F32)
            for h in range(IDX_HEADS):
                lg = _dot_nt(ik, iq[:, h * IDX_DIM:(h + 1) * IDX_DIM])
                sc = sc + jnp.maximum(lg, 0.0) * iw_t[IW_LANE + h:IW_LANE + h + 1, :]
            sc = jnp.where(sc == 0.0, 0.0, sc)
            bits = lax.bitcast_convert_type(sc, I32)
            key = bits ^ ((bits >> 31) & 0x7FFFFFFF)
            s_pos = s0 + lax.broadcasted_iota(I32, (tc, 1), 0)
            key = jnp.where(s_pos <= t_row, key, INT_MIN)
            hi_ref[pl.ds(s0, tc), :] = (key >> 16).astype(I16)
            lo_ref[pl.ds(s0, tc), :] = ((key & 0xFFFF) + I16_MIN).astype(I16)
        return 0

    lax.fori_loop(0, n_kt, score_body, 0)

    def count_ge(ref, thr_row):
        thr = jnp.broadcast_to(thr_row, (SEL_ROWS, tq))

        def body(kt, cnt):
            s0 = pl.multiple_of(kt * tk, tk)
            for c in range(n_ch):
                cnt = cnt + jnp.where(ref[pl.ds(s0 + c * SEL_ROWS, SEL_ROWS), :] >= thr, one, zero)
            return cnt
        cnt = lax.fori_loop(0, n_kt, body, jnp.zeros((SEL_ROWS, tq), BF16))
        return jnp.sum(cnt.astype(F32), axis=0, keepdims=True)

    def kth_largest(ref, k):
        def bit_body(it, carry):
            cand, cnt = carry
            trial = cand | jnp.left_shift(jnp.int32(1), 15 - it)
            total = count_ge(ref, (trial + I16_MIN).astype(I16))
            ok = total >= k
            return jnp.where(ok, trial, cand), jnp.where(ok, total, cnt)
        return lax.fori_loop(0, 16, bit_body, (jnp.zeros((1, tq), I32), jnp.zeros((1, tq), F32)))

    raw_hi, n_ge_hi = kth_largest(hi_ref, float(topk))
    cand_hi = jnp.maximum(raw_hi, 1)
    p16 = jnp.broadcast_to((cand_hi + I16_MIN).astype(I16), (SEL_ROWS, tq))
    n_above = count_ge(hi_ref, (jnp.minimum(cand_hi + 1, 2 ** 16 - 1) + I16_MIN).astype(I16))
    need = float(topk) - n_above

    def tie_body(kt, _):
        s0 = pl.multiple_of(kt * tk, tk)
        for c in range(n_ch):
            ds = pl.ds(s0 + c * SEL_ROWS, SEL_ROWS)
            lq_ref[ds, :] = jnp.where(hi_ref[ds, :] == p16, lo_ref[ds, :], jnp.full((), I16_MIN, I16))
        return 0

    lax.fori_loop(0, n_kt, tie_body, 0)
    cand_lo, n_ge_lo = kth_largest(lq_ref, need)
    q16_row = (cand_lo + I16_MIN).astype(I16)
    q16 = jnp.broadcast_to(q16_row, (SEL_ROWS, tq))

    def sel_body(kt, _):
        s0 = pl.multiple_of(kt * tk, tk)
        for c in range(n_ch):
            ds = pl.ds(s0 + c * SEL_ROWS, SEL_ROWS)
            hi = hi_ref[ds, :]
            tie = jnp.where(hi == p16, jnp.where(lo_ref[ds, :] >= q16, one, zero), zero)
            selb_ref[ds, :] = jnp.where(hi > p16, one, tie)
        return 0

    lax.fori_loop(0, n_kt, sel_body, 0)

    n_gt = jnp.where(cand_lo >= 2 ** 16 - 1, 0.0,
                     count_ge(lq_ref, (jnp.minimum(cand_lo + 1, 2 ** 16 - 1) + I16_MIN).astype(I16)))
    n_with_hi = jnp.where(raw_hi > 0, n_ge_hi - n_above, 0.0)
    n_ge = jnp.where(cand_lo > 0, n_ge_lo, n_with_hi)
    quota = need - n_gt
    any_tied = jnp.max(jnp.where(n_ge - n_gt > quota, 1.0, 0.0)) > 0.0

    @pl.when(any_tied)
    def _():
        p_t = jnp.broadcast_to(p16[0:1, :], (tk, tq))
        q_t = jnp.broadcast_to(q16_row, (tk, tq))
        lower = jnp.where(lax.broadcasted_iota(I32, (tk, tk), 0) >= lax.broadcasted_iota(I32, (tk, tk), 1),
                          1.0, 0.0).astype(BF16)

        def exact_ties(kt):
            ds = pl.ds(pl.multiple_of(kt * tk, tk), tk)
            exact = jnp.where(hi_ref[ds, :] == p_t, jnp.where(lo_ref[ds, :] == q_t, one, zero), zero)
            return ds, exact, _dot(lower, exact)

        def drop(ds, exact, rank):
            selb_ref[ds, :] = selb_ref[ds, :] - jnp.where(rank > quota, 1.0, 0.0).astype(BF16) * exact

        def drop_pair(j, seen):
            ds_a, exact_a, in_a = exact_ties(2 * j)
            ds_b, exact_b, in_b = exact_ties(2 * j + 1)
            rank_a = in_a + seen
            rank_b = in_b + rank_a[tk - 1:tk, :]
            drop(ds_a, exact_a, rank_a)
            drop(ds_b, exact_b, rank_b)
            return rank_b[tk - 1:tk, :]

        seen = lax.fori_loop(0, n_kt // 2, drop_pair, jnp.zeros((1, tq), F32))

        @pl.when(n_kt % 2 == 1)
        def _():
            ds, exact, in_tile = exact_ties(n_kt - 1)
            drop(ds, exact, in_tile + seen)

    def att_keys(kt):
        s0 = pl.multiple_of(kt * tk, tk)
        return jnp.concatenate([k_ref[0, pl.ds(s0, tk), :], pf_ref[pl.ds(s0, tk), :]], axis=1)

    def att_vals(kt):
        s0 = pl.multiple_of(kt * tk, tk)
        return jnp.concatenate([v_ref[0, pl.ds(s0, tk), :], vf_ref[...]], axis=1)

    def att_mask(kt):
        s0 = pl.multiple_of(kt * tk, tk)
        return selb_ref[pl.ds(s0, tk), :].astype(F32).T > 0.5

    _flash_attention(q4_ref, acc_ref, p_ref, a_ref, n_kt, att_keys, att_vals, tq, mask_fn=att_mask)
    for h in range(N_HEADS):
        rows = slice(h * tq, (h + 1) * tq)
        o = acc_ref[rows, 0:HEAD_DIM] / jnp.maximum(acc_ref[rows, HEAD_DIM:HEAD_DIM + 1], 1e-30)
        o_ref[0, :, h * HEAD_DIM:(h + 1) * HEAD_DIM] = o.astype(BF16)


def _dsa_mixer(proj):
    b, s, _ = proj.shape
    tq = min(ATT_TQ, s)
    tk = min(ATT_TK, s)
    full = lambda off: pl.BlockSpec((1, s, LANES), functools.partial(lambda c, bi, i: (bi, 0, c), off // LANES))
    return pl.pallas_call(
        _dsa_kernel,
        out_shape=jax.ShapeDtypeStruct((b, s, W_MIX), BF16),
        grid=(b, s // tq),
        in_specs=[
            pl.BlockSpec((1, tq, W_MIX), lambda bi, i: (bi, i, _PACK_OFF['d_q'] // W_MIX)),
            pl.BlockSpec((1, tq, W_MIX), lambda bi, i: (bi, i, _PACK_OFF['i_q'] // W_MIX)),
            pl.BlockSpec((1, tq, LANES), lambda bi, i: (bi, i, SMALL_OFF // LANES)),
            full(SMALL_OFF), full(_PACK_OFF['d_k']), full(_PACK_OFF['d_v']),
            _resident((s, LANES)), _resident((N_HEADS * tq, LANES)), _resident((tk, LANES)),
        ],
        out_specs=pl.BlockSpec((1, tq, W_MIX), lambda bi, i: (bi, i, 0)),
        scratch_shapes=[pltpu.VMEM((N_HEADS * tq, 2 * HEAD_DIM), BF16),
                        pltpu.VMEM((N_HEADS * tq, 2 * HEAD_DIM), F32),
                        pltpu.VMEM((s, tq), I16), pltpu.VMEM((s, tq), I16), pltpu.VMEM((s, tq), I16),
                        pltpu.VMEM((s, tq), BF16),
                        pltpu.VMEM((N_HEADS * tq, tk), BF16),
                        pltpu.VMEM((N_HEADS * tq, tk), F32)],
        compiler_params=_cparams(("parallel", "arbitrary")),
        name="dsa_attention",
    )(proj, proj, proj, proj, proj, proj,
      _pos_features(s), _slope_features(SLOPES_DSA, tq), _ones_feature(tk))


OUTPROJ_TM = 512
LN_ROWS = 256


def _outproj_kernel(y0_ref, y1_ref, y2_ref, y3_ref, w_ref, x_ref, g_ref, b_ref, o_ref, ob_ref):
    for r0 in range(0, x_ref.shape[0], LN_ROWS):
        rows = slice(r0, r0 + LN_ROWS)
        acc = DN_ALPHA * x_ref[rows, :]
        for gi, y_ref in enumerate((y0_ref, y1_ref, y2_ref, y3_ref)):
            acc = acc + _dot(y_ref[rows, :], w_ref[gi * W_MIX:(gi + 1) * W_MIX, :])
        y = _layernorm(acc, g_ref[...], b_ref[...])
        o_ref[rows, :] = y
        ob_ref[rows, :] = y.astype(BF16)


def _outproj_ln(ys, w_out, x, g, b, l):
    n_tok, d = x.shape
    tm = OUTPROJ_TM
    yspec = pl.BlockSpec((tm, W_MIX), lambda i: (i, 0))
    xspec = pl.BlockSpec((tm, d), lambda i: (i, 0))
    return pl.pallas_call(
        _outproj_kernel,
        out_shape=(jax.ShapeDtypeStruct((n_tok, d), F32), jax.ShapeDtypeStruct((n_tok, d), BF16)),
        grid=(n_tok // tm,),
        in_specs=[yspec, yspec, yspec, yspec, _of_layer(w_out, l), xspec, _of_layer(g, l), _of_layer(b, l)],
        out_specs=(xspec, xspec),
        compiler_params=_cparams(("parallel",)),
        name="outproj_ln",
    )(*ys, w_out, x, g, b)


FFN_UP_TM = 1024
FFN_TF = 512
FFN_DOWN_TM = 256


def _ffn_up_kernel(xb_ref, wg_ref, wu_ref, h_ref, wgb_ref, wub_ref):
    @pl.when(pl.program_id(1) == 0)
    def _():
        wgb_ref[...] = wg_ref[...].astype(BF16)
        wub_ref[...] = wu_ref[...].astype(BF16)

    xb = xb_ref[...]
    gate = _dot(xb, wgb_ref[...])
    up = _dot(xb, wub_ref[...])
    h_ref[...] = (gate * jax.nn.sigmoid(gate) * up).astype(BF16)


def _ffn_down_kernel(h_ref, wd_ref, x_ref, g_ref, b_ref, *rest):
    *out_refs, acc_a, acc_b = rest
    i = pl.program_id(0)

    @pl.when(i == 0)
    def _():
        acc_b[...] = jnp.zeros(acc_b.shape, F32)

    def step(acc_prev, acc_next):
        y = _layernorm(acc_prev[...], g_ref[...], b_ref[...])
        out_refs[0][...] = y
        if len(out_refs) > 1:
            out_refs[1][...] = y.astype(BF16)
        acc_next[...] = DN_ALPHA * x_ref[...] + _dot(h_ref[...], wd_ref[...])

    @pl.when(i % 2 == 0)
    def _():
        step(acc_b, acc_a)

    @pl.when(i % 2 == 1)
    def _():
        step(acc_a, acc_b)


def _ffn_ln(xb, w_gate_up, w_down, x, g, b, l, want_bf16):
    n_tok, d = x.shape
    d_ff = w_down.shape[1]
    tm, tf = FFN_UP_TM, FFN_TF
    nf = d_ff // tf
    h = pl.pallas_call(
        _ffn_up_kernel,
        out_shape=jax.ShapeDtypeStruct((n_tok, d_ff), BF16),
        grid=(nf, n_tok // tm),
        in_specs=[
            pl.BlockSpec((tm, d), lambda f, i: (i, 0)),
            pl.BlockSpec((None, d, tf), lambda f, i: (l, 0, f)),
            pl.BlockSpec((None, d, tf), lambda f, i: (l, 0, f + nf)),
        ],
        out_specs=pl.BlockSpec((tm, tf), lambda f, i: (i, f)),
        scratch_shapes=[pltpu.VMEM((d, tf), BF16), pltpu.VMEM((d, tf), BF16)],
        compiler_params=_cparams(("parallel", "arbitrary")),
        name="ffn_up",
    )(xb, w_gate_up, w_gate_up)
    tm = FFN_DOWN_TM
    n_tiles = n_tok // tm
    in_row = lambda i: (jnp.minimum(i, n_tiles - 1), 0)
    out_row = lambda i: (jnp.maximum(i - 1, 0), 0)
    out_dtypes = (F32, BF16) if want_bf16 else (F32,)
    outs = pl.pallas_call(
        _ffn_down_kernel,
        out_shape=tuple(jax.ShapeDtypeStruct((n_tok, d), t) for t in out_dtypes),
        grid=(n_tiles + 1,),
        in_specs=[pl.BlockSpec((tm, d_ff), in_row), _of_layer(w_down, l),
                  pl.BlockSpec((tm, d), in_row), _of_layer(g, l), _of_layer(b, l)],
        out_specs=tuple(pl.BlockSpec((tm, d), out_row) for _ in out_dtypes),
        scratch_shapes=[pltpu.VMEM((tm, d), F32), pltpu.VMEM((tm, d), F32)],
        compiler_params=_cparams(("arbitrary",)),
        name="ffn_down_ln",
    )(h, w_down, x, g, b)
    return outs if want_bf16 else (outs[0], None)


PACK_ROWS = 256
_FOLD = {'d_q': ATTN_SCALE * LOG2E, 'n_q': ATTN_SCALE * LOG2E, 'i_w': IDX_SCALE}


def _pack_kernel(w_ref, o_ref):
    for n in _PACK_ORDER:
        src, dst, size = _SRC_OFF[n], _PACK_OFF[n], _SRC_SIZE[n]
        o_ref[:, dst:dst + size] = (w_ref[:, src:src + size] * _FOLD.get(n, 1.0)).astype(BF16)
    used = sum(_SRC_SIZES)
    o_ref[:, used:D_PACK] = jnp.zeros((o_ref.shape[0], D_PACK - used), BF16)


def _pack_w_in(w):
    n_layers, d, d_in = w.shape
    return pl.pallas_call(
        _pack_kernel,
        out_shape=jax.ShapeDtypeStruct((n_layers, d, D_PACK), BF16),
        grid=(n_layers, d // PACK_ROWS),
        in_specs=[pl.BlockSpec((None, PACK_ROWS, d_in), lambda l, i: (l, i, 0))],
        out_specs=pl.BlockSpec((None, PACK_ROWS, D_PACK), lambda l, i: (l, i, 0)),
        compiler_params=_cparams(("parallel", "parallel")),
        name="pack_w_in",
    )(w)


def _prepare(p):
    row = lambda v: v[:, None, :].astype(F32)
    n_layers = p['w_in'].shape[0]
    flat = lambda v: v.reshape(n_layers, 1, -1).astype(BF16)
    return dict(
        w_pack=_pack_w_in(p['w_in']), w_out=p['w_out'].astype(BF16),
        pool_w=p['pool_w'].astype(BF16), pool_scale=row(p['pool_scale']),
        conv_w=jnp.concatenate([p['conv_w'].astype(F32), jnp.zeros((n_layers, 1, W_MIX), F32)], axis=1),
        conv_b=row(p['conv_b']), conv_ln_g=row(p['conv_ln_g']), conv_ln_b=row(p['conv_ln_b']),
        conv_pw_w=p['conv_pw_w'].astype(BF16), conv_pw_b=row(p['conv_pw_b']),
        cmp_pos_k=flat(p['cmp_pos_k']), cmp_pos_v=flat(p['cmp_pos_v']),
        cmp_k_w1=p['cmp_k_w1'].astype(BF16), cmp_k_w2=p['cmp_k_w2'].astype(BF16),
        cmp_v_w1=p['cmp_v_w1'].astype(BF16), cmp_v_w2=p['cmp_v_w2'].astype(BF16),
        ln1_g=row(p['ln1_g']), ln1_b=row(p['ln1_b']), ln2_g=row(p['ln2_g']), ln2_b=row(p['ln2_b']),
        w_gate_up=p['w_gate_up'], w_down=p['w_down'].astype(BF16))


def _layer(x, xb, p, l, last):
    b, s, d = x.shape
    n_tok = b * s
    x_in = x if xb is None else xb
    proj, ck, cv = _inproj(x_in.reshape(n_tok, d), p['w_pack'], l)
    proj = proj.reshape(b, s, D_PACK)
    y_pool = _pool_mixer(proj, p['pool_w'], p['pool_scale'], l)
    y_conv = _conv_mixer(proj, p['conv_w'], p['conv_b'], p['conv_ln_g'], p['conv_ln_b'],
                         p['conv_pw_w'], p['conv_pw_b'], l)
    y_dsa = _dsa_mixer(proj)
    k_cmp = _compress(ck.reshape(b, s, HEAD_DIM), p['cmp_pos_k'], p['cmp_k_w1'], p['cmp_k_w2'], l)
    v_cmp = _compress(cv.reshape(b, s, HEAD_DIM), p['cmp_pos_v'], p['cmp_v_w1'], p['cmp_v_w2'], l)
    y_nsa = _nsa_mixer(proj, k_cmp, v_cmp)
    ys = [y.reshape(n_tok, W_MIX) for y in (y_pool, y_conv, y_dsa, y_nsa)]
    x1, x1b = _outproj_ln(ys, p['w_out'], x.reshape(n_tok, d), p['ln1_g'], p['ln1_b'], l)
    x2, x2b = _ffn_ln(x1b, p['w_gate_up'], p['w_down'], x1, p['ln2_g'], p['ln2_b'], l,
                      want_bf16=not last)
    return x2.reshape(b, s, d), None if last else x2b.reshape(b, s, d)


def kernel(x, w_in, w_out, pool_w, pool_scale, conv_w, conv_b, conv_ln_g, conv_ln_b, conv_pw_w, conv_pw_b,
           cmp_pos_k, cmp_pos_v, cmp_k_w1, cmp_k_w2, cmp_v_w1, cmp_v_w2, ln1_g, ln1_b, ln2_g, ln2_b,
           w_gate_up, w_down):
    params = dict(w_in=w_in, w_out=w_out, pool_w=pool_w, pool_scale=pool_scale, conv_w=conv_w, conv_b=conv_b,
                  conv_ln_g=conv_ln_g, conv_ln_b=conv_ln_b, conv_pw_w=conv_pw_w, conv_pw_b=conv_pw_b,
                  cmp_pos_k=cmp_pos_k, cmp_pos_v=cmp_pos_v, cmp_k_w1=cmp_k_w1, cmp_k_w2=cmp_k_w2,
                  cmp_v_w1=cmp_v_w1, cmp_v_w2=cmp_v_w2, ln1_g=ln1_g, ln1_b=ln1_b, ln2_g=ln2_g, ln2_b=ln2_b,
                  w_gate_up=w_gate_up, w_down=w_down)
    prepared = _prepare(params)
    xb = None
    for l in range(w_in.shape[0]):
        x, xb = _layer(x, xb, prepared, l, last=l == w_in.shape[0] - 1)
    return x
```

```python
import functools

import numpy as np
import jax
import jax.numpy as jnp
from jax import lax
from jax.experimental import pallas as pl
from jax.experimental.pallas import tpu as pltpu

F32 = jnp.float32
BF16 = jnp.bfloat16
I32 = jnp.int32

D_MODEL = 2048
DEPTH = 2
W_MIX = D_MODEL // 4
HEAD_DIM = 128
N_HEADS = W_MIX // HEAD_DIM
POOL_WINDOWS = (2, 4, 8, 16)
POOL_GROUP = W_MIX // len(POOL_WINDOWS)
CONV_WIDTH = 31
IDX_HEADS = 8
IDX_DIM = 64
DSA_TOPK_MAX = 256
CMP_BLOCK = 32
CMP_STRIDE = 16
SEL_BLOCK = 64
SEL_TOPK = 16
NSA_WINDOW = 512
DN_ALPHA = (2 * DEPTH) ** 0.25
ATTN_SCALE = HEAD_DIM ** -0.5
IDX_SCALE = (IDX_HEADS * IDX_DIM) ** -0.5
LN_EPS = 1e-5
LOG2E = 1.4426950408889634

LANES = 128
SUBLANES = 8
VMEM_LIMIT = 48 * 1024 * 1024

NEG = -1e30
INT_MIN = -2 ** 31

_SRC_SIZES = (W_MIX, W_MIX, W_MIX, W_MIX, HEAD_DIM, HEAD_DIM, IDX_HEADS * IDX_DIM, IDX_DIM, IDX_HEADS,
              W_MIX, HEAD_DIM, HEAD_DIM, HEAD_DIM, HEAD_DIM, HEAD_DIM, HEAD_DIM, N_HEADS * 3)
_SRC_NAMES = ('pool', 'c_a', 'c_g', 'd_q', 'd_k', 'd_v', 'i_q', 'i_k', 'i_w',
              'n_q', 'n_ck', 'n_cv', 'n_sk', 'n_sv', 'n_wk', 'n_wv', 'n_g')
_SRC_OFF = dict(zip(_SRC_NAMES, np.concatenate([[0], np.cumsum(_SRC_SIZES)[:-1]]).tolist()))
_SRC_SIZE = dict(zip(_SRC_NAMES, _SRC_SIZES))
_PACK_ORDER = ('pool', 'c_a', 'c_g', 'd_q', 'i_q', 'n_q', 'd_k', 'd_v',
               'n_ck', 'n_cv', 'n_sk', 'n_sv', 'n_wk', 'n_wv', 'i_k', 'i_w', 'n_g')
_PACK_OFF = {}
_o = 0
for _n in _PACK_ORDER:
    _PACK_OFF[_n] = _o
    _o += _SRC_SIZE[_n]
D_PACK = ((_o + LANES - 1) // LANES) * LANES
SMALL_OFF = _PACK_OFF['i_k']
IK_LANE = 0
IW_LANE = _PACK_OFF['i_w'] - SMALL_OFF
NG_LANE = _PACK_OFF['n_g'] - SMALL_OFF


def _alibi_slopes():
    n = 2 * N_HEADS
    s = np.power(2.0, -8.0 * np.arange(1, n + 1) / n).astype(np.float32)
    return [float(v) for v in s[0::2]], [float(v) for v in s[1::2]]


SLOPES_DSA, SLOPES_NSA = _alibi_slopes()


def _cparams(sem):
    return pltpu.CompilerParams(dimension_semantics=sem, vmem_limit_bytes=VMEM_LIMIT)


def _resident(shape):
    nd = len(shape)
    return pl.BlockSpec(shape, lambda *_: (0,) * nd, pipeline_mode=pl.Buffered(1))


def _of_layer(arr, l):
    nd = arr.ndim
    return pl.BlockSpec((None,) + arr.shape[1:], lambda *_: (l,) + (0,) * (nd - 1),
                        pipeline_mode=pl.Buffered(1))


def _dot(a, b):
    return jnp.dot(a, b, preferred_element_type=F32)


def _dot_nt(a, b):
    return lax.dot_general(a, b, (((1,), (1,)), ((), ())), preferred_element_type=F32)


def _layernorm(x, g, b):
    mu = jnp.mean(x, axis=-1, keepdims=True)
    xc = x - mu
    var = jnp.mean(xc * xc, axis=-1, keepdims=True)
    return xc * lax.rsqrt(var + LN_EPS) * g + b


INPROJ_TM = 512
INPROJ_CHUNK = 512


def _inproj_kernel(x_ref, w_ref, o_ref, ck_ref, cv_ref):
    x = x_ref[...].astype(BF16)
    n = o_ref.shape[1]
    for c0 in range(0, n, INPROJ_CHUNK):
        c1 = min(c0 + INPROJ_CHUNK, n)
        o_ref[:, c0:c1] = _dot(x, w_ref[:, c0:c1]).astype(BF16)
    ck_ref[...] = o_ref[:, _PACK_OFF['n_ck']:_PACK_OFF['n_ck'] + HEAD_DIM]
    cv_ref[...] = o_ref[:, _PACK_OFF['n_cv']:_PACK_OFF['n_cv'] + HEAD_DIM]


def _inproj(x, w_pack, l):
    n_tok, d = x.shape
    tm = INPROJ_TM
    col = pl.BlockSpec((tm, HEAD_DIM), lambda i: (i, 0))
    return pl.pallas_call(
        _inproj_kernel,
        out_shape=(jax.ShapeDtypeStruct((n_tok, D_PACK), BF16),
                   jax.ShapeDtypeStruct((n_tok, HEAD_DIM), BF16), jax.ShapeDtypeStruct((n_tok, HEAD_DIM), BF16)),
        grid=(n_tok // tm,),
        in_specs=[pl.BlockSpec((tm, d), lambda i: (i, 0)), _of_layer(w_pack, l)],
        out_specs=(pl.BlockSpec((tm, D_PACK), lambda i: (i, 0)), col, col),
        compiler_params=_cparams(("parallel",)),
        name="inproj",
    )(x, w_pack)


POOL_TS = 512
POOL_HALO = 16


def _pool_kernel(u_ref, halo_ref, w_ref, sc_ref, o_ref, xs_ref):
    i = pl.program_id(1)
    ts = u_ref.shape[1]
    xs_ref[POOL_HALO:POOL_HALO + ts, :] = u_ref[0].astype(F32)
    xs_ref[0:POOL_HALO, :] = jnp.where(i > 0, halo_ref[0].astype(F32), 0.0)
    pos = i * ts + lax.broadcasted_iota(I32, (ts, 1), 0)
    for g, win in enumerate(POOL_WINDOWS):
        c = slice(g * POOL_GROUP, (g + 1) * POOL_GROUP)
        x = xs_ref[POOL_HALO:POOL_HALO + ts, c]
        acc = x
        for k in range(1, win):
            acc = acc + xs_ref[POOL_HALO - k:POOL_HALO - k + ts, c]
        cnt = jnp.minimum(pos + 1, win).astype(F32)
        d = acc / cnt - x
        y = _dot(d.astype(BF16), w_ref[g])
        o_ref[0, :, c] = (y * sc_ref[:, c]).astype(BF16)


def _pool_mixer(proj, pool_w, pool_scale, l):
    b, s, _ = proj.shape
    ts = min(POOL_TS, s)
    hb = ts // POOL_HALO
    blk = _PACK_OFF['pool'] // W_MIX
    return pl.pallas_call(
        _pool_kernel,
        out_shape=jax.ShapeDtypeStruct((b, s, W_MIX), BF16),
        grid=(b, s // ts),
        in_specs=[
            pl.BlockSpec((1, ts, W_MIX), lambda bi, i: (bi, i, blk)),
            pl.BlockSpec((1, POOL_HALO, W_MIX), lambda bi, i: (bi, jnp.maximum(i * hb - 1, 0), blk)),
            _of_layer(pool_w, l),
            _of_layer(pool_scale, l),
        ],
        out_specs=pl.BlockSpec((1, ts, W_MIX), lambda bi, i: (bi, i, 0)),
        scratch_shapes=[pltpu.VMEM((ts + POOL_HALO, W_MIX), F32)],
        compiler_params=_cparams(("parallel", "parallel")),
        name="pool_mixer",
    )(proj, proj, pool_w, pool_scale)


CONV_TS = 512
CONV_HALO = 32
CONV_ROWS = 32


def _conv_kernel(a_ref, g_ref, ha_ref, hg_ref, cw_ref, cb_ref, lg_ref, lb_ref, pw_ref, pb_ref,
                 o_ref, hs_ref, sh_ref, y_ref):
    i = pl.program_id(1)
    ts = a_ref.shape[1]
    hs_ref[CONV_HALO:CONV_HALO + ts, :] = a_ref[0].astype(F32) * jax.nn.sigmoid(g_ref[0].astype(F32))
    halo = ha_ref[0].astype(F32) * jax.nn.sigmoid(hg_ref[0].astype(F32))
    hs_ref[0:CONV_HALO, :] = jnp.where(i > 0, halo, 0.0)
    n_sh = ts + CONV_HALO - SUBLANES
    for b in range(1, SUBLANES):
        sh_ref[b - 1, 0:n_sh, :] = hs_ref[b:b + n_sh, :]
    base = CONV_HALO - (CONV_WIDTH - 1)
    for r0 in range(0, ts, CONV_ROWS):
        acc = jnp.broadcast_to(cb_ref[...], (CONV_ROWS, W_MIX))
        for j in range(CONV_WIDTH):
            a8, b = divmod(base + j, SUBLANES)
            r = r0 + a8 * SUBLANES
            src = hs_ref[r:r + CONV_ROWS, :] if b == 0 else sh_ref[b - 1, r:r + CONV_ROWS, :]
            acc = acc + src * cw_ref[j:j + 1, :]
        y = _layernorm(acc, lg_ref[...], lb_ref[...])
        y_ref[r0:r0 + CONV_ROWS, :] = (y * jax.nn.sigmoid(y)).astype(BF16)
    o_ref[0] = (_dot(y_ref[...], pw_ref[...]) + pb_ref[...]).astype(BF16)


def _conv_mixer(proj, cw, conv_b, ln_g, ln_b, pw_w, pw_b, l):
    b, s, _ = proj.shape
    ts = min(CONV_TS, s)
    hb = ts // CONV_HALO
    ba = _PACK_OFF['c_a'] // W_MIX
    bg = _PACK_OFF['c_g'] // W_MIX
    halo_map = lambda blk: (lambda bi, i: (bi, jnp.maximum(i * hb - 1, 0), blk))
    return pl.pallas_call(
        _conv_kernel,
        out_shape=jax.ShapeDtypeStruct((b, s, W_MIX), BF16),
        grid=(b, s // ts),
        in_specs=[
            pl.BlockSpec((1, ts, W_MIX), lambda bi, i: (bi, i, ba)),
            pl.BlockSpec((1, ts, W_MIX), lambda bi, i: (bi, i, bg)),
            pl.BlockSpec((1, CONV_HALO, W_MIX), halo_map(ba)),
            pl.BlockSpec((1, CONV_HALO, W_MIX), halo_map(bg)),
            _of_layer(cw, l), _of_layer(conv_b, l), _of_layer(ln_g, l), _of_layer(ln_b, l),
            _of_layer(pw_w, l), _of_layer(pw_b, l),
        ],
        out_specs=pl.BlockSpec((1, ts, W_MIX), lambda bi, i: (bi, i, 0)),
        scratch_shapes=[pltpu.VMEM((ts + CONV_HALO, W_MIX), F32),
                        pltpu.VMEM((SUBLANES - 1, ts + CONV_HALO - SUBLANES, W_MIX), F32),
                        pltpu.VMEM((ts, W_MIX), BF16)],
        compiler_params=_cparams(("parallel", "parallel")),
        name="conv_mixer",
    )(proj, proj, proj, proj, cw, conv_b, ln_g, ln_b, pw_w, pw_b)


def _compress_kernel(r_ref, pos_ref, w1_ref, w2_ref, o_ref):
    r = r_ref[0]
    half = r.shape[1]
    n = r.shape[0]
    top = _dot(r, w1_ref[0:half, :])
    bot = _dot(r, w1_ref[half:2 * half, :])
    bot_next = pltpu.roll(bot, n - 1, 0)
    posb = _dot(jnp.broadcast_to(pos_ref[...], (8, 2 * half)), w1_ref[...])[0:1, :]
    h = jax.nn.gelu(top + bot_next + posb)
    out = _dot(h.astype(BF16), w2_ref[...])
    row = lax.broadcasted_iota(I32, out.shape, 0)
    o_ref[0] = jnp.where(row < n - 1, out, 0.0).astype(BF16)


def _compress(raw, pos, w1, w2, l):
    b, s, d = raw.shape
    n = s // CMP_STRIDE
    r = raw.reshape(b, n, CMP_STRIDE * d)
    return pl.pallas_call(
        _compress_kernel,
        out_shape=jax.ShapeDtypeStruct((b, n, d), BF16),
        grid=(b,),
        in_specs=[pl.BlockSpec((1, n, CMP_STRIDE * d), lambda bi: (bi, 0, 0)),
                  _of_layer(pos, l), _of_layer(w1, l), _of_layer(w2, l)],
        out_specs=pl.BlockSpec((1, n, d), lambda bi: (bi, 0, 0)),
        compiler_params=_cparams(("parallel",)),
        name="nsa_compress",
    )(r, pos, w1, w2)


ATT_TQ = 256
ATT_TK = 512
POS_RADIX = 64
BLOCK_COL = 64


def _pos_features(s_len, block_onehot=False):
    assert s_len <= POS_RADIX * 256
    s = np.arange(s_len)
    f = np.zeros((s_len, LANES), np.float32)
    f[:, 0] = f[:, 1] = s // POS_RADIX
    f[:, 2] = f[:, 3] = s % POS_RADIX
    if block_onehot:
        assert s_len // SEL_BLOCK <= LANES - BLOCK_COL
        f[s, BLOCK_COL + s // SEL_BLOCK] = 1.0
    return jnp.asarray(f, BF16)


def _slope_features(slopes, tq):
    f = np.zeros((len(slopes) * tq, LANES), np.float32)
    for h, sl in enumerate(slopes):
        c = np.float32(sl * LOG2E)
        ca = np.float32(np.asarray(c, dtype=BF16))
        cb = np.float32(np.asarray(c - ca, dtype=BF16))
        f[h * tq:(h + 1) * tq, 0:4] = [POS_RADIX * ca, POS_RADIX * cb, ca, cb]
    return jnp.asarray(f, BF16)


def _ones_feature(n):
    f = np.zeros((n, LANES), np.float32)
    f[:, 0] = 1.0
    return jnp.asarray(f, BF16)


def _stack_queries(q_ref, qf_ref, q4_ref):
    tq = q_ref.shape[1]
    for h in range(N_HEADS):
        q4_ref[h * tq:(h + 1) * tq, 0:HEAD_DIM] = q_ref[0, :, h * HEAD_DIM:(h + 1) * HEAD_DIM]
    q4_ref[:, HEAD_DIM:2 * HEAD_DIM] = qf_ref[...]


def _flash_attention(q4_ref, acc_ref, p_ref, a_ref, n_kt, keys_fn, vals_fn, tq, mask_fn=None,
                     last_mask_fn=None):
    acc_ref[...] = jnp.zeros(acc_ref.shape, F32)
    p_ref[...] = jnp.zeros(p_ref.shape, BF16)
    heads = [slice(h * tq, (h + 1) * tq) for h in range(N_HEADS)]

    def tile_logits(kt):
        kk = keys_fn(kt)
        if mask_fn is None:
            return lambda rows: _dot_nt(q4_ref[rows, :], kk)
        mask = mask_fn(kt)
        return lambda rows: jnp.where(mask, _dot_nt(q4_ref[rows, :], kk), NEG)

    def step(kt, ms, last):
        vv = vals_fn(jnp.maximum(kt - 1, 0))
        next_logits = None if last else tile_logits(kt + 1)
        last_mask = last_mask_fn() if (last and last_mask_fn) else None
        new_ms = []
        for h, rows in enumerate(heads):
            ah = a_ref[rows, :]
            if last_mask is not None:
                ah = jnp.where(last_mask, ah, NEG)
            pv = _dot(p_ref[rows, :], vv)
            if not last:
                a_ref[rows, :] = next_logits(rows)
            m_new = jnp.maximum(ms[h], jnp.max(ah, axis=-1, keepdims=True))
            p_ref[rows, :] = jnp.exp2(ah - m_new).astype(BF16)
            acc_ref[rows, :] = jnp.exp2(ms[h] - m_new) * (acc_ref[rows, :] + pv)
            new_ms.append(m_new)
        return tuple(new_ms)

    first_logits = tile_logits(0)
    for rows in heads:
        a_ref[rows, :] = first_logits(rows)
    m0 = tuple(jnp.full((tq, 1), NEG, F32) for _ in range(N_HEADS))
    ms = lax.fori_loop(0, n_kt - 1, lambda kt, ms: step(kt, ms, False), m0)
    step(n_kt - 1, ms, True)
    vv = vals_fn(n_kt - 1)
    for rows in heads:
        acc_ref[rows, :] += _dot(p_ref[rows, :], vv)


def _softmax2_rows(a, mask):
    a = jnp.where(mask, a, NEG)
    m = jnp.max(a, axis=-1, keepdims=True)
    e = jnp.where(mask, jnp.exp2(a - m), 0.0)
    s = jnp.sum(e, axis=-1, keepdims=True)
    return e / jnp.maximum(s, 1e-30)


def _split3(x):
    hi = x.astype(BF16)
    r1 = x - hi.astype(F32)
    mid = r1.astype(BF16)
    lo = (r1 - mid.astype(F32)).astype(BF16)
    return hi, mid, lo


def _nsa_kernel(q_ref, sm_ref, kc_ref, vc_ref, sk_ref, sv_ref, wk_ref, wv_ref, pf_ref, qf_ref, vf_ref,
                o_ref, q4_ref, p4_ref, acc_ref, selm_ref, p_ref, a_ref, ow_ref):
    tq = q_ref.shape[1]
    s_len = sk_ref.shape[1]
    n_cmp = kc_ref.shape[1]
    n_sel = s_len // SEL_BLOCK
    assert n_sel <= LANES - BLOCK_COL
    n_top = min(SEL_TOPK, n_sel)
    t0 = pl.program_id(1) * tq
    _stack_queries(q_ref, qf_ref, q4_ref)
    t_col = t0 + lax.broadcasted_iota(I32, (tq, 1), 0)

    a_all = _dot_nt(q4_ref[:, 0:HEAD_DIM], kc_ref[0])
    c_idx = lax.broadcasted_iota(I32, (1, n_cmp), 1)
    cd = t_col - (c_idx * CMP_STRIDE + (CMP_BLOCK - 1))
    cmask = (cd >= 0) & (c_idx < n_cmp - 1)
    cdf = cd.astype(F32)
    p_sum = jnp.zeros((tq, n_cmp), F32)
    for h in range(N_HEADS):
        p = _softmax2_rows(a_all[h * tq:(h + 1) * tq] - (SLOPES_NSA[h] * LOG2E) * cdf, cmask)
        p_sum = p_sum + p
        p4_ref[h * tq:(h + 1) * tq, 0:n_cmp] = p.astype(BF16)
    o_cmp = _dot(p4_ref[:, 0:n_cmp], vc_ref[0])

    wlen = min(NSA_WINDOW + tq, s_len)
    ks = pl.multiple_of(jnp.maximum(t0 + tq - wlen, 0), LANES)
    kw = jnp.concatenate([wk_ref[0, pl.ds(ks, wlen), :], pf_ref[pl.ds(ks, wlen), :]], axis=1)
    vw = jnp.concatenate([wv_ref[0, pl.ds(ks, wlen), :], vf_ref[0:wlen, :]], axis=1)
    wd = t_col - (ks + lax.broadcasted_iota(I32, (1, wlen), 1))
    wmask = (wd >= 0) & (wd < NSA_WINDOW)
    for h in range(N_HEADS):
        rows = slice(h * tq, (h + 1) * tq)
        ah = jnp.where(wmask, _dot_nt(q4_ref[rows, :], kw), NEG)
        m = jnp.max(ah, axis=-1, keepdims=True)
        ow_ref[rows, :] = _dot(jnp.exp2(ah - m).astype(BF16), vw)

    n_selp = selm_ref.shape[0]
    jj = lax.broadcasted_iota(I32, (n_selp, n_cmp), 0)
    cc = lax.broadcasted_iota(I32, (n_selp, n_cmp), 1)
    c_start = cc * CMP_STRIDE
    overlap = ((c_start < (jj + 1) * SEL_BLOCK) & (c_start + (CMP_BLOCK - 1) >= jj * SEL_BLOCK)
               & (cc < n_cmp - 1))
    ov = jnp.where(overlap, 1.0, 0.0).astype(BF16)
    hi, mid, lo = _split3(p_sum)
    imp = _dot_nt(ov, hi) + _dot_nt(ov, mid) + _dot_nt(ov, lo)
    j_col = lax.broadcasted_iota(I32, (n_selp, 1), 0)
    t_blk = (t0 + lax.broadcasted_iota(I32, (1, tq), 1)) // SEL_BLOCK
    forced = (j_col == 0) | (j_col == t_blk) | (j_col == t_blk - 1)
    imp = jnp.where(forced, jnp.inf, imp)
    imp = jnp.where(j_col <= t_blk, imp, -jnp.inf)
    rank = jnp.zeros((n_selp, tq), F32)
    for i2 in range(n_sel):
        ci = imp[i2:i2 + 1, :]
        tie_first = jnp.where(j_col > i2, 1.0, 0.0)
        rank = rank + jnp.where(ci > imp, 1.0, jnp.where(ci == imp, tie_first, 0.0))
    selm_ref[...] = jnp.where((rank < n_top) & (j_col < n_sel), 0.0, NEG)
    sel_bias = selm_ref[...].T[:, 0:LANES - BLOCK_COL].astype(BF16)
    for h in range(N_HEADS):
        q4_ref[h * tq:(h + 1) * tq, HEAD_DIM + BLOCK_COL:2 * HEAD_DIM] = sel_bias

    tk = min(ATT_TK, s_len)
    n_kt = (t0 + tq - 1) // tk + 1

    def sel_keys(kt):
        s0 = pl.multiple_of(kt * tk, tk)
        return jnp.concatenate([sk_ref[0, pl.ds(s0, tk), :], pf_ref[pl.ds(s0, tk), :]], axis=1)

    def sel_vals(kt):
        s0 = pl.multiple_of(kt * tk, tk)
        return jnp.concatenate([sv_ref[0, pl.ds(s0, tk), :], vf_ref[0:tk, :]], axis=1)

    def causal_last():
        return (n_kt - 1) * tk + lax.broadcasted_iota(I32, (1, tk), 1) <= t_col

    _flash_attention(q4_ref, acc_ref, p_ref, a_ref, n_kt, sel_keys, sel_vals, tq, last_mask_fn=causal_last)

    gates = jax.nn.sigmoid(sm_ref[0].astype(F32))
    for h in range(N_HEADS):
        rows = slice(h * tq, (h + 1) * tq)
        o_slc = acc_ref[rows, 0:HEAD_DIM] / jnp.maximum(acc_ref[rows, HEAD_DIM:HEAD_DIM + 1], 1e-30)
        o_w = ow_ref[rows, 0:HEAD_DIM] / jnp.maximum(ow_ref[rows, HEAD_DIM:HEAD_DIM + 1], 1e-30)
        g0 = gates[:, NG_LANE + 3 * h:NG_LANE + 3 * h + 1]
        g1 = gates[:, NG_LANE + 3 * h + 1:NG_LANE + 3 * h + 2]
        g2 = gates[:, NG_LANE + 3 * h + 2:NG_LANE + 3 * h + 3]
        o = g0 * o_cmp[rows] + g1 * o_slc + g2 * o_w
        o_ref[0, :, h * HEAD_DIM:(h + 1) * HEAD_DIM] = o.astype(BF16)


def _nsa_mixer(proj, k_cmp, v_cmp):
    b, s, _ = proj.shape
    tq = min(ATT_TQ, s)
    tk = min(ATT_TK, s)
    n_cmp = k_cmp.shape[1]
    wlen = min(NSA_WINDOW + tq, s)
    col = lambda name: _PACK_OFF[name] // HEAD_DIM
    full = lambda name: pl.BlockSpec((1, s, HEAD_DIM), functools.partial(lambda c, bi, i: (bi, 0, c), col(name)))
    nv = max(tk, wlen)
    return pl.pallas_call(
        _nsa_kernel,
        out_shape=jax.ShapeDtypeStruct((b, s, W_MIX), BF16),
        grid=(b, s // tq),
        in_specs=[
            pl.BlockSpec((1, tq, W_MIX), lambda bi, i: (bi, i, _PACK_OFF['n_q'] // W_MIX)),
            pl.BlockSpec((1, tq, LANES), lambda bi, i: (bi, i, SMALL_OFF // LANES)),
            pl.BlockSpec((1, n_cmp, HEAD_DIM), lambda bi, i: (bi, 0, 0)),
            pl.BlockSpec((1, n_cmp, HEAD_DIM), lambda bi, i: (bi, 0, 0)),
            full('n_sk'), full('n_sv'), full('n_wk'), full('n_wv'),
            _resident((s, LANES)), _resident((N_HEADS * tq, LANES)), _resident((nv, LANES)),
        ],
        out_specs=pl.BlockSpec((1, tq, W_MIX), lambda bi, i: (bi, i, 0)),
        scratch_shapes=[pltpu.VMEM((N_HEADS * tq, 2 * HEAD_DIM), BF16),
                        pltpu.VMEM((N_HEADS * tq, n_cmp), BF16),
                        pltpu.VMEM((N_HEADS * tq, 2 * HEAD_DIM), F32),
                        pltpu.VMEM((((s // SEL_BLOCK + LANES - 1) // LANES) * LANES, tq), F32),
                        pltpu.VMEM((N_HEADS * tq, tk), BF16),
                        pltpu.VMEM((N_HEADS * tq, tk), F32),
                        pltpu.VMEM((N_HEADS * tq, 2 * HEAD_DIM), F32)],
        compiler_params=_cparams(("parallel", "arbitrary")),
        name="nsa_attention",
    )(proj, proj, k_cmp, v_cmp, proj, proj, proj, proj,
      _pos_features(s, block_onehot=True), _slope_features(SLOPES_NSA, tq), _ones_feature(nv))


I16 = jnp.int16
I16_MIN = -2 ** 15
SEL_ROWS = 64
SCORE_CHAINS = 4


def _dsa_kernel(q_ref, iq_ref, sm_ref, smf_ref, k_ref, v_ref, pf_ref, qf_ref, vf_ref, o_ref,
                q4_ref, acc_ref, hi_ref, lo_ref, lq_ref, selb_ref, p_ref, a_ref):
    tq = q_ref.shape[1]
    s_len = k_ref.shape[1]
    topk = min(DSA_TOPK_MAX, s_len // 4)
    tk = min(ATT_TK, s_len)
    n_ch = tk // SEL_ROWS
    t0 = pl.program_id(1) * tq
    n_kt = (t0 + tq - 1) // tk + 1
    _stack_queries(q_ref, qf_ref, q4_ref)
    t_row = t0 + lax.broadcasted_iota(I32, (1, tq), 1)
    iq = iq_ref[0]
    iw_t = sm_ref[0].astype(F32).T
    one = jnp.ones((), BF16)
    zero = jnp.zeros((), BF16)

    def score_body(kt, _):
        tc = tk // SCORE_CHAINS
        for c in range(SCORE_CHAINS):
            s0 = pl.multiple_of(kt * tk + c * tc, tc)
            ik = smf_ref[0, pl.ds(s0, tc), IK_LANE:IK_LANE + IDX_DIM]
            sc = jnp.zeros((tc, tq), F32)
            for h in range(IDX_HEADS):
                lg = _dot_nt(ik, iq[:, h * IDX_DIM:(h + 1) * IDX_DIM])
                sc = sc + jnp.maximum(lg, 0.0) * iw_t[IW_LANE + h:IW_LANE + h + 1, :]
            sc = jnp.where(sc == 0.0, 0.0, sc)
            bits = lax.bitcast_convert_type(sc, I32)
            key = bits ^ ((bits >> 31) & 0x7FFFFFFF)
            s_pos = s0 + lax.broadcasted_iota(I32, (tc, 1), 0)
            key = jnp.where(s_pos <= t_row, key, INT_MIN)
            hi_ref[pl.ds(s0, tc), :] = (key >> 16).astype(I16)
            lo_ref[pl.ds(s0, tc), :] = ((key & 0xFFFF) + I16_MIN).astype(I16)
        return 0

    lax.fori_loop(0, n_kt, score_body, 0)

    def count_ge(ref, thr_row):
        thr = jnp.broadcast_to(thr_row, (SEL_ROWS, tq))

        def body(kt, cnt):
            s0 = pl.multiple_of(kt * tk, tk)
            for c in range(n_ch):
                cnt = cnt + jnp.where(ref[pl.ds(s0 + c * SEL_ROWS, SEL_ROWS), :] >= thr, one, zero)
            return cnt
        cnt = lax.fori_loop(0, n_kt, body, jnp.zeros((SEL_ROWS, tq), BF16))
        return jnp.sum(cnt.astype(F32), axis=0, keepdims=True)

    def kth_largest(ref, k):
        def bit_body(it, carry):
            cand, cnt = carry
            trial = cand | jnp.left_shift(jnp.int32(1), 15 - it)
            total = count_ge(ref, (trial + I16_MIN).astype(I16))
            ok = total >= k
            return jnp.where(ok, trial, cand), jnp.where(ok, total, cnt)
        return lax.fori_loop(0, 16, bit_body, (jnp.zeros((1, tq), I32), jnp.zeros((1, tq), F32)))

    raw_hi, n_ge_hi = kth_largest(hi_ref, float(topk))
    cand_hi = jnp.maximum(raw_hi, 1)
    p16 = jnp.broadcast_to((cand_hi + I16_MIN).astype(I16), (SEL_ROWS, tq))
    n_above = count_ge(hi_ref, (jnp.minimum(cand_hi + 1, 2 ** 16 - 1) + I16_MIN).astype(I16))
    need = float(topk) - n_above

    def tie_body(kt, _):
        s0 = pl.multiple_of(kt * tk, tk)
        for c in range(n_ch):
            ds = pl.ds(s0 + c * SEL_ROWS, SEL_ROWS)
            lq_ref[ds, :] = jnp.where(hi_ref[ds, :] == p16, lo_ref[ds, :], jnp.full((), I16_MIN, I16))
        return 0

    lax.fori_loop(0, n_kt, tie_body, 0)
    cand_lo, n_ge_lo = kth_largest(lq_ref, need)
    q16_row = (cand_lo + I16_MIN).astype(I16)
    q16 = jnp.broadcast_to(q16_row, (SEL_ROWS, tq))

    def sel_body(kt, _):
        s0 = pl.multiple_of(kt * tk, tk)
        for c in range(n_ch):
            ds = pl.ds(s0 + c * SEL_ROWS, SEL_ROWS)
            hi = hi_ref[ds, :]
            tie = jnp.where(hi == p16, jnp.where(lo_ref[ds, :] >= q16, one, zero), zero)
            selb_ref[ds, :] = jnp.where(hi > p16, one, tie)
        return 0

    lax.fori_loop(0, n_kt, sel_body, 0)

    n_gt = jnp.where(cand_lo >= 2 ** 16 - 1, 0.0,
                     count_ge(lq_ref, (jnp.minimum(cand_lo + 1, 2 ** 16 - 1) + I16_MIN).astype(I16)))
    n_with_hi = jnp.where(raw_hi > 0, n_ge_hi - n_above, 0.0)
    n_ge = jnp.where(cand_lo > 0, n_ge_lo, n_with_hi)
    quota = need - n_gt
    any_tied = jnp.max(jnp.where(n_ge - n_gt > quota, 1.0, 0.0)) > 0.0

    @pl.when(any_tied)
    def _():
        p_t = jnp.broadcast_to(p16[0:1, :], (tk, tq))
        q_t = jnp.broadcast_to(q16_row, (tk, tq))
        lower = jnp.where(lax.broadcasted_iota(I32, (tk, tk), 0) >= lax.broadcasted_iota(I32, (tk, tk), 1),
                          1.0, 0.0).astype(BF16)

        def exact_ties(kt):
            ds = pl.ds(pl.multiple_of(kt * tk, tk), tk)
            exact = jnp.where(hi_ref[ds, :] == p_t, jnp.where(lo_ref[ds, :] == q_t, one, zero), zero)
            return ds, exact, _dot(lower, exact)

        def drop(ds, exact, rank):
            selb_ref[ds, :] = selb_ref[ds, :] - jnp.where(rank > quota, 1.0, 0.0).astype(BF16) * exact

        def drop_pair(j, seen):
            ds_a, exact_a, in_a = exact_ties(2 * j)
            ds_b, exact_b, in_b = exact_ties(2 * j + 1)
            rank_a = in_a + seen
            rank_b = in_b + rank_a[tk - 1:tk, :]
            drop(ds_a, exact_a, rank_a)
            drop(ds_b, exact_b, rank_b)
            return rank_b[tk - 1:tk, :]

        seen = lax.fori_loop(0, n_kt // 2, drop_pair, jnp.zeros((1, tq), F32))

        @pl.when(n_kt % 2 == 1)
        def _():
            ds, exact, in_tile = exact_ties(n_kt - 1)
            drop(ds, exact, in_tile + seen)

    def att_keys(kt):
        s0 = pl.multiple_of(kt * tk, tk)
        return jnp.concatenate([k_ref[0, pl.ds(s0, tk), :], pf_ref[pl.ds(s0, tk), :]], axis=1)

    def att_vals(kt):
        s0 = pl.multiple_of(kt * tk, tk)
        return jnp.concatenate([v_ref[0, pl.ds(s0, tk), :], vf_ref[...]], axis=1)

    def att_mask(kt):
        s0 = pl.multiple_of(kt * tk, tk)
        return selb_ref[pl.ds(s0, tk), :].astype(F32).T > 0.5

    _flash_attention(q4_ref, acc_ref, p_ref, a_ref, n_kt, att_keys, att_vals, tq, mask_fn=att_mask)
    for h in range(N_HEADS):
        rows = slice(h * tq, (h + 1) * tq)
        o = acc_ref[rows, 0:HEAD_DIM] / jnp.maximum(acc_ref[rows, HEAD_DIM:HEAD_DIM + 1], 1e-30)
        o_ref[0, :, h * HEAD_DIM:(h + 1) * HEAD_DIM] = o.astype(BF16)


def _dsa_mixer(proj):
    b, s, _ = proj.shape
    tq = min(ATT_TQ, s)
    tk = min(ATT_TK, s)
    full = lambda off: pl.BlockSpec((1, s, LANES), functools.partial(lambda c, bi, i: (bi, 0, c), off // LANES))
    return pl.pallas_call(
        _dsa_kernel,
        out_shape=jax.ShapeDtypeStruct((b, s, W_MIX), BF16),
        grid=(b, s // tq),
        in_specs=[
            pl.BlockSpec((1, tq, W_MIX), lambda bi, i: (bi, i, _PACK_OFF['d_q'] // W_MIX)),
            pl.BlockSpec((1, tq, W_MIX), lambda bi, i: (bi, i, _PACK_OFF['i_q'] // W_MIX)),
            pl.BlockSpec((1, tq, LANES), lambda bi, i: (bi, i, SMALL_OFF // LANES)),
            full(SMALL_OFF), full(_PACK_OFF['d_k']), full(_PACK_OFF['d_v']),
            _resident((s, LANES)), _resident((N_HEADS * tq, LANES)), _resident((tk, LANES)),
        ],
        out_specs=pl.BlockSpec((1, tq, W_MIX), lambda bi, i: (bi, i, 0)),
        scratch_shapes=[pltpu.VMEM((N_HEADS * tq, 2 * HEAD_DIM), BF16),
                        pltpu.VMEM((N_HEADS * tq, 2 * HEAD_DIM), F32),
                        pltpu.VMEM((s, tq), I16), pltpu.VMEM((s, tq), I16), pltpu.VMEM((s, tq), I16),
                        pltpu.VMEM((s, tq), BF16),
                        pltpu.VMEM((N_HEADS * tq, tk), BF16),
                        pltpu.VMEM((N_HEADS * tq, tk), F32)],
        compiler_params=_cparams(("parallel", "arbitrary")),
        name="dsa_attention",
    )(proj, proj, proj, proj, proj, proj,
      _pos_features(s), _slope_features(SLOPES_DSA, tq), _ones_feature(tk))


OUTPROJ_TM = 512
LN_ROWS = 256


def _outproj_kernel(y0_ref, y1_ref, y2_ref, y3_ref, w_ref, x_ref, g_ref, b_ref, o_ref, ob_ref):
    for r0 in range(0, x_ref.shape[0], LN_ROWS):
        rows = slice(r0, r0 + LN_ROWS)
        acc = DN_ALPHA * x_ref[rows, :]
        for gi, y_ref in enumerate((y0_ref, y1_ref, y2_ref, y3_ref)):
            acc = acc + _dot(y_ref[rows, :], w_ref[gi * W_MIX:(gi + 1) * W_MIX, :])
        y = _layernorm(acc, g_ref[...], b_ref[...])
        o_ref[rows, :] = y
        ob_ref[rows, :] = y.astype(BF16)


def _outproj_ln(ys, w_out, x, g, b, l):
    n_tok, d = x.shape
    tm = OUTPROJ_TM
    yspec = pl.BlockSpec((tm, W_MIX), lambda i: (i, 0))
    xspec = pl.BlockSpec((tm, d), lambda i: (i, 0))
    return pl.pallas_call(
        _outproj_kernel,
        out_shape=(jax.ShapeDtypeStruct((n_tok, d), F32), jax.ShapeDtypeStruct((n_tok, d), BF16)),
        grid=(n_tok // tm,),
        in_specs=[yspec, yspec, yspec, yspec, _of_layer(w_out, l), xspec, _of_layer(g, l), _of_layer(b, l)],
        out_specs=(xspec, xspec),
        compiler_params=_cparams(("parallel",)),
        name="outproj_ln",
    )(*ys, w_out, x, g, b)


FFN_UP_TM = 1024
FFN_TF = 512
FFN_DOWN_TM = 256


def _ffn_up_kernel(xb_ref, wg_ref, wu_ref, h_ref, wgb_ref, wub_ref):
    @pl.when(pl.program_id(1) == 0)
    def _():
        wgb_ref[...] = wg_ref[...].astype(BF16)
        wub_ref[...] = wu_ref[...].astype(BF16)

    xb = xb_ref[...]
    gate = _dot(xb, wgb_ref[...])
    up = _dot(xb, wub_ref[...])
    h_ref[...] = (gate * jax.nn.sigmoid(gate) * up).astype(BF16)


def _ffn_down_kernel(h_ref, wd_ref, x_ref, g_ref, b_ref, *rest):
    *out_refs, acc_a, acc_b = rest
    i = pl.program_id(0)

    @pl.when(i == 0)
    def _():
        acc_b[...] = jnp.zeros(acc_b.shape, F32)

    def step(acc_prev, acc_next):
        y = _layernorm(acc_prev[...], g_ref[...], b_ref[...])
        out_refs[0][...] = y
        if len(out_refs) > 1:
            out_refs[1][...] = y.astype(BF16)
        acc_next[...] = DN_ALPHA * x_ref[...] + _dot(h_ref[...], wd_ref[...])

    @pl.when(i % 2 == 0)
    def _():
        step(acc_b, acc_a)

    @pl.when(i % 2 == 1)
    def _():
        step(acc_a, acc_b)


def _ffn_ln(xb, w_gate_up, w_down, x, g, b, l, want_bf16):
    n_tok, d = x.shape
    d_ff = w_down.shape[1]
    tm, tf = FFN_UP_TM, FFN_TF
    nf = d_ff // tf
    h = pl.pallas_call(
        _ffn_up_kernel,
        out_shape=jax.ShapeDtypeStruct((n_tok, d_ff), BF16),
        grid=(nf, n_tok // tm),
        in_specs=[
            pl.BlockSpec((tm, d), lambda f, i: (i, 0)),
            pl.BlockSpec((None, d, tf), lambda f, i: (l, 0, f)),
            pl.BlockSpec((None, d, tf), lambda f, i: (l, 0, f + nf)),
        ],
        out_specs=pl.BlockSpec((tm, tf), lambda f, i: (i, f)),
        scratch_shapes=[pltpu.VMEM((d, tf), BF16), pltpu.VMEM((d, tf), BF16)],
        compiler_params=_cparams(("parallel", "arbitrary")),
        name="ffn_up",
    )(xb, w_gate_up, w_gate_up)
    tm = FFN_DOWN_TM
    n_tiles = n_tok // tm
    in_row = lambda i: (jnp.minimum(i, n_tiles - 1), 0)
    out_row = lambda i: (jnp.maximum(i - 1, 0), 0)
    out_dtypes = (F32, BF16) if want_bf16 else (F32,)
    outs = pl.pallas_call(
        _ffn_down_kernel,
        out_shape=tuple(jax.ShapeDtypeStruct((n_tok, d), t) for t in out_dtypes),
        grid=(n_tiles + 1,),
        in_specs=[pl.BlockSpec((tm, d_ff), in_row), _of_layer(w_down, l),
                  pl.BlockSpec((tm, d), in_row), _of_layer(g, l), _of_layer(b, l)],
        out_specs=tuple(pl.BlockSpec((tm, d), out_row) for _ in out_dtypes),
        scratch_shapes=[pltpu.VMEM((tm, d), F32), pltpu.VMEM((tm, d), F32)],
        compiler_params=_cparams(("arbitrary",)),
        name="ffn_down_ln",
    )(h, w_down, x, g, b)
    return outs if want_bf16 else (outs[0], None)


PACK_ROWS = 256
_FOLD = {'d_q': ATTN_SCALE * LOG2E, 'n_q': ATTN_SCALE * LOG2E, 'i_w': IDX_SCALE}


def _pack_kernel(w_ref, o_ref):
    small = [n for n in _PACK_ORDER if _SRC_SIZE[n] % LANES]
    for l in range(w_ref.shape[1]):
        for n in _PACK_ORDER:
            if n not in small:
                src, dst, size = _SRC_OFF[n], _PACK_OFF[n], _SRC_SIZE[n]
                seg = w_ref[src:src + size, l, :] * _FOLD.get(n, 1.0)
                o_ref[l, :, dst:dst + size] = seg.T.astype(BF16)
        parts = [w_ref[_SRC_OFF[n]:_SRC_OFF[n] + _SRC_SIZE[n], l, :] * _FOLD.get(n, 1.0) for n in small]
        parts.append(jnp.zeros((LANES - sum(_SRC_SIZE[n] for n in small), w_ref.shape[2]), F32))
        o_ref[l, :, SMALL_OFF:SMALL_OFF + LANES] = jnp.concatenate(parts, axis=0).T.astype(BF16)


def _pack_w_in(w):
    n_layers, d, d_in = w.shape
    assert D_PACK == SMALL_OFF + LANES and [n for n in _PACK_ORDER if _SRC_SIZE[n] % LANES] == list(_PACK_ORDER[-3:])
    return pl.pallas_call(
        _pack_kernel,
        out_shape=jax.ShapeDtypeStruct((n_layers, d, D_PACK), BF16),
        grid=(d // PACK_ROWS,),
        in_specs=[pl.BlockSpec((d_in, n_layers, PACK_ROWS), lambda i: (0, 0, i))],
        out_specs=pl.BlockSpec((n_layers, PACK_ROWS, D_PACK), lambda i: (0, i, 0)),
        compiler_params=_cparams(("parallel",)),
        name="pack_w_in",
    )(jnp.transpose(w, (2, 0, 1)))


def _prepare(p):
    row = lambda v: v[:, None, :].astype(F32)
    n_layers = p['w_in'].shape[0]
    flat = lambda v: v.reshape(n_layers, 1, -1).astype(BF16)
    return dict(
        w_pack=_pack_w_in(p['w_in']), w_out=p['w_out'].astype(BF16),
        pool_w=p['pool_w'].astype(BF16), pool_scale=row(p['pool_scale']),
        conv_w=jnp.concatenate([p['conv_w'].astype(F32), jnp.zeros((n_layers, 1, W_MIX), F32)], axis=1),
        conv_b=row(p['conv_b']), conv_ln_g=row(p['conv_ln_g']), conv_ln_b=row(p['conv_ln_b']),
        conv_pw_w=p['conv_pw_w'].astype(BF16), conv_pw_b=row(p['conv_pw_b']),
        cmp_pos_k=flat(p['cmp_pos_k']), cmp_pos_v=flat(p['cmp_pos_v']),
        cmp_k_w1=p['cmp_k_w1'].astype(BF16), cmp_k_w2=p['cmp_k_w2'].astype(BF16),
        cmp_v_w1=p['cmp_v_w1'].astype(BF16), cmp_v_w2=p['cmp_v_w2'].astype(BF16),
        ln1_g=row(p['ln1_g']), ln1_b=row(p['ln1_b']), ln2_g=row(p['ln2_g']), ln2_b=row(p['ln2_b']),
        w_gate_up=p['w_gate_up'], w_down=p['w_down'].astype(BF16))


def _layer(x, xb, p, l, last):
    b, s, d = x.shape
    n_tok = b * s
    x_in = x if xb is None else xb
    proj, ck, cv = _inproj(x_in.reshape(n_tok, d), p['w_pack'], l)
    proj = proj.reshape(b, s, D_PACK)
    y_pool = _pool_mixer(proj, p['pool_w'], p['pool_scale'], l)
    y_conv = _conv_mixer(proj, p['conv_w'], p['conv_b'], p['conv_ln_g'], p['conv_ln_b'],
                         p['conv_pw_w'], p['conv_pw_b'], l)
    y_dsa = _dsa_mixer(proj)
    k_cmp = _compress(ck.reshape(b, s, HEAD_DIM), p['cmp_pos_k'], p['cmp_k_w1'], p['cmp_k_w2'], l)
    v_cmp = _compress(cv.reshape(b, s, HEAD_DIM), p['cmp_pos_v'], p['cmp_v_w1'], p['cmp_v_w2'], l)
    y_nsa = _nsa_mixer(proj, k_cmp, v_cmp)
    ys = [y.reshape(n_tok, W_MIX) for y in (y_pool, y_conv, y_dsa, y_nsa)]
    x1, x1b = _outproj_ln(ys, p['w_out'], x.reshape(n_tok, d), p['ln1_g'], p['ln1_b'], l)
    x2, x2b = _ffn_ln(x1b, p['w_gate_up'], p['w_down'], x1, p['ln2_g'], p['ln2_b'], l,
                      want_bf16=not last)
    return x2.reshape(b, s, d), None if last else x2b.reshape(b, s, d)


def kernel(x, w_in, w_out, pool_w, pool_scale, conv_w, conv_b, conv_ln_g, conv_ln_b, conv_pw_w, conv_pw_b,
           cmp_pos_k, cmp_pos_v, cmp_k_w1, cmp_k_w2, cmp_v_w1, cmp_v_w2, ln1_g, ln1_b, ln2_g, ln2_b,
           w_gate_up, w_down):
    params = dict(w_in=w_in, w_out=w_out, pool_w=pool_w, pool_scale=pool_scale, conv_w=conv_w, conv_b=conv_b,
                  conv_ln_g=conv_ln_g, conv_ln_b=conv_ln_b, conv_pw_w=conv_pw_w, conv_pw_b=conv_pw_b,
                  cmp_pos_k=cmp_pos_k, cmp_pos_v=cmp_pos_v, cmp_k_w1=cmp_k_w1, cmp_k_w2=cmp_k_w2,
                  cmp_v_w1=cmp_v_w1, cmp_v_w2=cmp_v_w2, ln1_g=ln1_g, ln1_b=ln1_b, ln2_g=ln2_g, ln2_b=ln2_b,
                  w_gate_up=w_gate_up, w_down=w_down)
    prepared = _prepare(params)
    xb = None
    for l in range(w_in.shape[0]):
        x, xb = _layer(x, xb, prepared, l, last=l == w_in.shape[0] - 1)
    return x
```

```python
import functools

import numpy as np
import jax
import jax.numpy as jnp
from jax import lax
from jax.experimental import pallas as pl
from jax.experimental.pallas import tpu as pltpu

F32 = jnp.float32
BF16 = jnp.bfloat16
I32 = jnp.int32

D_MODEL = 2048
DEPTH = 2
W_MIX = D_MODEL // 4
HEAD_DIM = 128
N_HEADS = W_MIX // HEAD_DIM
POOL_WINDOWS = (2, 4, 8, 16)
POOL_GROUP = W_MIX // len(POOL_WINDOWS)
CONV_WIDTH = 31
IDX_HEADS = 8
IDX_DIM = 64
DSA_TOPK_MAX = 256
CMP_BLOCK = 32
CMP_STRIDE = 16
SEL_BLOCK = 64
SEL_TOPK = 16
NSA_WINDOW = 512
DN_ALPHA = (2 * DEPTH) ** 0.25
ATTN_SCALE = HEAD_DIM ** -0.5
IDX_SCALE = (IDX_HEADS * IDX_DIM) ** -0.5
LN_EPS = 1e-5
LOG2E = 1.4426950408889634

LANES = 128
SUBLANES = 8
VMEM_LIMIT = 48 * 1024 * 1024

NEG = -1e30
INT_MIN = -2 ** 31

_SRC_SIZES = (W_MIX, W_MIX, W_MIX, W_MIX, HEAD_DIM, HEAD_DIM, IDX_HEADS * IDX_DIM, IDX_DIM, IDX_HEADS,
              W_MIX, HEAD_DIM, HEAD_DIM, HEAD_DIM, HEAD_DIM, HEAD_DIM, HEAD_DIM, N_HEADS * 3)
_SRC_NAMES = ('pool', 'c_a', 'c_g', 'd_q', 'd_k', 'd_v', 'i_q', 'i_k', 'i_w',
              'n_q', 'n_ck', 'n_cv', 'n_sk', 'n_sv', 'n_wk', 'n_wv', 'n_g')
_SRC_OFF = dict(zip(_SRC_NAMES, np.concatenate([[0], np.cumsum(_SRC_SIZES)[:-1]]).tolist()))
_SRC_SIZE = dict(zip(_SRC_NAMES, _SRC_SIZES))
_PACK_ORDER = ('pool', 'c_a', 'c_g', 'd_q', 'i_q', 'n_q', 'd_k', 'd_v',
               'n_ck', 'n_cv', 'n_sk', 'n_sv', 'n_wk', 'n_wv', 'i_k', 'i_w', 'n_g')
_PACK_OFF = {}
_o = 0
for _n in _PACK_ORDER:
    _PACK_OFF[_n] = _o
    _o += _SRC_SIZE[_n]
D_PACK = ((_o + LANES - 1) // LANES) * LANES
SMALL_OFF = _PACK_OFF['i_k']
IK_LANE = 0
IW_LANE = _PACK_OFF['i_w'] - SMALL_OFF
NG_LANE = _PACK_OFF['n_g'] - SMALL_OFF


def _alibi_slopes():
    n = 2 * N_HEADS
    s = np.power(2.0, -8.0 * np.arange(1, n + 1) / n).astype(np.float32)
    return [float(v) for v in s[0::2]], [float(v) for v in s[1::2]]


SLOPES_DSA, SLOPES_NSA = _alibi_slopes()


def _cparams(sem):
    return pltpu.CompilerParams(dimension_semantics=sem, vmem_limit_bytes=VMEM_LIMIT)


def _resident(shape):
    nd = len(shape)
    return pl.BlockSpec(shape, lambda *_: (0,) * nd, pipeline_mode=pl.Buffered(1))


def _of_layer(arr, l):
    nd = arr.ndim
    return pl.BlockSpec((None,) + arr.shape[1:], lambda *_: (l,) + (0,) * (nd - 1),
                        pipeline_mode=pl.Buffered(1))


def _dot(a, b):
    return jnp.dot(a, b, preferred_element_type=F32)


def _dot_nt(a, b):
    return lax.dot_general(a, b, (((1,), (1,)), ((), ())), preferred_element_type=F32)


def _layernorm(x, g, b):
    mu = jnp.mean(x, axis=-1, keepdims=True)
    xc = x - mu
    var = jnp.mean(xc * xc, axis=-1, keepdims=True)
    return xc * lax.rsqrt(var + LN_EPS) * g + b


INPROJ_TM = 512
INPROJ_CHUNK = 512


def _inproj_kernel(x_ref, w_ref, o_ref, ck_ref, cv_ref):
    x = x_ref[...].astype(BF16)
    n = o_ref.shape[1]
    for c0 in range(0, n, INPROJ_CHUNK):
        c1 = min(c0 + INPROJ_CHUNK, n)
        o_ref[:, c0:c1] = _dot(x, w_ref[:, c0:c1]).astype(BF16)
    ck_ref[...] = o_ref[:, _PACK_OFF['n_ck']:_PACK_OFF['n_ck'] + HEAD_DIM]
    cv_ref[...] = o_ref[:, _PACK_OFF['n_cv']:_PACK_OFF['n_cv'] + HEAD_DIM]


def _inproj(x, w_pack, l):
    n_tok, d = x.shape
    tm = INPROJ_TM
    col = pl.BlockSpec((tm, HEAD_DIM), lambda i: (i, 0))
    return pl.pallas_call(
        _inproj_kernel,
        out_shape=(jax.ShapeDtypeStruct((n_tok, D_PACK), BF16),
                   jax.ShapeDtypeStruct((n_tok, HEAD_DIM), BF16), jax.ShapeDtypeStruct((n_tok, HEAD_DIM), BF16)),
        grid=(n_tok // tm,),
        in_specs=[pl.BlockSpec((tm, d), lambda i: (i, 0)), _of_layer(w_pack, l)],
        out_specs=(pl.BlockSpec((tm, D_PACK), lambda i: (i, 0)), col, col),
        compiler_params=_cparams(("parallel",)),
        name="inproj",
    )(x, w_pack)


POOL_TS = 512
POOL_HALO = 16


def _pool_kernel(u_ref, halo_ref, w_ref, sc_ref, o_ref, xs_ref):
    i = pl.program_id(1)
    ts = u_ref.shape[1]
    xs_ref[POOL_HALO:POOL_HALO + ts, :] = u_ref[0].astype(F32)
    xs_ref[0:POOL_HALO, :] = jnp.where(i > 0, halo_ref[0].astype(F32), 0.0)
    pos = i * ts + lax.broadcasted_iota(I32, (ts, 1), 0)
    for g, win in enumerate(POOL_WINDOWS):
        c = slice(g * POOL_GROUP, (g + 1) * POOL_GROUP)
        x = xs_ref[POOL_HALO:POOL_HALO + ts, c]
        acc = x
        for k in range(1, win):
            acc = acc + xs_ref[POOL_HALO - k:POOL_HALO - k + ts, c]
        cnt = jnp.minimum(pos + 1, win).astype(F32)
        d = acc / cnt - x
        y = _dot(d.astype(BF16), w_ref[g])
        o_ref[0, :, c] = (y * sc_ref[:, c]).astype(BF16)


def _pool_mixer(proj, pool_w, pool_scale, l):
    b, s, _ = proj.shape
    ts = min(POOL_TS, s)
    hb = ts // POOL_HALO
    blk = _PACK_OFF['pool'] // W_MIX
    return pl.pallas_call(
        _pool_kernel,
        out_shape=jax.ShapeDtypeStruct((b, s, W_MIX), BF16),
        grid=(b, s // ts),
        in_specs=[
            pl.BlockSpec((1, ts, W_MIX), lambda bi, i: (bi, i, blk)),
            pl.BlockSpec((1, POOL_HALO, W_MIX), lambda bi, i: (bi, jnp.maximum(i * hb - 1, 0), blk)),
            _of_layer(pool_w, l),
            _of_layer(pool_scale, l),
        ],
        out_specs=pl.BlockSpec((1, ts, W_MIX), lambda bi, i: (bi, i, 0)),
        scratch_shapes=[pltpu.VMEM((ts + POOL_HALO, W_MIX), F32)],
        compiler_params=_cparams(("parallel", "parallel")),
        name="pool_mixer",
    )(proj, proj, pool_w, pool_scale)


CONV_TS = 512
CONV_HALO = 32
CONV_ROWS = 32


def _conv_kernel(a_ref, g_ref, ha_ref, hg_ref, cw_ref, cb_ref, lg_ref, lb_ref, pw_ref, pb_ref,
                 o_ref, hs_ref, sh_ref, y_ref):
    i = pl.program_id(1)
    ts = a_ref.shape[1]
    hs_ref[CONV_HALO:CONV_HALO + ts, :] = a_ref[0].astype(F32) * jax.nn.sigmoid(g_ref[0].astype(F32))
    halo = ha_ref[0].astype(F32) * jax.nn.sigmoid(hg_ref[0].astype(F32))
    hs_ref[0:CONV_HALO, :] = jnp.where(i > 0, halo, 0.0)
    n_sh = ts + CONV_HALO - SUBLANES
    for b in range(1, SUBLANES):
        sh_ref[b - 1, 0:n_sh, :] = hs_ref[b:b + n_sh, :]
    base = CONV_HALO - (CONV_WIDTH - 1)
    for r0 in range(0, ts, CONV_ROWS):
        acc = jnp.broadcast_to(cb_ref[...], (CONV_ROWS, W_MIX))
        for j in range(CONV_WIDTH):
            a8, b = divmod(base + j, SUBLANES)
            r = r0 + a8 * SUBLANES
            src = hs_ref[r:r + CONV_ROWS, :] if b == 0 else sh_ref[b - 1, r:r + CONV_ROWS, :]
            acc = acc + src * cw_ref[j:j + 1, :]
        y = _layernorm(acc, lg_ref[...], lb_ref[...])
        y_ref[r0:r0 + CONV_ROWS, :] = (y * jax.nn.sigmoid(y)).astype(BF16)
    o_ref[0] = (_dot(y_ref[...], pw_ref[...]) + pb_ref[...]).astype(BF16)


def _conv_mixer(proj, cw, conv_b, ln_g, ln_b, pw_w, pw_b, l):
    b, s, _ = proj.shape
    ts = min(CONV_TS, s)
    hb = ts // CONV_HALO
    ba = _PACK_OFF['c_a'] // W_MIX
    bg = _PACK_OFF['c_g'] // W_MIX
    halo_map = lambda blk: (lambda bi, i: (bi, jnp.maximum(i * hb - 1, 0), blk))
    return pl.pallas_call(
        _conv_kernel,
        out_shape=jax.ShapeDtypeStruct((b, s, W_MIX), BF16),
        grid=(b, s // ts),
        in_specs=[
            pl.BlockSpec((1, ts, W_MIX), lambda bi, i: (bi, i, ba)),
            pl.BlockSpec((1, ts, W_MIX), lambda bi, i: (bi, i, bg)),
            pl.BlockSpec((1, CONV_HALO, W_MIX), halo_map(ba)),
            pl.BlockSpec((1, CONV_HALO, W_MIX), halo_map(bg)),
            _of_layer(cw, l), _of_layer(conv_b, l), _of_layer(ln_g, l), _of_layer(ln_b, l),
            _of_layer(pw_w, l), _of_layer(pw_b, l),
        ],
        out_specs=pl.BlockSpec((1, ts, W_MIX), lambda bi, i: (bi, i, 0)),
        scratch_shapes=[pltpu.VMEM((ts + CONV_HALO, W_MIX), F32),
                        pltpu.VMEM((SUBLANES - 1, ts + CONV_HALO - SUBLANES, W_MIX), F32),
                        pltpu.VMEM((ts, W_MIX), BF16)],
        compiler_params=_cparams(("parallel", "parallel")),
        name="conv_mixer",
    )(proj, proj, proj, proj, cw, conv_b, ln_g, ln_b, pw_w, pw_b)


def _compress_kernel(r_ref, pos_ref, w1_ref, w2_ref, o_ref):
    r = r_ref[0]
    half = r.shape[1]
    n = r.shape[0]
    top = _dot(r, w1_ref[0:half, :])
    bot = _dot(r, w1_ref[half:2 * half, :])
    bot_next = pltpu.roll(bot, n - 1, 0)
    posb = _dot(jnp.broadcast_to(pos_ref[...], (8, 2 * half)), w1_ref[...])[0:1, :]
    h = jax.nn.gelu(top + bot_next + posb)
    out = _dot(h.astype(BF16), w2_ref[...])
    row = lax.broadcasted_iota(I32, out.shape, 0)
    o_ref[0] = jnp.where(row < n - 1, out, 0.0).astype(BF16)


def _compress(raw, pos, w1, w2, l):
    b, s, d = raw.shape
    n = s // CMP_STRIDE
    r = raw.reshape(b, n, CMP_STRIDE * d)
    return pl.pallas_call(
        _compress_kernel,
        out_shape=jax.ShapeDtypeStruct((b, n, d), BF16),
        grid=(b,),
        in_specs=[pl.BlockSpec((1, n, CMP_STRIDE * d), lambda bi: (bi, 0, 0)),
                  _of_layer(pos, l), _of_layer(w1, l), _of_layer(w2, l)],
        out_specs=pl.BlockSpec((1, n, d), lambda bi: (bi, 0, 0)),
        compiler_params=_cparams(("parallel",)),
        name="nsa_compress",
    )(r, pos, w1, w2)


ATT_TQ = 256
ATT_TK = 512
POS_RADIX = 64
BLOCK_COL = 64


def _pos_features(s_len, block_onehot=False):
    assert s_len <= POS_RADIX * 256
    s = np.arange(s_len)
    f = np.zeros((s_len, LANES), np.float32)
    f[:, 0] = f[:, 1] = s // POS_RADIX
    f[:, 2] = f[:, 3] = s % POS_RADIX
    if block_onehot:
        assert s_len // SEL_BLOCK <= LANES - BLOCK_COL
        f[s, BLOCK_COL + s // SEL_BLOCK] = 1.0
    return jnp.asarray(f, BF16)


def _slope_features(slopes, tq):
    f = np.zeros((len(slopes) * tq, LANES), np.float32)
    for h, sl in enumerate(slopes):
        c = np.float32(sl * LOG2E)
        ca = np.float32(np.asarray(c, dtype=BF16))
        cb = np.float32(np.asarray(c - ca, dtype=BF16))
        f[h * tq:(h + 1) * tq, 0:4] = [POS_RADIX * ca, POS_RADIX * cb, ca, cb]
    return jnp.asarray(f, BF16)


def _ones_feature(n):
    f = np.zeros((n, LANES), np.float32)
    f[:, 0] = 1.0
    return jnp.asarray(f, BF16)


def _stack_queries(q_ref, qf_ref, q4_ref):
    tq = q_ref.shape[1]
    for h in range(N_HEADS):
        q4_ref[h * tq:(h + 1) * tq, 0:HEAD_DIM] = q_ref[0, :, h * HEAD_DIM:(h + 1) * HEAD_DIM]
    q4_ref[:, HEAD_DIM:2 * HEAD_DIM] = qf_ref[...]


def _flash_attention(q4_ref, acc_ref, p_ref, a_ref, n_kt, keys_fn, vals_fn, tq, mask_fn=None,
                     last_mask_fn=None):
    acc_ref[...] = jnp.zeros(acc_ref.shape, F32)
    p_ref[...] = jnp.zeros(p_ref.shape, BF16)
    heads = [slice(h * tq, (h + 1) * tq) for h in range(N_HEADS)]

    def tile_logits(kt):
        kk = keys_fn(kt)
        if mask_fn is None:
            return lambda rows: _dot_nt(q4_ref[rows, :], kk)
        mask = mask_fn(kt)
        return lambda rows: jnp.where(mask, _dot_nt(q4_ref[rows, :], kk), NEG)

    def step(kt, ms, last):
        vv = vals_fn(jnp.maximum(kt - 1, 0))
        next_logits = None if last else tile_logits(kt + 1)
        last_mask = last_mask_fn() if (last and last_mask_fn) else None
        new_ms = []
        for h, rows in enumerate(heads):
            ah = a_ref[rows, :]
            if last_mask is not None:
                ah = jnp.where(last_mask, ah, NEG)
            pv = _dot(p_ref[rows, :], vv)
            if not last:
                a_ref[rows, :] = next_logits(rows)
            m_new = jnp.maximum(ms[h], jnp.max(ah, axis=-1, keepdims=True))
            p_ref[rows, :] = jnp.exp2(ah - m_new).astype(BF16)
            acc_ref[rows, :] = jnp.exp2(ms[h] - m_new) * (acc_ref[rows, :] + pv)
            new_ms.append(m_new)
        return tuple(new_ms)

    first_logits = tile_logits(0)
    for rows in heads:
        a_ref[rows, :] = first_logits(rows)
    m0 = tuple(jnp.full((tq, 1), NEG, F32) for _ in range(N_HEADS))
    ms = lax.fori_loop(0, n_kt - 1, lambda kt, ms: step(kt, ms, False), m0)
    step(n_kt - 1, ms, True)
    vv = vals_fn(n_kt - 1)
    for rows in heads:
        acc_ref[rows, :] += _dot(p_ref[rows, :], vv)


def _softmax2_rows(a, mask):
    a = jnp.where(mask, a, NEG)
    m = jnp.max(a, axis=-1, keepdims=True)
    e = jnp.where(mask, jnp.exp2(a - m), 0.0)
    s = jnp.sum(e, axis=-1, keepdims=True)
    return e / jnp.maximum(s, 1e-30)


def _split3(x):
    hi = x.astype(BF16)
    r1 = x - hi.astype(F32)
    mid = r1.astype(BF16)
    lo = (r1 - mid.astype(F32)).astype(BF16)
    return hi, mid, lo


def _nsa_kernel(q_ref, sm_ref, kc_ref, vc_ref, sk_ref, sv_ref, wk_ref, wv_ref, pf_ref, qf_ref, vf_ref,
                o_ref, q4_ref, oc_ref, acc_ref, selm_ref, p_ref, a_ref, ow_ref):
    tq = q_ref.shape[1]
    s_len = sk_ref.shape[1]
    n_cmp = kc_ref.shape[1]
    n_sel = s_len // SEL_BLOCK
    assert n_sel <= LANES - BLOCK_COL
    n_top = min(SEL_TOPK, n_sel)
    t0 = pl.program_id(1) * tq
    _stack_queries(q_ref, qf_ref, q4_ref)
    t_col = t0 + lax.broadcasted_iota(I32, (tq, 1), 0)

    c_idx = lax.broadcasted_iota(I32, (1, n_cmp), 1)
    cd = t_col - (c_idx * CMP_STRIDE + (CMP_BLOCK - 1))
    cmask = (cd >= 0) & (c_idx < n_cmp - 1)
    cdf = cd.astype(F32)
    p_sum = jnp.zeros((tq, n_cmp), F32)
    for h in range(N_HEADS):
        rows = slice(h * tq, (h + 1) * tq)
        a_h = _dot_nt(q4_ref[rows, 0:HEAD_DIM], kc_ref[0])
        p = _softmax2_rows(a_h - (SLOPES_NSA[h] * LOG2E) * cdf, cmask)
        p_sum = p_sum + p
        oc_ref[rows, :] = _dot(p.astype(BF16), vc_ref[0])

    wlen = min(NSA_WINDOW + tq, s_len)
    ks = pl.multiple_of(jnp.maximum(t0 + tq - wlen, 0), LANES)
    kw = jnp.concatenate([wk_ref[0, pl.ds(ks, wlen), :], pf_ref[pl.ds(ks, wlen), :]], axis=1)
    vw = jnp.concatenate([wv_ref[0, pl.ds(ks, wlen), :], vf_ref[0:wlen, :]], axis=1)
    wd = t_col - (ks + lax.broadcasted_iota(I32, (1, wlen), 1))
    wmask = (wd >= 0) & (wd < NSA_WINDOW)
    for h in range(N_HEADS):
        rows = slice(h * tq, (h + 1) * tq)
        ah = jnp.where(wmask, _dot_nt(q4_ref[rows, :], kw), NEG)
        m = jnp.max(ah, axis=-1, keepdims=True)
        ow_ref[rows, :] = _dot(jnp.exp2(ah - m).astype(BF16), vw)

    n_selp = selm_ref.shape[0]
    jj = lax.broadcasted_iota(I32, (n_selp, n_cmp), 0)
    cc = lax.broadcasted_iota(I32, (n_selp, n_cmp), 1)
    c_start = cc * CMP_STRIDE
    overlap = ((c_start < (jj + 1) * SEL_BLOCK) & (c_start + (CMP_BLOCK - 1) >= jj * SEL_BLOCK)
               & (cc < n_cmp - 1))
    ov = jnp.where(overlap, 1.0, 0.0).astype(BF16)
    hi, mid, lo = _split3(p_sum)
    imp = _dot_nt(ov, hi) + _dot_nt(ov, mid) + _dot_nt(ov, lo)
    j_col = lax.broadcasted_iota(I32, (n_selp, 1), 0)
    t_blk = (t0 + lax.broadcasted_iota(I32, (1, tq), 1)) // SEL_BLOCK
    forced = (j_col == 0) | (j_col == t_blk) | (j_col == t_blk - 1)
    imp = jnp.where(forced, jnp.inf, imp)
    imp = jnp.where(j_col <= t_blk, imp, -jnp.inf)
    rank = jnp.zeros((n_selp, tq), F32)
    for i2 in range(n_sel):
        ci = imp[i2:i2 + 1, :]
        tie_first = jnp.where(j_col > i2, 1.0, 0.0)
        rank = rank + jnp.where(ci > imp, 1.0, jnp.where(ci == imp, tie_first, 0.0))
    selm_ref[...] = jnp.where((rank < n_top) & (j_col < n_sel), 0.0, NEG)
    sel_bias = selm_ref[...].T[:, 0:LANES - BLOCK_COL].astype(BF16)
    for h in range(N_HEADS):
        q4_ref[h * tq:(h + 1) * tq, HEAD_DIM + BLOCK_COL:2 * HEAD_DIM] = sel_bias

    tk = min(ATT_TK, s_len)
    n_kt = (t0 + tq - 1) // tk + 1

    def sel_keys(kt):
        s0 = pl.multiple_of(kt * tk, tk)
        return jnp.concatenate([sk_ref[0, pl.ds(s0, tk), :], pf_ref[pl.ds(s0, tk), :]], axis=1)

    def sel_vals(kt):
        s0 = pl.multiple_of(kt * tk, tk)
        return jnp.concatenate([sv_ref[0, pl.ds(s0, tk), :], vf_ref[0:tk, :]], axis=1)

    def causal_last():
        return (n_kt - 1) * tk + lax.broadcasted_iota(I32, (1, tk), 1) <= t_col

    _flash_attention(q4_ref, acc_ref, p_ref, a_ref, n_kt, sel_keys, sel_vals, tq, last_mask_fn=causal_last)

    gates = jax.nn.sigmoid(sm_ref[0].astype(F32))
    for h in range(N_HEADS):
        rows = slice(h * tq, (h + 1) * tq)
        o_slc = acc_ref[rows, 0:HEAD_DIM] / jnp.maximum(acc_ref[rows, HEAD_DIM:HEAD_DIM + 1], 1e-30)
        o_w = ow_ref[rows, 0:HEAD_DIM] / jnp.maximum(ow_ref[rows, HEAD_DIM:HEAD_DIM + 1], 1e-30)
        g0 = gates[:, NG_LANE + 3 * h:NG_LANE + 3 * h + 1]
        g1 = gates[:, NG_LANE + 3 * h + 1:NG_LANE + 3 * h + 2]
        g2 = gates[:, NG_LANE + 3 * h + 2:NG_LANE + 3 * h + 3]
        o = g0 * oc_ref[rows, :] + g1 * o_slc + g2 * o_w
        o_ref[0, :, h * HEAD_DIM:(h + 1) * HEAD_DIM] = o.astype(BF16)


def _nsa_mixer(proj, k_cmp, v_cmp):
    b, s, _ = proj.shape
    tq = min(ATT_TQ, s)
    tk = min(ATT_TK, s)
    n_cmp = k_cmp.shape[1]
    wlen = min(NSA_WINDOW + tq, s)
    col = lambda name: _PACK_OFF[name] // HEAD_DIM
    full = lambda name: pl.BlockSpec((1, s, HEAD_DIM), functools.partial(lambda c, bi, i: (bi, 0, c), col(name)))
    nv = max(tk, wlen)
    return pl.pallas_call(
        _nsa_kernel,
        out_shape=jax.ShapeDtypeStruct((b, s, W_MIX), BF16),
        grid=(b, s // tq),
        in_specs=[
            pl.BlockSpec((1, tq, W_MIX), lambda bi, i: (bi, i, _PACK_OFF['n_q'] // W_MIX)),
            pl.BlockSpec((1, tq, LANES), lambda bi, i: (bi, i, SMALL_OFF // LANES)),
            pl.BlockSpec((1, n_cmp, HEAD_DIM), lambda bi, i: (bi, 0, 0)),
            pl.BlockSpec((1, n_cmp, HEAD_DIM), lambda bi, i: (bi, 0, 0)),
            full('n_sk'), full('n_sv'), full('n_wk'), full('n_wv'),
            _resident((s, LANES)), _resident((N_HEADS * tq, LANES)), _resident((nv, LANES)),
        ],
        out_specs=pl.BlockSpec((1, tq, W_MIX), lambda bi, i: (bi, i, 0)),
        scratch_shapes=[pltpu.VMEM((N_HEADS * tq, 2 * HEAD_DIM), BF16),
                        pltpu.VMEM((N_HEADS * tq, HEAD_DIM), F32),
                        pltpu.VMEM((N_HEADS * tq, 2 * HEAD_DIM), F32),
                        pltpu.VMEM((((s // SEL_BLOCK + LANES - 1) // LANES) * LANES, tq), F32),
                        pltpu.VMEM((N_HEADS * tq, tk), BF16),
                        pltpu.VMEM((N_HEADS * tq, tk), F32),
                        pltpu.VMEM((N_HEADS * tq, 2 * HEAD_DIM), F32)],
        compiler_params=_cparams(("parallel", "arbitrary")),
        name="nsa_attention",
    )(proj, proj, k_cmp, v_cmp, proj, proj, proj, proj,
      _pos_features(s, block_onehot=True), _slope_features(SLOPES_NSA, tq), _ones_feature(nv))


I16 = jnp.int16
I16_MIN = -2 ** 15
SEL_ROWS = 64
SCORE_CHAINS = 4


def _dsa_kernel(q_ref, iq_ref, sm_ref, smf_ref, k_ref, v_ref, pf_ref, qf_ref, vf_ref, o_ref,
                q4_ref, acc_ref, hi_ref, lo_ref, lq_ref, selb_ref, p_ref, a_ref):
    tq = q_ref.shape[1]
    s_len = k_ref.shape[1]
    topk = min(DSA_TOPK_MAX, s_len // 4)
    tk = min(ATT_TK, s_len)
    n_ch = tk // SEL_ROWS
    t0 = pl.program_id(1) * tq
    n_kt = (t0 + tq - 1) // tk + 1
    _stack_queries(q_ref, qf_ref, q4_ref)
    t_row = t0 + lax.broadcasted_iota(I32, (1, tq), 1)
    iq = iq_ref[0]
    iw_t = sm_ref[0].astype(F32).T
    one = jnp.ones((), BF16)
    zero = jnp.zeros((), BF16)

    def score_body(kt, _):
        tc = tk // SCORE_CHAINS
        for c in range(SCORE_CHAINS):
            s0 = pl.multiple_of(kt * tk + c * tc, tc)
            ik = smf_ref[0, pl.ds(s0, tc), IK_LANE:IK_LANE + IDX_DIM]
            sc = jnp.zeros((tc, tq), F32)
            for h in range(IDX_HEADS):
                lg = _dot_nt(ik, iq[:, h * IDX_DIM:(h + 1) * IDX_DIM])
                sc = sc + jnp.maximum(lg, 0.0) * iw_t[IW_LANE + h:IW_LANE + h + 1, :]
            sc = jnp.where(sc == 0.0, 0.0, sc)
            bits = lax.bitcast_convert_type(sc, I32)
            key = bits ^ ((bits >> 31) & 0x7FFFFFFF)
            s_pos = s0 + lax.broadcasted_iota(I32, (tc, 1), 0)
            key = jnp.where(s_pos <= t_row, key, INT_MIN)
            hi_ref[pl.ds(s0, tc), :] = (key >> 16).astype(I16)
            lo_ref[pl.ds(s0, tc), :] = ((key & 0xFFFF) + I16_MIN).astype(I16)
        return 0

    lax.fori_loop(0, n_kt, score_body, 0)

    def count_ge(ref, thr_row):
        thr = jnp.broadcast_to(thr_row, (SEL_ROWS, tq))

        def body(kt, cnt):
            s0 = pl.multiple_of(kt * tk, tk)
            for c in range(n_ch):
                cnt = cnt + jnp.where(ref[pl.ds(s0 + c * SEL_ROWS, SEL_ROWS), :] >= thr, one, zero)
            return cnt
        cnt = lax.fori_loop(0, n_kt, body, jnp.zeros((SEL_ROWS, tq), BF16))
        return jnp.sum(cnt.astype(F32), axis=0, keepdims=True)

    def kth_largest(ref, k):
        def bit_body(it, carry):
            cand, cnt = carry
            trial = cand | jnp.left_shift(jnp.int32(1), 15 - it)
            total = count_ge(ref, (trial + I16_MIN).astype(I16))
            ok = total >= k
            return jnp.where(ok, trial, cand), jnp.where(ok, total, cnt)
        return lax.fori_loop(0, 16, bit_body, (jnp.zeros((1, tq), I32), jnp.zeros((1, tq), F32)))

    raw_hi, n_ge_hi = kth_largest(hi_ref, float(topk))
    cand_hi = jnp.maximum(raw_hi, 1)
    p16 = jnp.broadcast_to((cand_hi + I16_MIN).astype(I16), (SEL_ROWS, tq))
    n_above = count_ge(hi_ref, (jnp.minimum(cand_hi + 1, 2 ** 16 - 1) + I16_MIN).astype(I16))
    need = float(topk) - n_above

    def tie_body(kt, _):
        s0 = pl.multiple_of(kt * tk, tk)
        for c in range(n_ch):
            ds = pl.ds(s0 + c * SEL_ROWS, SEL_ROWS)
            lq_ref[ds, :] = jnp.where(hi_ref[ds, :] == p16, lo_ref[ds, :], jnp.full((), I16_MIN, I16))
        return 0

    lax.fori_loop(0, n_kt, tie_body, 0)
    cand_lo, n_ge_lo = kth_largest(lq_ref, need)
    q16_row = (cand_lo + I16_MIN).astype(I16)
    q16 = jnp.broadcast_to(q16_row, (SEL_ROWS, tq))

    def sel_body(kt, _):
        s0 = pl.multiple_of(kt * tk, tk)
        for c in range(n_ch):
            ds = pl.ds(s0 + c * SEL_ROWS, SEL_ROWS)
            hi = hi_ref[ds, :]
            tie = jnp.where(hi == p16, jnp.where(lo_ref[ds, :] >= q16, one, zero), zero)
            selb_ref[ds, :] = jnp.where(hi > p16, one, tie)
        return 0

    lax.fori_loop(0, n_kt, sel_body, 0)

    n_gt = jnp.where(cand_lo >= 2 ** 16 - 1, 0.0,
                     count_ge(lq_ref, (jnp.minimum(cand_lo + 1, 2 ** 16 - 1) + I16_MIN).astype(I16)))
    n_with_hi = jnp.where(raw_hi > 0, n_ge_hi - n_above, 0.0)
    n_ge = jnp.where(cand_lo > 0, n_ge_lo, n_with_hi)
    quota = need - n_gt
    any_tied = jnp.max(jnp.where(n_ge - n_gt > quota, 1.0, 0.0)) > 0.0

    @pl.when(any_tied)
    def _():
        p_t = jnp.broadcast_to(p16[0:1, :], (tk, tq))
        q_t = jnp.broadcast_to(q16_row, (tk, tq))
        lower = jnp.where(lax.broadcasted_iota(I32, (tk, tk), 0) >= lax.broadcasted_iota(I32, (tk, tk), 1),
                          1.0, 0.0).astype(BF16)

        def exact_ties(kt):
            ds = pl.ds(pl.multiple_of(kt * tk, tk), tk)
            exact = jnp.where(hi_ref[ds, :] == p_t, jnp.where(lo_ref[ds, :] == q_t, one, zero), zero)
            return ds, exact, _dot(lower, exact)

        def drop(ds, exact, rank):
            selb_ref[ds, :] = selb_ref[ds, :] - jnp.where(rank > quota, 1.0, 0.0).astype(BF16) * exact

        def drop_pair(j, seen):
            ds_a, exact_a, in_a = exact_ties(2 * j)
            ds_b, exact_b, in_b = exact_ties(2 * j + 1)
            rank_a = in_a + seen
            rank_b = in_b + rank_a[tk - 1:tk, :]
            drop(ds_a, exact_a, rank_a)
            drop(ds_b, exact_b, rank_b)
            return rank_b[tk - 1:tk, :]

        seen = lax.fori_loop(0, n_kt // 2, drop_pair, jnp.zeros((1, tq), F32))

        @pl.when(n_kt % 2 == 1)
        def _():
            ds, exact, in_tile = exact_ties(n_kt - 1)
            drop(ds, exact, in_tile + seen)

    def att_keys(kt):
        s0 = pl.multiple_of(kt * tk, tk)
        return jnp.concatenate([k_ref[0, pl.ds(s0, tk), :], pf_ref[pl.ds(s0, tk), :]], axis=1)

    def att_vals(kt):
        s0 = pl.multiple_of(kt * tk, tk)
        return jnp.concatenate([v_ref[0, pl.ds(s0, tk), :], vf_ref[...]], axis=1)

    def att_mask(kt):
        s0 = pl.multiple_of(kt * tk, tk)
        return selb_ref[pl.ds(s0, tk), :].astype(F32).T > 0.5

    _flash_attention(q4_ref, acc_ref, p_ref, a_ref, n_kt, att_keys, att_vals, tq, mask_fn=att_mask)
    for h in range(N_HEADS):
        rows = slice(h * tq, (h + 1) * tq)
        o = acc_ref[rows, 0:HEAD_DIM] / jnp.maximum(acc_ref[rows, HEAD_DIM:HEAD_DIM + 1], 1e-30)
        o_ref[0, :, h * HEAD_DIM:(h + 1) * HEAD_DIM] = o.astype(BF16)


def _dsa_mixer(proj):
    b, s, _ = proj.shape
    tq = min(ATT_TQ, s)
    tk = min(ATT_TK, s)
    full = lambda off: pl.BlockSpec((1, s, LANES), functools.partial(lambda c, bi, i: (bi, 0, c), off // LANES))
    return pl.pallas_call(
        _dsa_kernel,
        out_shape=jax.ShapeDtypeStruct((b, s, W_MIX), BF16),
        grid=(b, s // tq),
        in_specs=[
            pl.BlockSpec((1, tq, W_MIX), lambda bi, i: (bi, i, _PACK_OFF['d_q'] // W_MIX)),
            pl.BlockSpec((1, tq, W_MIX), lambda bi, i: (bi, i, _PACK_OFF['i_q'] // W_MIX)),
            pl.BlockSpec((1, tq, LANES), lambda bi, i: (bi, i, SMALL_OFF // LANES)),
            full(SMALL_OFF), full(_PACK_OFF['d_k']), full(_PACK_OFF['d_v']),
            _resident((s, LANES)), _resident((N_HEADS * tq, LANES)), _resident((tk, LANES)),
        ],
        out_specs=pl.BlockSpec((1, tq, W_MIX), lambda bi, i: (bi, i, 0)),
        scratch_shapes=[pltpu.VMEM((N_HEADS * tq, 2 * HEAD_DIM), BF16),
                        pltpu.VMEM((N_HEADS * tq, 2 * HEAD_DIM), F32),
                        pltpu.VMEM((s, tq), I16), pltpu.VMEM((s, tq), I16), pltpu.VMEM((s, tq), I16),
                        pltpu.VMEM((s, tq), BF16),
                        pltpu.VMEM((N_HEADS * tq, tk), BF16),
                        pltpu.VMEM((N_HEADS * tq, tk), F32)],
        compiler_params=_cparams(("parallel", "arbitrary")),
        name="dsa_attention",
    )(proj, proj, proj, proj, proj, proj,
      _pos_features(s), _slope_features(SLOPES_DSA, tq), _ones_feature(tk))


OUTPROJ_TM = 512
LN_ROWS = 256


def _outproj_kernel(y0_ref, y1_ref, y2_ref, y3_ref, w_ref, x_ref, g_ref, b_ref, o_ref, ob_ref):
    for r0 in range(0, x_ref.shape[0], LN_ROWS):
        rows = slice(r0, r0 + LN_ROWS)
        acc = DN_ALPHA * x_ref[rows, :]
        for gi, y_ref in enumerate((y0_ref, y1_ref, y2_ref, y3_ref)):
            acc = acc + _dot(y_ref[rows, :], w_ref[gi * W_MIX:(gi + 1) * W_MIX, :])
        y = _layernorm(acc, g_ref[...], b_ref[...])
        o_ref[rows, :] = y
        ob_ref[rows, :] = y.astype(BF16)


def _outproj_ln(ys, w_out, x, g, b, l):
    n_tok, d = x.shape
    tm = OUTPROJ_TM
    yspec = pl.BlockSpec((tm, W_MIX), lambda i: (i, 0))
    xspec = pl.BlockSpec((tm, d), lambda i: (i, 0))
    return pl.pallas_call(
        _outproj_kernel,
        out_shape=(jax.ShapeDtypeStruct((n_tok, d), F32), jax.ShapeDtypeStruct((n_tok, d), BF16)),
        grid=(n_tok // tm,),
        in_specs=[yspec, yspec, yspec, yspec, _of_layer(w_out, l), xspec, _of_layer(g, l), _of_layer(b, l)],
        out_specs=(xspec, xspec),
        compiler_params=_cparams(("parallel",)),
        name="outproj_ln",
    )(*ys, w_out, x, g, b)


FFN_UP_TM = 1024
FFN_TF = 512
FFN_DOWN_TM = 256


def _ffn_up_kernel(xb_ref, wg_ref, wu_ref, h_ref, wgb_ref, wub_ref):
    @pl.when(pl.program_id(1) == 0)
    def _():
        wgb_ref[...] = wg_ref[...].astype(BF16)
        wub_ref[...] = wu_ref[...].astype(BF16)

    xb = xb_ref[...]
    gate = _dot(xb, wgb_ref[...])
    up = _dot(xb, wub_ref[...])
    h_ref[...] = (gate * jax.nn.sigmoid(gate) * up).astype(BF16)


def _ffn_down_kernel(h_ref, wd_ref, x_ref, g_ref, b_ref, *rest):
    *out_refs, acc_a, acc_b = rest
    i = pl.program_id(0)

    @pl.when(i == 0)
    def _():
        acc_b[...] = jnp.zeros(acc_b.shape, F32)

    def step(acc_prev, acc_next):
        y = _layernorm(acc_prev[...], g_ref[...], b_ref[...])
        out_refs[0][...] = y
        if len(out_refs) > 1:
            out_refs[1][...] = y.astype(BF16)
        acc_next[...] = DN_ALPHA * x_ref[...] + _dot(h_ref[...], wd_ref[...])

    @pl.when(i % 2 == 0)
    def _():
        step(acc_b, acc_a)

    @pl.when(i % 2 == 1)
    def _():
        step(acc_a, acc_b)


def _ffn_ln(xb, w_gate_up, w_down, x, g, b, l, want_bf16):
    n_tok, d = x.shape
    d_ff = w_down.shape[1]
    tm, tf = FFN_UP_TM, FFN_TF
    nf = d_ff // tf
    h = pl.pallas_call(
        _ffn_up_kernel,
        out_shape=jax.ShapeDtypeStruct((n_tok, d_ff), BF16),
        grid=(nf, n_tok // tm),
        in_specs=[
            pl.BlockSpec((tm, d), lambda f, i: (i, 0)),
            pl.BlockSpec((None, d, tf), lambda f, i: (l, 0, f)),
            pl.BlockSpec((None, d, tf), lambda f, i: (l, 0, f + nf)),
        ],
        out_specs=pl.BlockSpec((tm, tf), lambda f, i: (i, f)),
        scratch_shapes=[pltpu.VMEM((d, tf), BF16), pltpu.VMEM((d, tf), BF16)],
        compiler_params=_cparams(("parallel", "arbitrary")),
        name="ffn_up",
    )(xb, w_gate_up, w_gate_up)
    tm = FFN_DOWN_TM
    n_tiles = n_tok // tm
    in_row = lambda i: (jnp.minimum(i, n_tiles - 1), 0)
    out_row = lambda i: (jnp.maximum(i - 1, 0), 0)
    out_dtypes = (F32, BF16) if want_bf16 else (F32,)
    outs = pl.pallas_call(
        _ffn_down_kernel,
        out_shape=tuple(jax.ShapeDtypeStruct((n_tok, d), t) for t in out_dtypes),
        grid=(n_tiles + 1,),
        in_specs=[pl.BlockSpec((tm, d_ff), in_row), _of_layer(w_down, l),
                  pl.BlockSpec((tm, d), in_row), _of_layer(g, l), _of_layer(b, l)],
        out_specs=tuple(pl.BlockSpec((tm, d), out_row) for _ in out_dtypes),
        scratch_shapes=[pltpu.VMEM((tm, d), F32), pltpu.VMEM((tm, d), F32)],
        compiler_params=_cparams(("arbitrary",)),
        name="ffn_down_ln",
    )(h, w_down, x, g, b)
    return outs if want_bf16 else (outs[0], None)


PACK_ROWS = 256
_FOLD = {'d_q': ATTN_SCALE * LOG2E, 'n_q': ATTN_SCALE * LOG2E, 'i_w': IDX_SCALE}


def _pack_kernel(w_ref, o_ref):
    small = [n for n in _PACK_ORDER if _SRC_SIZE[n] % LANES]
    for l in range(w_ref.shape[1]):
        for n in _PACK_ORDER:
            if n not in small:
                src, dst, size = _SRC_OFF[n], _PACK_OFF[n], _SRC_SIZE[n]
                seg = w_ref[src:src + size, l, :] * _FOLD.get(n, 1.0)
                o_ref[l, :, dst:dst + size] = seg.T.astype(BF16)
        parts = [w_ref[_SRC_OFF[n]:_SRC_OFF[n] + _SRC_SIZE[n], l, :] * _FOLD.get(n, 1.0) for n in small]
        parts.append(jnp.zeros((LANES - sum(_SRC_SIZE[n] for n in small), w_ref.shape[2]), F32))
        o_ref[l, :, SMALL_OFF:SMALL_OFF + LANES] = jnp.concatenate(parts, axis=0).T.astype(BF16)


def _pack_w_in(w):
    n_layers, d, d_in = w.shape
    assert D_PACK == SMALL_OFF + LANES and [n for n in _PACK_ORDER if _SRC_SIZE[n] % LANES] == list(_PACK_ORDER[-3:])
    return pl.pallas_call(
        _pack_kernel,
        out_shape=jax.ShapeDtypeStruct((n_layers, d, D_PACK), BF16),
        grid=(d // PACK_ROWS,),
        in_specs=[pl.BlockSpec((d_in, n_layers, PACK_ROWS), lambda i: (0, 0, i))],
        out_specs=pl.BlockSpec((n_layers, PACK_ROWS, D_PACK), lambda i: (0, i, 0)),
        compiler_params=_cparams(("parallel",)),
        name="pack_w_in",
    )(jnp.transpose(w, (2, 0, 1)))


def _prepare(p):
    row = lambda v: v[:, None, :].astype(F32)
    n_layers = p['w_in'].shape[0]
    flat = lambda v: v.reshape(n_layers, 1, -1).astype(BF16)
    return dict(
        w_pack=_pack_w_in(p['w_in']), w_out=p['w_out'].astype(BF16),
        pool_w=p['pool_w'].astype(BF16), pool_scale=row(p['pool_scale']),
        conv_w=jnp.concatenate([p['conv_w'].astype(F32), jnp.zeros((n_layers, 1, W_MIX), F32)], axis=1),
        conv_b=row(p['conv_b']), conv_ln_g=row(p['conv_ln_g']), conv_ln_b=row(p['conv_ln_b']),
        conv_pw_w=p['conv_pw_w'].astype(BF16), conv_pw_b=row(p['conv_pw_b']),
        cmp_pos_k=flat(p['cmp_pos_k']), cmp_pos_v=flat(p['cmp_pos_v']),
        cmp_k_w1=p['cmp_k_w1'].astype(BF16), cmp_k_w2=p['cmp_k_w2'].astype(BF16),
        cmp_v_w1=p['cmp_v_w1'].astype(BF16), cmp_v_w2=p['cmp_v_w2'].astype(BF16),
        ln1_g=row(p['ln1_g']), ln1_b=row(p['ln1_b']), ln2_g=row(p['ln2_g']), ln2_b=row(p['ln2_b']),
        w_gate_up=p['w_gate_up'], w_down=p['w_down'].astype(BF16))


def _layer(x, xb, p, l, last):
    b, s, d = x.shape
    n_tok = b * s
    x_in = x if xb is None else xb
    proj, ck, cv = _inproj(x_in.reshape(n_tok, d), p['w_pack'], l)
    proj = proj.reshape(b, s, D_PACK)
    y_pool = _pool_mixer(proj, p['pool_w'], p['pool_scale'], l)
    y_conv = _conv_mixer(proj, p['conv_w'], p['conv_b'], p['conv_ln_g'], p['conv_ln_b'],
                         p['conv_pw_w'], p['conv_pw_b'], l)
    y_dsa = _dsa_mixer(proj)
    k_cmp = _compress(ck.reshape(b, s, HEAD_DIM), p['cmp_pos_k'], p['cmp_k_w1'], p['cmp_k_w2'], l)
    v_cmp = _compress(cv.reshape(b, s, HEAD_DIM), p['cmp_pos_v'], p['cmp_v_w1'], p['cmp_v_w2'], l)
    y_nsa = _nsa_mixer(proj, k_cmp, v_cmp)
    ys = [y.reshape(n_tok, W_MIX) for y in (y_pool, y_conv, y_dsa, y_nsa)]
    x1, x1b = _outproj_ln(ys, p['w_out'], x.reshape(n_tok, d), p['ln1_g'], p['ln1_b'], l)
    x2, x2b = _ffn_ln(x1b, p['w_gate_up'], p['w_down'], x1, p['ln2_g'], p['ln2_b'], l,
                      want_bf16=not last)
    return x2.reshape(b, s, d), None if last else x2b.reshape(b, s, d)


def kernel(x, w_in, w_out, pool_w, pool_scale, conv_w, conv_b, conv_ln_g, conv_ln_b, conv_pw_w, conv_pw_b,
           cmp_pos_k, cmp_pos_v, cmp_k_w1, cmp_k_w2, cmp_v_w1, cmp_v_w2, ln1_g, ln1_b, ln2_g, ln2_b,
           w_gate_up, w_down):
    params = dict(w_in=w_in, w_out=w_out, pool_w=pool_w, pool_scale=pool_scale, conv_w=conv_w, conv_b=conv_b,
                  conv_ln_g=conv_ln_g, conv_ln_b=conv_ln_b, conv_pw_w=conv_pw_w, conv_pw_b=conv_pw_b,
                  cmp_pos_k=cmp_pos_k, cmp_pos_v=cmp_pos_v, cmp_k_w1=cmp_k_w1, cmp_k_w2=cmp_k_w2,
                  cmp_v_w1=cmp_v_w1, cmp_v_w2=cmp_v_w2, ln1_g=ln1_g, ln1_b=ln1_b, ln2_g=ln2_g, ln2_b=ln2_b,
                  w_gate_up=w_gate_up, w_down=w_down)
    prepared = _prepare(params)
    xb = None
    for l in range(w_in.shape[0]):
        x, xb = _layer(x, xb, prepared, l, last=l == w_in.shape[0] - 1)
    return x
```
